```python
import math
import jax
import jax.numpy as jnp
from jax import lax
import numpy as np


D_MODEL = 1024
BATCH = 2
SEQ = 16384
DEPTH = 2

N_META = 16
A_HEAD_DIM = 64
A_W = D_MODEL // 2
A_HEADS = A_W // A_HEAD_DIM
IDX_HEADS = 8
IDX_DIM = 64
TOPK_MAX = 256
B_HEAD_DIM = 64
B_HALF = B_HEAD_DIM // 2
B_W = D_MODEL // 4
B_HEADS = B_W // B_HEAD_DIM
C_HEAD_DIM = 64
C_W = D_MODEL // 4
C_HEADS = C_W // C_HEAD_DIM
RET_CHUNK = 128
ROPE_BASE = 10000.0
REL_BUCKETS = 32
REL_MAX_DIST = 128
D_FF = 4 * D_MODEL
Q_BLOCK = 128
EPS = 1e-6
IN_SIZES = (A_W, A_W, A_W, IDX_HEADS * IDX_DIM, IDX_DIM, IDX_HEADS, B_W, B_W, B_W, C_W, C_W, C_W, C_W)
IN_COLS = sum(IN_SIZES)

kernel_name = 'hybrid_dsa_diff_retention_block'


def _split_points():
    pts, acc = [], 0
    for s in IN_SIZES[:-1]:
        acc += s
        pts.append(acc)
    return pts


def rmsnorm(x, g):
    xf = x.astype(jnp.float32)
    y = xf * lax.rsqrt(jnp.mean(xf * xf, axis=-1, keepdims=True) + EPS)
    return (y * g.astype(jnp.float32)).astype(x.dtype)


def rel_bucket(dist):
    n = jnp.maximum(dist, 0)
    max_exact = REL_BUCKETS // 2
    nf = jnp.maximum(n, 1).astype(jnp.float32)
    large = max_exact + (jnp.log(nf / max_exact) / math.log(REL_MAX_DIST / max_exact)
                         * (REL_BUCKETS - max_exact)).astype(jnp.int32)
    large = jnp.minimum(large, REL_BUCKETS - 1)
    return jnp.where(n < max_exact, n, large)


def rotary(x, pos):
    d = x.shape[-1]
    inv = ROPE_BASE ** (-jnp.arange(0, d, 2, dtype=jnp.float32) / d)
    ang = pos.astype(jnp.float32)[:, None] * inv[None, :]
    cos = jnp.cos(ang)[None, :, None, :]
    sin = jnp.sin(ang)[None, :, None, :]
    xf = x.astype(jnp.float32)
    x1, x2 = xf[..., : d // 2], xf[..., d // 2:]
    return jnp.concatenate([x1 * cos - x2 * sin, x1 * sin + x2 * cos], axis=-1).astype(x.dtype)


def to_blocks(a, nblk):
    pad = nblk * Q_BLOCK - a.shape[1]
    a = jnp.pad(a, [(0, 0), (0, pad)] + [(0, 0)] * (a.ndim - 2))
    a = a.reshape((a.shape[0], nblk, Q_BLOCK) + a.shape[2:])
    return jnp.moveaxis(a, 1, 0)


def from_blocks(a, t_len):
    a = jnp.moveaxis(a, 0, 1)
    a = a.reshape((a.shape[0], -1) + a.shape[3:])
    return a[:, :t_len]


def sparse_indexer_attention(q, k, v, qi, ki, wi, bias_table, topk):
    bsz, t_len = q.shape[0], q.shape[1]
    nblk = -(-t_len // Q_BLOCK)
    key_pos = jnp.arange(t_len)
    scale = A_HEAD_DIM ** -0.5
    wi = wi.astype(jnp.float32) * (IDX_HEADS ** -0.5 * IDX_DIM ** -0.5)
    bidx = jnp.arange(bsz)[:, None]

    def body(xs):
        qb, qib, wib, start = xs
        qpos = start + jnp.arange(Q_BLOCK)
        causal = key_pos[None, :] <= qpos[:, None]
        dots = jax.nn.relu(jnp.einsum('bqhd,bsd->bqhs', qib, ki)).astype(jnp.float32)
        score = jnp.einsum('bqh,bqhs->bqs', wib, dots)
        score = jnp.where(causal[None], score, -jnp.inf)
        _, idx = lax.top_k(score, topk)
        flat = idx.reshape(bsz, Q_BLOCK * topk)
        kg = k[bidx, flat].reshape(bsz, Q_BLOCK, topk, A_HEADS, A_HEAD_DIM)
        vg = v[bidx, flat].reshape(bsz, Q_BLOCK, topk, A_HEADS, A_HEAD_DIM)
        dist = qpos[None, :, None] - idx
        bias = jnp.moveaxis(bias_table[rel_bucket(dist)], -1, 1).astype(jnp.float32)
        logits = jnp.einsum('bqhd,bqkhd->bhqk', qb, kg).astype(jnp.float32) * scale + bias
        logits = jnp.where((dist >= 0)[:, None], logits, -jnp.inf)
        p = jax.nn.softmax(logits, axis=-1).astype(v.dtype)
        return jnp.einsum('bhqk,bqkhd->bqhd', p, vg)

    starts = jnp.arange(nblk) * Q_BLOCK
    out = lax.map(body, (to_blocks(q, nblk), to_blocks(qi, nblk), to_blocks(wi, nblk), starts))
    return from_blocks(out, t_len)


def diff_attention(q, k, v, bias_table, lam):
    t_len = q.shape[1]
    nblk = -(-t_len // Q_BLOCK)
    key_pos = jnp.arange(t_len)
    scale = B_HALF ** -0.5

    def body(xs):
        qb, start = xs
        qpos = start + jnp.arange(Q_BLOCK)
        dist = qpos[:, None] - key_pos[None, :]
        bias = jnp.moveaxis(bias_table[rel_bucket(dist)], -1, 0).astype(jnp.float32)
        logits = jnp.einsum('bqhcd,bshcd->bchqs', qb, k).astype(jnp.float32) * scale + bias[None, None]
        logits = jnp.where(dist >= 0, logits, -jnp.inf)
        p = jax.nn.softmax(logits, axis=-1)
        w = (p[:, 0] - lam * p[:, 1]).astype(v.dtype)
        return jnp.einsum('bhqs,bshd->bqhd', w, v)

    starts = jnp.arange(nblk) * Q_BLOCK
    out = lax.map(body, (to_blocks(q, nblk), starts))
    return from_blocks(out, t_len)


def _retention_chunk(qc, kc, vc, state, log_gamma):
    L = qc.shape[1]
    i = jnp.arange(L, dtype=jnp.float32)
    diff = i[:, None] - i[None, :]
    dmat = jnp.where(diff >= 0, jnp.exp(log_gamma[:, None, None] * jnp.maximum(diff, 0.0)), 0.0)
    inner = jnp.einsum('bihd,bjhd->bhij', qc, kc) * dmat
    out = jnp.einsum('bhij,bjhe->bihe', inner, vc)
    q_decay = jnp.exp(log_gamma[None, :] * (i[:, None] + 1.0))
    out = out + jnp.einsum('bihd,bhde->bihe', qc, state) * q_decay[None, :, :, None]
    k_decay = jnp.exp(log_gamma[None, :] * (L - 1.0 - i[:, None]))
    state = state * jnp.exp(log_gamma * L)[None, :, None, None] + jnp.einsum('bjhd,jh,bjhe->bhde', kc, k_decay, vc)
    return out, state


def retention(q, k, v):
    bsz, t_len, nh, dk = q.shape
    dv = v.shape[-1]
    log_gamma = jnp.log1p(-jnp.exp2(-5.0 - jnp.arange(nh, dtype=jnp.float32)))
    qf = q.astype(jnp.float32)
    kf = k.astype(jnp.float32) * (dk ** -0.5)
    vf = v.astype(jnp.float32)
    state0 = jnp.zeros((bsz, nh, dk, dv), jnp.float32)
    out_meta, state = _retention_chunk(qf[:, :N_META], kf[:, :N_META], vf[:, :N_META], state0, log_gamma)
    nc = (t_len - N_META) // RET_CHUNK

    def to_chunks(a):
        return jnp.moveaxis(a[:, N_META:].reshape(bsz, nc, RET_CHUNK, nh, a.shape[-1]), 1, 0)

    def step(s, xs):
        qc, kc, vc = xs
        o, s = _retention_chunk(qc, kc, vc, s, log_gamma)
        return s, o

    _, outs = lax.scan(step, state, (to_chunks(qf), to_chunks(kf), to_chunks(vf)))
    out_real = jnp.moveaxis(outs, 0, 1).reshape(bsz, nc * RET_CHUNK, nh, dv)
    return jnp.concatenate([out_meta, out_real], axis=1)


def setup_inputs(seed: int = 0) -> dict:
    key = jax.random.key(seed)
    ks = jax.random.split(key, 16)

    def nrm(k, shape, scale):
        return jax.random.normal(k, shape, jnp.float32) * scale

    return {
        'x': nrm(ks[0], (BATCH, SEQ, D_MODEL), 1.0),
        'meta': nrm(ks[1], (N_META, D_MODEL), 1.0),
        'rel_bias': nrm(ks[2], (REL_BUCKETS, A_HEADS + B_HEADS), 0.5),
        'w_in': nrm(ks[3], (DEPTH, D_MODEL, IN_COLS), D_MODEL ** -0.5),
        'norm_mix': 1.0 + nrm(ks[4], (DEPTH, D_MODEL), 0.01),
        'diff_lambda': nrm(ks[5], (DEPTH, 4, B_HALF), 0.1),
        'diff_norm': 1.0 + nrm(ks[6], (DEPTH, B_HEAD_DIM), 0.01),
        'ret_norm': 1.0 + nrm(ks[7], (DEPTH, C_W), 0.01),
        'w_out': nrm(ks[8], (DEPTH, D_MODEL, D_MODEL), 0.5 * D_MODEL ** -0.5),
        'norm_ff': 1.0 + nrm(ks[9], (DEPTH, D_MODEL), 0.01),
        'w_ff1': nrm(ks[10], (DEPTH, D_MODEL, D_FF), D_MODEL ** -0.5),
        'w_ff2': nrm(ks[11], (DEPTH, D_FF, D_MODEL), 0.5 * D_FF ** -0.5),
        'final_norm': 1.0 + nrm(ks[12], (D_MODEL,), 0.01),
    }


def reference(x, meta, rel_bias, w_in, norm_mix, diff_lambda, diff_norm, ret_norm, w_out, norm_ff, w_ff1, w_ff2, final_norm):
    bsz, s_len, _ = x.shape
    t_len = s_len + N_META
    topk = min(TOPK_MAX, s_len // 4)
    h = jnp.concatenate([jnp.broadcast_to(meta.astype(x.dtype)[None], (bsz, N_META, D_MODEL)), x], axis=1)
    pos = jnp.arange(t_len)
    split_pts = _split_points()
    bias_a = rel_bias[:, :A_HEADS]
    bias_b = rel_bias[:, A_HEADS:]
    for l in range(DEPTH):
        u = rmsnorm(h, norm_mix[l])
        proj = u @ w_in[l]
        (qa, ka, va, qi, ki, wi, qb, kb, vb, qc, kc, vc, gc) = jnp.split(proj, split_pts, axis=-1)

        a_out = sparse_indexer_attention(
            qa.reshape(bsz, t_len, A_HEADS, A_HEAD_DIM),
            ka.reshape(bsz, t_len, A_HEADS, A_HEAD_DIM),
            va.reshape(bsz, t_len, A_HEADS, A_HEAD_DIM),
            qi.reshape(bsz, t_len, IDX_HEADS, IDX_DIM), ki, wi, bias_a, topk,
        ).reshape(bsz, t_len, A_W)

        lambda_init = 0.8 - 0.6 * math.exp(-0.3 * l)
        lp = diff_lambda[l].astype(jnp.float32)
        lam = jnp.exp(jnp.sum(lp[0] * lp[1])) - jnp.exp(jnp.sum(lp[2] * lp[3])) + lambda_init
        b_heads = diff_attention(
            qb.reshape(bsz, t_len, B_HEADS, 2, B_HALF),
            kb.reshape(bsz, t_len, B_HEADS, 2, B_HALF),
            vb.reshape(bsz, t_len, B_HEADS, B_HEAD_DIM), bias_b, lam,
        )
        b_out = (rmsnorm(b_heads, diff_norm[l]) * (1.0 - lambda_init)).reshape(bsz, t_len, B_W)

        qcr = rotary(qc.reshape(bsz, t_len, C_HEADS, C_HEAD_DIM), pos)
        kcr = rotary(kc.reshape(bsz, t_len, C_HEADS, C_HEAD_DIM), pos)
        ret = retention(qcr, kcr, vc.reshape(bsz, t_len, C_HEADS, C_HEAD_DIM)).astype(h.dtype)
        ret = rmsnorm(ret, ret_norm[l].reshape(C_HEADS, C_HEAD_DIM)).reshape(bsz, t_len, C_W)
        c_out = jax.nn.silu(gc) * ret

        h = h + jnp.concatenate([a_out, b_out, c_out], axis=-1) @ w_out[l]
        u = rmsnorm(h, norm_ff[l])
        h = h + jnp.square(jax.nn.relu(u @ w_ff1[l])) @ w_ff2[l]
    return rmsnorm(h, final_norm)[:, N_META:]
```

```python
import functools
import math

import numpy as np
import jax
import jax.numpy as jnp
from jax import lax
from jax.experimental import pallas as pl
from jax.experimental.pallas import tpu as pltpu

D_MODEL = 1024
N_META = 16
A_HEADS = 8
A_HEAD_DIM = 64
A_W = A_HEADS * A_HEAD_DIM
IDX_HEADS = 8
IDX_DIM = 64
TOPK_MAX = 256
B_HEADS = 4
B_HEAD_DIM = 64
B_HALF = 32
B_W = B_HEADS * B_HEAD_DIM
C_HEADS = 4
C_HEAD_DIM = 64
C_W = C_HEADS * C_HEAD_DIM
ROPE_BASE = 10000.0
REL_BUCKETS = 32
REL_MAX_DIST = 128
D_FF = 4 * D_MODEL
EPS = 1e-6
IN_SIZES = (A_W, A_W, A_W, IDX_HEADS * IDX_DIM, IDX_DIM, IDX_HEADS, B_W, B_W, B_W, C_W, C_W, C_W, C_W)

LANES = 128
ATT_BLK = 512
IDX_ROWS = 128
IDX_CHUNK = 512
RET_BLK = 256
ROW_TILE = 512
FF_CHUNK = 1024
VMEM_LIMIT_BYTES = 56 * 1024 * 1024

NEG = -1e30
INT_MIN = -(2 ** 31)

P16_COLS = 4 * A_W + 3 * B_W + 2 * IDX_DIM
P32_COLS = 4 * C_W + LANES

BF16 = jnp.bfloat16
F32 = jnp.float32


def _cparams(sem):
    return pltpu.CompilerParams(dimension_semantics=sem, vmem_limit_bytes=VMEM_LIMIT_BYTES)


def _dot_nt(a, b):
    return lax.dot_general(a, b, (((1,), (1,)), ((), ())), preferred_element_type=F32)


def _dot(a, b):
    return jnp.dot(a, b, preferred_element_type=F32)


def _inproj_kernel(h_ref, g_ref, w16_ref, w32_ref, o16_ref, o32_ref):
    x = h_ref[...]
    ms = jnp.mean(x * x, axis=-1, keepdims=True)
    u = ((x * lax.rsqrt(ms + EPS)) * g_ref[...]).astype(BF16)
    o16_ref[...] = _dot(u, w16_ref[...]).astype(BF16)
    o32_ref[...] = _dot(u, w32_ref[...])


def _inproj(h2d, g, w16, w32):
    rows = h2d.shape[0]
    return pl.pallas_call(
        _inproj_kernel,
        grid=(rows // ROW_TILE,),
        in_specs=[
            pl.BlockSpec((ROW_TILE, D_MODEL), lambda i: (i, 0)),
            pl.BlockSpec((1, D_MODEL), lambda i: (0, 0)),
            pl.BlockSpec((D_MODEL, P16_COLS), lambda i: (0, 0)),
            pl.BlockSpec((D_MODEL, P32_COLS), lambda i: (0, 0)),
        ],
        out_specs=[
            pl.BlockSpec((ROW_TILE, P16_COLS), lambda i: (i, 0)),
            pl.BlockSpec((ROW_TILE, P32_COLS), lambda i: (i, 0)),
        ],
        out_shape=[
            jax.ShapeDtypeStruct((rows, P16_COLS), BF16),
            jax.ShapeDtypeStruct((rows, P32_COLS), F32),
        ],
        compiler_params=_cparams(("arbitrary",)),
        name="inproj",
    )(h2d, g, w16, w32)


def _indexer_kernel(qi_ref, ki_ref, wi_ref, keys_ref, tau_ref, ntake_ref, tie_ref, qm_ref, *, topk, n_chunks):
    i = pl.program_id(1)
    row0 = i * IDX_ROWS
    n_live = (row0 + IDX_ROWS - 1) // IDX_CHUNK + 1

    lane = lax.broadcasted_iota(jnp.int32, (IDX_ROWS, LANES), 1)
    for h in range(IDX_HEADS):
        pair = qi_ref[:, (h // 2) * LANES:(h // 2 + 1) * LANES]
        keep = (lane >= IDX_DIM) if h % 2 else (lane < IDX_DIM)
        qm_ref[h] = jnp.where(keep, pair, jnp.zeros_like(pair))

    w = wi_ref[...] * (IDX_HEADS ** -0.5 * IDX_DIM ** -0.5)
    wcol = [w[:, h:h + 1] for h in range(IDX_HEADS)]
    row = row0 + lax.broadcasted_iota(jnp.int32, (IDX_ROWS, 1), 0)
    col_in_chunk = lax.broadcasted_iota(jnp.int32, (IDX_ROWS, IDX_CHUNK), 1)

    def score_chunk(c, carry):
        start = pl.multiple_of(c * IDX_CHUNK, IDX_CHUNK)
        kc = ki_ref[pl.ds(start, IDX_CHUNK), :]
        acc = jnp.zeros((IDX_ROWS, IDX_CHUNK), F32)
        for h in range(IDX_HEADS):
            acc = acc + wcol[h] * jnp.maximum(_dot_nt(qm_ref[h], kc), 0.0)
        bits = pltpu.bitcast(acc, jnp.int32)
        key = bits ^ ((bits >> 31) & jnp.int32(0x7FFFFFFF))
        key = jnp.where(col_in_chunk + start <= row, key, jnp.int32(INT_MIN))
        keys_ref[:, pl.ds(start, IDX_CHUNK)] = key
        return carry

    lax.fori_loop(0, n_live, score_chunk, 0)

    def fill_chunk(c, carry):
        start = pl.multiple_of(c * IDX_CHUNK, IDX_CHUNK)
        keys_ref[:, pl.ds(start, IDX_CHUNK)] = jnp.full((IDX_ROWS, IDX_CHUNK), INT_MIN, jnp.int32)
        return carry

    lax.fori_loop(n_live, n_chunks, fill_chunk, 0)

    def count(pred):
        def body(c, acc):
            start = pl.multiple_of(c * IDX_CHUNK, IDX_CHUNK)
            m = jnp.where(pred(keys_ref[:, pl.ds(start, IDX_CHUNK)]), 1.0, 0.0)
            for t in range(IDX_CHUNK // LANES):
                acc = acc + m[:, t * LANES:(t + 1) * LANES]
            return acc
        acc = lax.fori_loop(0, n_live, body, jnp.zeros((IDX_ROWS, LANES), F32))
        return jnp.sum(acc, axis=1, keepdims=True)

    def bit_step(it, ubits):
        one = jnp.left_shift(jnp.int32(1), 31 - it)
        cand_u = ubits | one
        cand = cand_u ^ jnp.int32(INT_MIN)
        cnt = count(lambda k: k >= cand)
        return jnp.where(cnt >= float(topk), cand_u, ubits)

    ubits = lax.fori_loop(0, 32, bit_step, jnp.zeros((IDX_ROWS, 1), jnp.int32))
    tau = jnp.maximum(ubits ^ jnp.int32(INT_MIN), jnp.int32(INT_MIN + 1))
    n_gt = count(lambda k: k > tau)
    n_ge = count(lambda k: k >= tau)
    tau_ref[...] = tau
    ntake_ref[...] = float(topk) - n_gt
    tie_ref[...] = (n_ge > float(topk)).astype(jnp.int32)


def _indexer(p16, p32, topk):
    bsz, tp, _ = p16.shape
    n_chunks = tp // IDX_CHUNK
    kern = functools.partial(_indexer_kernel, topk=topk, n_chunks=n_chunks)
    col = lambda n: pl.BlockSpec((None, IDX_ROWS, 1), lambda b, i: (b, i, 0))
    return pl.pallas_call(
        kern,
        grid=(bsz, tp // IDX_ROWS),
        in_specs=[
            pl.BlockSpec((None, IDX_ROWS, A_W), lambda b, i: (b, i, 3)),
            pl.BlockSpec((None, tp, LANES), lambda b, i: (b, 0, (P16_COLS - LANES) // LANES)),
            pl.BlockSpec((None, IDX_ROWS, LANES), lambda b, i: (b, i, 4 * C_W // LANES)),
        ],
        out_specs=[
            pl.BlockSpec((None, IDX_ROWS, tp), lambda b, i: (b, i, 0)),
            col(0), col(1), col(2),
        ],
        out_shape=[
            jax.ShapeDtypeStruct((bsz, tp, tp), jnp.int32),
            jax.ShapeDtypeStruct((bsz, tp, 1), jnp.int32),
            jax.ShapeDtypeStruct((bsz, tp, 1), F32),
            jax.ShapeDtypeStruct((bsz, tp, 1), jnp.int32),
        ],
        scratch_shapes=[pltpu.VMEM((IDX_HEADS, IDX_ROWS, LANES), BF16)],
        compiler_params=_cparams(("arbitrary", "arbitrary")),
        name="indexer",
    )(p16, p16, p32)


def _lane_masked_queries(q_ref, qm_ref, n_slots, width):
    per_group = LANES // width
    lane = lax.broadcasted_iota(jnp.int32, (ATT_BLK, LANES), 1)
    for s in range(n_slots):
        g, r = divmod(s, per_group)
        grp = q_ref[:, g * LANES:(g + 1) * LANES]
        keep = (lane >= r * width) & (lane < (r + 1) * width)
        qm_ref[s] = jnp.where(keep, grp, jnp.zeros_like(grp))


def _flash_update(s, logit, v_grp, m_ref, l_ref, acc_ref):
    m_prev = m_ref[s]
    m_new = jnp.maximum(m_prev, jnp.max(logit, axis=1, keepdims=True))
    alpha = jnp.exp(m_prev - m_new)
    p = jnp.exp(logit - m_new)
    l_ref[s] = alpha * l_ref[s] + jnp.sum(p, axis=1, keepdims=True)
    acc_ref[s] = alpha * acc_ref[s] + _dot(p.astype(BF16), v_grp)
    m_ref[s] = m_new


def _flash_init(m_ref, l_ref, acc_ref):
    m_ref[...] = jnp.full(m_ref.shape, NEG, F32)
    l_ref[...] = jnp.zeros(l_ref.shape, F32)
    acc_ref[...] = jnp.zeros(acc_ref.shape, F32)


def _triangle_steps(nq):
    qs, ks, kinds = [], [], []
    for q in range(nq):
        for k in range(q + 1):
            qs.append(q)
            ks.append(k)
            kinds.append(2 if k == q else (1 if k == q - 1 else 0))
    return (jnp.asarray(qs, jnp.int32), jnp.asarray(ks, jnp.int32), jnp.asarray(kinds, jnp.int32))


def _sparse_attn_kernel(qmap, kmap, kind, tieflag, q_ref, k_ref, v_ref, keys_ref, tau_ref, ntake_ref, bias_ref,
                        o_ref, qm_ref, m_ref, l_ref, acc_ref, mask_ref, carry_ref, *, nq):
    b = pl.program_id(0)
    s = pl.program_id(1)
    qb = qmap[s]
    kb = kmap[s]

    @pl.when(kb == 0)
    def _():
        _lane_masked_queries(q_ref, qm_ref, A_HEADS, A_HEAD_DIM)
        _flash_init(m_ref, l_ref, acc_ref)
        carry_ref[...] = jnp.zeros(carry_ref.shape, F32)

    has_ties = tieflag[b * nq + qb]

    @pl.when(has_ties == 0)
    def _():
        mask_ref[...] = jnp.where(keys_ref[...] >= tau_ref[...], 0.0, NEG)

    @pl.when(has_ties != 0)
    def _():
        key = keys_ref[...]
        tau = tau_ref[...]
        eq = key == tau
        eq_b = jnp.where(eq, 1.0, 0.0).astype(BF16)
        r = lax.broadcasted_iota(jnp.int32, (ATT_BLK, ATT_BLK), 0)
        c = lax.broadcasted_iota(jnp.int32, (ATT_BLK, ATT_BLK), 1)
        before = jnp.where(r < c, 1.0, 0.0).astype(BF16)
        seen = carry_ref[...] + _dot(eq_b, before)
        keep = (key > tau) | (eq & (seen < ntake_ref[...]))
        mask_ref[...] = jnp.where(keep, 0.0, NEG)
        carry_ref[...] = carry_ref[...] + jnp.sum(eq_b.astype(F32), axis=1, keepdims=True)

    scale = A_HEAD_DIM ** -0.5
    for h in range(A_HEADS):
        g = h // 2
        logit = _dot_nt(qm_ref[h], k_ref[:, g * LANES:(g + 1) * LANES]) * scale + bias_ref[h] + mask_ref[...]
        _flash_update(h, logit, v_ref[:, g * LANES:(g + 1) * LANES], m_ref, l_ref, acc_ref)

    @pl.when(kb == qb)
    def _():
        lane = lax.broadcasted_iota(jnp.int32, (ATT_BLK, LANES), 1)
        for g in range(A_HEADS // 2):
            lo = acc_ref[2 * g] / l_ref[2 * g]
            hi = acc_ref[2 * g + 1] / l_ref[2 * g + 1]
            o_ref[:, g * LANES:(g + 1) * LANES] = jnp.where(lane < A_HEAD_DIM, lo, hi).astype(o_ref.dtype)


def _sparse_attn(p16, keys, tau, ntake, tieflag, bias_tiles):
    bsz, tp, _ = p16.shape
    nq = tp // ATT_BLK
    qmap, kmap, kind = _triangle_steps(nq)
    qblk = lambda c: pl.BlockSpec((None, ATT_BLK, A_W), lambda b, s, qm, km, kd, tf: (b, qm[s], c))
    kblk = lambda c: pl.BlockSpec((None, ATT_BLK, A_W), lambda b, s, qm, km, kd, tf: (b, km[s], c))
    colblk = pl.BlockSpec((None, ATT_BLK, 1), lambda b, s, qm, km, kd, tf: (b, qm[s], 0))
    grid_spec = pltpu.PrefetchScalarGridSpec(
        num_scalar_prefetch=4,
        grid=(bsz, int(qmap.shape[0])),
        in_specs=[
            qblk(0), kblk(1), kblk(2),
            pl.BlockSpec((None, ATT_BLK, ATT_BLK), lambda b, s, qm, km, kd, tf: (b, qm[s], km[s])),
            colblk, colblk,
            pl.BlockSpec((None, A_HEADS, ATT_BLK, ATT_BLK), lambda b, s, qm, km, kd, tf: (kd[s], 0, 0, 0)),
        ],
        out_specs=pl.BlockSpec((None, ATT_BLK, A_W), lambda b, s, qm, km, kd, tf: (b, qm[s], 0)),
        scratch_shapes=[
            pltpu.VMEM((A_HEADS, ATT_BLK, LANES), BF16),
            pltpu.VMEM((A_HEADS, ATT_BLK, 1), F32),
            pltpu.VMEM((A_HEADS, ATT_BLK, 1), F32),
            pltpu.VMEM((A_HEADS, ATT_BLK, LANES), F32),
            pltpu.VMEM((ATT_BLK, ATT_BLK), F32),
            pltpu.VMEM((ATT_BLK, 1), F32),
        ],
    )
    return pl.pallas_call(
        functools.partial(_sparse_attn_kernel, nq=nq),
        grid_spec=grid_spec,
        out_shape=jax.ShapeDtypeStruct((bsz, tp, A_W), BF16),
        compiler_params=_cparams(("arbitrary", "arbitrary")),
        name="sparse_attn",
    )(qmap, kmap, kind, tieflag, p16, p16, p16, keys, tau, ntake, bias_tiles)


def _diff_attn_kernel(qmap, kmap, kind, lam_ref, q_ref, k_ref, v_ref, bias_ref, g_ref,
                      o_ref, qm_ref, m_ref, l_ref, acc_ref, *, out_scale):
    s = pl.program_id(1)
    qb = qmap[s]
    kb = kmap[s]
    n_slots = 2 * B_HEADS

    @pl.when(kb == 0)
    def _():
        _lane_masked_queries(q_ref, qm_ref, n_slots, B_HALF)
        _flash_init(m_ref, l_ref, acc_ref)

    scale = B_HALF ** -0.5
    for slot in range(n_slots):
        h = slot // 2
        g = h // 2
        logit = _dot_nt(qm_ref[slot], k_ref[:, g * LANES:(g + 1) * LANES]) * scale + bias_ref[h]
        _flash_update(slot, logit, v_ref[:, g * LANES:(g + 1) * LANES], m_ref, l_ref, acc_ref)

    @pl.when(kb == qb)
    def _():
        lam = lam_ref[0]
        lane = lax.broadcasted_iota(jnp.int32, (ATT_BLK, LANES), 1)
        low = lane < B_HEAD_DIM
        for g in range(B_HEADS // 2):
            heads = []
            for h in (2 * g, 2 * g + 1):
                heads.append(acc_ref[2 * h] / l_ref[2 * h] - lam * (acc_ref[2 * h + 1] / l_ref[2 * h + 1]))
            x = jnp.where(low, heads[0], heads[1])
            sq = x * x
            ss_lo = jnp.sum(jnp.where(low, sq, 0.0), axis=1, keepdims=True)
            ss_hi = jnp.sum(jnp.where(low, 0.0, sq), axis=1, keepdims=True)
            ms = jnp.where(low, ss_lo, ss_hi) * (1.0 / B_HEAD_DIM)
            y = (x * lax.rsqrt(ms + EPS)) * g_ref[...]
            o_ref[:, g * LANES:(g + 1) * LANES] = (y * out_scale).astype(o_ref.dtype)


def _diff_attn(p16, lam, bias_tiles, g_pair, out_scale):
    bsz, tp, _ = p16.shape
    nq = tp // ATT_BLK
    qmap, kmap, kind = _triangle_steps(nq)
    base = 4 * A_W // B_W
    qblk = pl.BlockSpec((None, ATT_BLK, B_W), lambda b, s, qm, km, kd, lm: (b, qm[s], base))
    kblk = lambda c: pl.BlockSpec((None, ATT_BLK, B_W), lambda b, s, qm, km, kd, lm: (b, km[s], base + c))
    n_slots = 2 * B_HEADS
    grid_spec = pltpu.PrefetchScalarGridSpec(
        num_scalar_prefetch=4,
        grid=(bsz, int(qmap.shape[0])),
        in_specs=[
            qblk, kblk(1), kblk(2),
            pl.BlockSpec((None, B_HEADS, ATT_BLK, ATT_BLK), lambda b, s, qm, km, kd, lm: (kd[s], 0, 0, 0)),
            pl.BlockSpec((1, LANES), lambda b, s, qm, km, kd, lm: (0, 0)),
        ],
        out_specs=pl.BlockSpec((None, ATT_BLK, B_W), lambda b, s, qm, km, kd, lm: (b, qm[s], 0)),
        scratch_shapes=[
            pltpu.VMEM((n_slots, ATT_BLK, LANES), BF16),
            pltpu.VMEM((n_slots, ATT_BLK, 1), F32),
            pltpu.VMEM((n_slots, ATT_BLK, 1), F32),
            pltpu.VMEM((n_slots, ATT_BLK, LANES), F32),
        ],
    )
    return pl.pallas_call(
        functools.partial(_diff_attn_kernel, out_scale=out_scale),
        grid_spec=grid_spec,
        out_shape=jax.ShapeDtypeStruct((bsz, tp, B_W), BF16),
        compiler_params=_cparams(("arbitrary", "arbitrary")),
        name="diff_attn",
    )(qmap, kmap, kind, lam, p16, p16, p16, bias_tiles, g_pair)


def _retention_kernel(q_ref, k_ref, v_ref, gate_ref, cos_ref, sin_ref, dmat_ref, qdec_ref, kdec_ref, sdec_ref,
                      g_ref, o_ref, state_ref):
    c = pl.program_id(1)

    @pl.when(c == 0)
    def _():
        state_ref[...] = jnp.zeros(state_ref.shape, F32)

    lane = lax.broadcasted_iota(jnp.int32, (RET_BLK, C_W), 1)
    first_half = (lane % C_HEAD_DIM) < (C_HEAD_DIM // 2)

    def rope(x):
        swapped = jnp.where(first_half,
                            pltpu.roll(x, C_W - C_HEAD_DIM // 2, axis=1),
                            pltpu.roll(x, C_HEAD_DIM // 2, axis=1))
        return x * cos_ref[...] + swapped * sin_ref[...]

    q = rope(q_ref[...])
    k = rope(k_ref[...]) * (C_HEAD_DIM ** -0.5)
    q16 = q.astype(BF16)
    k16 = k.astype(BF16)
    v16 = v_ref[...].astype(BF16)

    state = state_ref[...]
    out = _dot(q16, state.astype(BF16)) * qdec_ref[...]
    for h in range(C_HEADS):
        in_head = (lane >= h * C_HEAD_DIM) & (lane < (h + 1) * C_HEAD_DIM)
        qh = jnp.where(in_head, q, 0.0).astype(BF16)
        inner = _dot_nt(qh, k16) * dmat_ref[h]
        out = out + jnp.where(in_head, _dot(inner.astype(BF16), v16), 0.0)

    kd = (k * kdec_ref[...]).astype(BF16)
    update = _dot(kd.T, v16)
    r = lax.broadcasted_iota(jnp.int32, (C_W, C_W), 0) // C_HEAD_DIM
    cc = lax.broadcasted_iota(jnp.int32, (C_W, C_W), 1) // C_HEAD_DIM
    state_ref[...] = state * sdec_ref[...] + jnp.where(r == cc, update, 0.0)

    sq = out * out
    ms = jnp.zeros_like(out)
    for h in range(C_HEADS):
        in_head = (lane >= h * C_HEAD_DIM) & (lane < (h + 1) * C_HEAD_DIM)
        ss = jnp.sum(jnp.where(in_head, sq, 0.0), axis=1, keepdims=True)
        ms = jnp.where(in_head, ss * (1.0 / C_HEAD_DIM), ms)
    normed = (out * lax.rsqrt(ms + EPS)) * g_ref[...]
    gate = gate_ref[...]
    o_ref[...] = ((gate * jax.nn.sigmoid(gate)) * normed).astype(o_ref.dtype)


def _retention(p32, cos, sin, dmat, qdec, kdec, sdec, g):
    bsz, tp, _ = p32.shape
    blk = lambda c: pl.BlockSpec((None, RET_BLK, C_W), lambda b, i: (b, i, c))
    const = lambda shape: pl.BlockSpec(shape, lambda b, i: (0,) * len(shape))
    return pl.pallas_call(
        _retention_kernel,
        grid=(bsz, tp // RET_BLK),
        in_specs=[
            blk(0), blk(1), blk(2), blk(3),
            pl.BlockSpec((RET_BLK, C_W), lambda b, i: (i, 0)),
            pl.BlockSpec((RET_BLK, C_W), lambda b, i: (i, 0)),
            const((C_HEADS, RET_BLK, RET_BLK)),
            const((RET_BLK, C_W)), const((RET_BLK, C_W)), const((1, C_W)), const((1, C_W)),
        ],
        out_specs=pl.BlockSpec((None, RET_BLK, C_W), lambda b, i: (b, i, 0)),
        out_shape=jax.ShapeDtypeStruct((bsz, tp, C_W), BF16),
        scratch_shapes=[pltpu.VMEM((C_W, C_W), F32)],
        compiler_params=_cparams(("arbitrary", "arbitrary")),
        name="retention",
    )(p32, p32, p32, p32, cos, sin, dmat, qdec, kdec, sdec, g)


def _mix_mlp_kernel(h_ref, a_ref, b_ref, c_ref, wo_ref, g_ref, w1_ref, w2_ref, gf_ref, o_ref, u_ref, acc_ref,
                    *, final_norm):
    j = pl.program_id(1)

    @pl.when(j == 0)
    def _():
        mixed = (_dot(a_ref[...], wo_ref[0:A_W, :])
                 + _dot(b_ref[...], wo_ref[A_W:A_W + B_W, :])
                 + _dot(c_ref[...], wo_ref[A_W + B_W:, :]))
        h1 = h_ref[...] + mixed
        acc_ref[...] = h1
        ms = jnp.mean(h1 * h1, axis=-1, keepdims=True)
        u_ref[...] = ((h1 * lax.rsqrt(ms + EPS)) * g_ref[...]).astype(BF16)

    t = jnp.maximum(_dot(u_ref[...], w1_ref[...]), 0.0)
    acc_ref[...] += _dot((t * t).astype(BF16), w2_ref[...])

    @pl.when(j == pl.num_programs(1) - 1)
    def _():
        y = acc_ref[...]
        if final_norm:
            ms = jnp.mean(y * y, axis=-1, keepdims=True)
            y = (y * lax.rsqrt(ms + EPS)) * gf_ref[...]
        o_ref[...] = y


def _mix_mlp(h2d, a, b, c, wo, g, w1, w2, gf, final_norm):
    rows = h2d.shape[0]
    row_blk = lambda w: pl.BlockSpec((ROW_TILE, w), lambda i, j: (i, 0))
    vec = pl.BlockSpec((1, D_MODEL), lambda i, j: (0, 0))
    return pl.pallas_call(
        functools.partial(_mix_mlp_kernel, final_norm=final_norm),
        grid=(rows // ROW_TILE, D_FF // FF_CHUNK),
        in_specs=[
            row_blk(D_MODEL), row_blk(A_W), row_blk(B_W), row_blk(C_W),
            pl.BlockSpec((D_MODEL, D_MODEL), lambda i, j: (0, 0)),
            vec,
            pl.BlockSpec((D_MODEL, FF_CHUNK), lambda i, j: (0, j)),
            pl.BlockSpec((FF_CHUNK, D_MODEL), lambda i, j: (j, 0)),
            vec,
        ],
        out_specs=row_blk(D_MODEL),
        out_shape=jax.ShapeDtypeStruct((rows, D_MODEL), F32),
        scratch_shapes=[pltpu.VMEM((ROW_TILE, D_MODEL), BF16), pltpu.VMEM((ROW_TILE, D_MODEL), F32)],
        compiler_params=_cparams(("arbitrary", "arbitrary")),
        name="mix_mlp",
    )(h2d, a, b, c, wo, g, w1, w2, gf)


def _t5_bucket(dist):
    n = jnp.maximum(dist, 0)
    max_exact = REL_BUCKETS // 2
    nf = jnp.maximum(n, 1).astype(F32)
    large = max_exact + (jnp.log(nf / max_exact) / math.log(REL_MAX_DIST / max_exact)
                         * (REL_BUCKETS - max_exact)).astype(jnp.int32)
    large = jnp.minimum(large, REL_BUCKETS - 1)
    return jnp.where(n < max_exact, n, large)


def _bias_tiles(table):
    i = jnp.arange(ATT_BLK)[:, None]
    j = jnp.arange(ATT_BLK)[None, :]
    tiles = []
    for offset in (2 * ATT_BLK, ATT_BLK, 0):
        dist = i - j + offset
        t = jnp.moveaxis(table[_t5_bucket(dist)], -1, 0).astype(F32)
        tiles.append(jnp.where(dist >= 0, t, NEG))
    return jnp.stack(tiles)


def _retention_constants(tp):
    nh, dk = C_HEADS, C_HEAD_DIM
    log_gamma = jnp.log1p(-jnp.exp2(-5.0 - jnp.arange(nh, dtype=F32)))
    i = jnp.arange(RET_BLK, dtype=F32)
    diff = i[:, None] - i[None, :]
    dmat = jnp.where(diff >= 0, jnp.exp(log_gamma[:, None, None] * jnp.maximum(diff, 0.0)), 0.0)
    per_lane = lambda a: jnp.repeat(a, dk, axis=-1)
    qdec = per_lane(jnp.exp(log_gamma[None, :] * (i[:, None] + 1.0)))
    kdec = per_lane(jnp.exp(log_gamma[None, :] * (RET_BLK - 1.0 - i[:, None])))
    sdec = per_lane(jnp.exp(log_gamma * RET_BLK)[None, :])
    inv = ROPE_BASE ** (-jnp.arange(0, dk, 2, dtype=F32) / dk)
    ang = jnp.arange(tp, dtype=F32)[:, None] * inv[None, :]
    cos, sin = jnp.cos(ang), jnp.sin(ang)
    cos_full = jnp.tile(jnp.concatenate([cos, cos], axis=-1), (1, nh))
    sin_full = jnp.tile(jnp.concatenate([-sin, sin], axis=-1), (1, nh))
    return cos_full, sin_full, dmat, qdec, kdec, sdec


def _split_w_in(w):
    parts, acc = [], 0
    for size in IN_SIZES:
        parts.append(w[:, acc:acc + size])
        acc += size
    return parts


def kernel(x, meta, rel_bias, w_in, norm_mix, diff_lambda, diff_norm, ret_norm, w_out, norm_ff, w_ff1, w_ff2,
           final_norm):
    bsz, s_len, _ = x.shape
    depth = w_in.shape[0]
    t_len = s_len + N_META
    tp = -(-t_len // ATT_BLK) * ATT_BLK
    topk = min(TOPK_MAX, s_len // 4)
    nq = tp // ATT_BLK

    h = jnp.concatenate([
        jnp.broadcast_to(meta.astype(x.dtype)[None], (bsz, N_META, D_MODEL)), x,
        jnp.zeros((bsz, tp - t_len, D_MODEL), x.dtype)], axis=1).reshape(bsz * tp, D_MODEL)

    bias_a = _bias_tiles(rel_bias[:, :A_HEADS])
    bias_b = _bias_tiles(rel_bias[:, A_HEADS:])
    cos, sin, dmat, qdec, kdec, sdec = _retention_constants(tp)
    row = lambda v: v.reshape(1, -1).astype(F32)

    for l in range(depth):
        qa, ka, va, qi, ki, wi, qb, kb, vb, qc, kc, vc, gc = _split_w_in(w_in[l])
        w16 = jnp.concatenate([qa, ka, va, qi, qb, kb, vb, ki, ki], axis=1).astype(BF16)
        w32 = jnp.concatenate([qc, kc, vc, gc, wi, jnp.zeros((D_MODEL, LANES - IDX_HEADS), F32)],
                              axis=1).astype(BF16)
        p16, p32 = _inproj(h, row(norm_mix[l]), w16, w32)
        p16 = p16.reshape(bsz, tp, P16_COLS)
        p32 = p32.reshape(bsz, tp, P32_COLS)

        keys, tau, ntake, tie = _indexer(p16, p32, topk)
        tieflag = jnp.max(tie.reshape(bsz * nq, ATT_BLK), axis=1)
        a_out = _sparse_attn(p16, keys, tau, ntake, tieflag, bias_a)

        lambda_init = 0.8 - 0.6 * math.exp(-0.3 * l)
        lp = diff_lambda[l].astype(F32)
        lam = jnp.exp(jnp.sum(lp[0] * lp[1])) - jnp.exp(jnp.sum(lp[2] * lp[3])) + lambda_init
        g_pair = row(jnp.concatenate([diff_norm[l], diff_norm[l]]))
        b_out = _diff_attn(p16, lam.reshape(1), bias_b, g_pair, 1.0 - lambda_init)

        c_out = _retention(p32, cos, sin, dmat, qdec, kdec, sdec, row(ret_norm[l]))

        last = l == depth - 1
        h = _mix_mlp(h, a_out.reshape(bsz * tp, A_W), b_out.reshape(bsz * tp, B_W), c_out.reshape(bsz * tp, C_W),
                     w_out[l].astype(BF16), row(norm_ff[l]), w_ff1[l].astype(BF16), w_ff2[l].astype(BF16),
                     row(final_norm), final_norm=last)

    return h.reshape(bsz, tp, D_MODEL)[:, N_META:t_len]
```

```python
import functools
import math

import numpy as np
import jax
import jax.numpy as jnp
from jax import lax
from jax.experimental import pallas as pl
from jax.experimental.pallas import tpu as pltpu

D_MODEL = 1024
N_META = 16
A_HEADS = 8
A_HEAD_DIM = 64
A_W = A_HEADS * A_HEAD_DIM
IDX_HEADS = 8
IDX_DIM = 64
TOPK_MAX = 256
B_HEADS = 4
B_HEAD_DIM = 64
B_HALF = 32
B_W = B_HEADS * B_HEAD_DIM
C_HEADS = 4
C_HEAD_DIM = 64
C_W = C_HEADS * C_HEAD_DIM
ROPE_BASE = 10000.0
REL_BUCKETS = 32
REL_MAX_DIST = 128
D_FF = 4 * D_MODEL
EPS = 1e-6
IN_SIZES = (A_W, A_W, A_W, IDX_HEADS * IDX_DIM, IDX_DIM, IDX_HEADS, B_W, B_W, B_W, C_W, C_W, C_W, C_W)

LANES = 128
ATT_BLK = 512
IDX_ROWS = 128
IDX_CHUNK = 512
RET_BLK = 256
ROW_TILE = 512
FF_CHUNK = 1024
VMEM_LIMIT_BYTES = 56 * 1024 * 1024

NEG = -1e30
LOG2E = math.log2(math.e)

P16_COLS = 4 * A_W + 3 * B_W + 2 * IDX_DIM
P32_COLS = 4 * C_W + LANES

BF16 = jnp.bfloat16
F32 = jnp.float32


def _cparams(sem):
    return pltpu.CompilerParams(dimension_semantics=sem, vmem_limit_bytes=VMEM_LIMIT_BYTES)


def _dot_nt(a, b):
    return lax.dot_general(a, b, (((1,), (1,)), ((), ())), preferred_element_type=F32)


def _dot(a, b):
    return jnp.dot(a, b, preferred_element_type=F32)


def _inproj_kernel(h_ref, g_ref, w16_ref, w32_ref, o16_ref, o32_ref):
    x = h_ref[...]
    ms = jnp.mean(x * x, axis=-1, keepdims=True)
    u = ((x * lax.rsqrt(ms + EPS)) * g_ref[...]).astype(BF16)
    o16_ref[...] = _dot(u, w16_ref[...]).astype(BF16)
    o32_ref[...] = _dot(u, w32_ref[...])


def _inproj(h2d, g, w16, w32):
    rows = h2d.shape[0]
    return pl.pallas_call(
        _inproj_kernel,
        grid=(rows // ROW_TILE,),
        in_specs=[
            pl.BlockSpec((ROW_TILE, D_MODEL), lambda i: (i, 0)),
            pl.BlockSpec((1, D_MODEL), lambda i: (0, 0)),
            pl.BlockSpec((D_MODEL, P16_COLS), lambda i: (0, 0)),
            pl.BlockSpec((D_MODEL, P32_COLS), lambda i: (0, 0)),
        ],
        out_specs=[
            pl.BlockSpec((ROW_TILE, P16_COLS), lambda i: (i, 0)),
            pl.BlockSpec((ROW_TILE, P32_COLS), lambda i: (i, 0)),
        ],
        out_shape=[
            jax.ShapeDtypeStruct((rows, P16_COLS), BF16),
            jax.ShapeDtypeStruct((rows, P32_COLS), F32),
        ],
        compiler_params=_cparams(("arbitrary",)),
        name="inproj",
    )(h2d, g, w16, w32)


def _indexer_kernel(qi_ref, ki_ref, wi_ref, sc_ref, tau_ref, ntake_ref, tie_ref, qm_ref, *, topk, n_chunks):
    i = pl.program_id(1)
    row0 = i * IDX_ROWS
    n_live = (row0 + IDX_ROWS - 1) // IDX_CHUNK + 1
    n_tiles = IDX_CHUNK // LANES
    k_f = float(topk)

    lane = lax.broadcasted_iota(jnp.int32, (IDX_ROWS, LANES), 1)
    for h in range(IDX_HEADS):
        pair = qi_ref[:, (h // 2) * LANES:(h // 2 + 1) * LANES]
        keep = (lane >= IDX_DIM) if h % 2 else (lane < IDX_DIM)
        qm_ref[h] = jnp.where(keep, pair, jnp.zeros_like(pair))

    w = wi_ref[...] * (IDX_HEADS ** -0.5 * IDX_DIM ** -0.5)
    wcol = [w[:, h:h + 1] for h in range(IDX_HEADS)]
    row = row0 + lax.broadcasted_iota(jnp.int32, (IDX_ROWS, 1), 0)
    col_in_chunk = lax.broadcasted_iota(jnp.int32, (IDX_ROWS, IDX_CHUNK), 1)

    def score_chunk(c, carry):
        lo, hi = carry
        start = pl.multiple_of(c * IDX_CHUNK, IDX_CHUNK)
        kc = ki_ref[pl.ds(start, IDX_CHUNK), :]
        acc = jnp.zeros((IDX_ROWS, IDX_CHUNK), F32)
        for h in range(IDX_HEADS):
            acc = acc + wcol[h] * jnp.maximum(_dot_nt(qm_ref[h], kc), 0.0)
        causal = col_in_chunk + start <= row
        sc = jnp.where(causal, acc, -jnp.inf)
        sc_ref[:, pl.ds(start, IDX_CHUNK)] = sc
        for_min = jnp.where(causal, acc, jnp.inf)
        for t in range(n_tiles):
            lo = jnp.minimum(lo, for_min[:, t * LANES:(t + 1) * LANES])
            hi = jnp.maximum(hi, sc[:, t * LANES:(t + 1) * LANES])
        return lo, hi

    lo_l, hi_l = lax.fori_loop(0, n_live, score_chunk, (jnp.full((IDX_ROWS, LANES), jnp.inf, F32),
                                                        jnp.full((IDX_ROWS, LANES), -jnp.inf, F32)))
    lo0 = jnp.min(lo_l, axis=1, keepdims=True)
    hi0 = jnp.max(hi_l, axis=1, keepdims=True)

    def fill_chunk(c, carry):
        start = pl.multiple_of(c * IDX_CHUNK, IDX_CHUNK)
        sc_ref[:, pl.ds(start, IDX_CHUNK)] = jnp.full((IDX_ROWS, IDX_CHUNK), -jnp.inf, F32)
        return carry

    lax.fori_loop(n_live, n_chunks, fill_chunk, 0)

    def count(pred):
        def body(c, acc):
            start = pl.multiple_of(c * IDX_CHUNK, IDX_CHUNK)
            m = jnp.where(pred(sc_ref[:, pl.ds(start, IDX_CHUNK)]), 1.0, 0.0)
            for t in range(n_tiles):
                acc = acc + m[:, t * LANES:(t + 1) * LANES]
            return acc
        acc = lax.fori_loop(0, n_live, body, jnp.zeros((IDX_ROWS, LANES), F32))
        return jnp.sum(acc, axis=1, keepdims=True)

    n_valid = (row + 1).astype(F32)
    keeps_all = n_valid <= k_f
    tied_at_max = count(lambda s: s >= hi0) >= k_f
    tau0 = jnp.where(keeps_all, jnp.float32(jnp.finfo(jnp.float32).min), hi0)
    active0 = jnp.where(keeps_all | tied_at_max, 0.0, 1.0)

    def unfinished(state):
        return jnp.sum(state[3]) > 0.0

    def bisect(state):
        lo, hi, tau, active = state
        mid = 0.5 * lo + 0.5 * hi
        collapsed = (mid <= lo) | (mid >= hi)
        cnt = count(lambda s: s >= mid)
        hit = cnt == k_f
        is_active = active > 0.0
        tau = jnp.where(is_active, jnp.where(collapsed, lo, jnp.where(hit, mid, tau)), tau)
        go_on = is_active & jnp.logical_not(collapsed | hit)
        lo = jnp.where(go_on & (cnt > k_f), mid, lo)
        hi = jnp.where(go_on & (cnt < k_f), mid, hi)
        return lo, hi, tau, jnp.where(go_on, 1.0, 0.0)

    _, _, tau, _ = lax.while_loop(unfinished, bisect, (lo0, hi0, tau0, active0))
    n_gt = count(lambda s: s > tau)
    n_ge = count(lambda s: s >= tau)
    tau_ref[...] = tau
    ntake_ref[...] = k_f - n_gt
    tie_ref[...] = (n_ge > k_f).astype(jnp.int32)


def _indexer(p16, p32, topk):
    bsz, tp, _ = p16.shape
    n_chunks = tp // IDX_CHUNK
    kern = functools.partial(_indexer_kernel, topk=topk, n_chunks=n_chunks)
    col = lambda n: pl.BlockSpec((None, IDX_ROWS, 1), lambda b, i: (b, i, 0))
    return pl.pallas_call(
        kern,
        grid=(bsz, tp // IDX_ROWS),
        in_specs=[
            pl.BlockSpec((None, IDX_ROWS, A_W), lambda b, i: (b, i, 3)),
            pl.BlockSpec((None, tp, LANES), lambda b, i: (b, 0, (P16_COLS - LANES) // LANES)),
            pl.BlockSpec((None, IDX_ROWS, LANES), lambda b, i: (b, i, 4 * C_W // LANES)),
        ],
        out_specs=[
            pl.BlockSpec((None, IDX_ROWS, tp), lambda b, i: (b, i, 0)),
            col(0), col(1), col(2),
        ],
        out_shape=[
            jax.ShapeDtypeStruct((bsz, tp, tp), F32),
            jax.ShapeDtypeStruct((bsz, tp, 1), F32),
            jax.ShapeDtypeStruct((bsz, tp, 1), F32),
            jax.ShapeDtypeStruct((bsz, tp, 1), jnp.int32),
        ],
        scratch_shapes=[pltpu.VMEM((IDX_HEADS, IDX_ROWS, LANES), BF16)],
        compiler_params=_cparams(("arbitrary", "arbitrary")),
        name="indexer",
    )(p16, p16, p32)


def _lane_masked_queries(q_ref, qm_ref, n_slots, width):
    per_group = LANES // width
    lane = lax.broadcasted_iota(jnp.int32, (ATT_BLK, LANES), 1)
    for s in range(n_slots):
        g, r = divmod(s, per_group)
        grp = q_ref[:, g * LANES:(g + 1) * LANES]
        keep = (lane >= r * width) & (lane < (r + 1) * width)
        qm_ref[s] = jnp.where(keep, grp, jnp.zeros_like(grp))


def _flash_update(s, logit2, v_ones, m_ref, acc_ref):
    m_prev = m_ref[s]
    m_new = jnp.maximum(m_prev, jnp.max(logit2, axis=1, keepdims=True))
    alpha = jnp.exp2(m_prev - m_new)
    p = jnp.concatenate([jnp.exp2(logit2[:, t * LANES:(t + 1) * LANES] - m_new)
                         for t in range(logit2.shape[1] // LANES)], axis=1).astype(BF16)
    acc_ref[s] = jnp.concatenate([alpha, alpha], axis=1) * acc_ref[s] + _dot(p, v_ones)
    m_ref[s] = m_new


def _with_ones(v_grp):
    return jnp.concatenate([v_grp, jnp.ones(v_grp.shape, v_grp.dtype)], axis=1)


def _flash_init(m_ref, acc_ref):
    m_ref[...] = jnp.full(m_ref.shape, NEG, F32)
    acc_ref[...] = jnp.zeros(acc_ref.shape, F32)


def _flash_result(s, acc_ref):
    acc = acc_ref[s]
    return acc[:, :LANES] / acc[:, LANES:]


def _triangle_steps(nq):
    qs, ks, kinds = [], [], []
    for q in range(nq):
        for k in range(q + 1):
            qs.append(q)
            ks.append(k)
            kinds.append(2 if k == q else (1 if k == q - 1 else 0))
    return (jnp.asarray(qs, jnp.int32), jnp.asarray(ks, jnp.int32), jnp.asarray(kinds, jnp.int32))


def _sparse_attn_kernel(qmap, kmap, kind, tieflag, q_ref, k_ref, v_ref, sc_ref, tau_ref, ntake_ref, bias_ref,
                        o_ref, qm_ref, m_ref, acc_ref, mask_ref, carry_ref, *, nq):
    b = pl.program_id(0)
    s = pl.program_id(1)
    qb = qmap[s]
    kb = kmap[s]

    @pl.when(kb == 0)
    def _():
        _lane_masked_queries(q_ref, qm_ref, A_HEADS, A_HEAD_DIM)
        _flash_init(m_ref, acc_ref)
        carry_ref[...] = jnp.zeros(carry_ref.shape, F32)

    has_ties = tieflag[b * nq + qb]

    @pl.when(has_ties == 0)
    def _():
        mask_ref[...] = jnp.where(sc_ref[...] >= tau_ref[...], 0.0, NEG)

    @pl.when(has_ties != 0)
    def _():
        sc = sc_ref[...]
        tau = tau_ref[...]
        eq = sc == tau
        eq_b = jnp.where(eq, 1.0, 0.0).astype(BF16)
        r = lax.broadcasted_iota(jnp.int32, (ATT_BLK, ATT_BLK), 0)
        c = lax.broadcasted_iota(jnp.int32, (ATT_BLK, ATT_BLK), 1)
        before = jnp.where(r < c, 1.0, 0.0).astype(BF16)
        seen = carry_ref[...] + _dot(eq_b, before)
        keep = (sc > tau) | (eq & (seen < ntake_ref[...]))
        mask_ref[...] = jnp.where(keep, 0.0, NEG)
        carry_ref[...] = carry_ref[...] + jnp.sum(eq_b.astype(F32), axis=1, keepdims=True)

    scale2 = A_HEAD_DIM ** -0.5 * LOG2E
    for g in range(A_HEADS // 2):
        k_grp = k_ref[:, g * LANES:(g + 1) * LANES]
        v_ones = _with_ones(v_ref[:, g * LANES:(g + 1) * LANES])
        for h in (2 * g, 2 * g + 1):
            logit2 = _dot_nt(qm_ref[h], k_grp) * scale2 + bias_ref[h] + mask_ref[...]
            _flash_update(h, logit2, v_ones, m_ref, acc_ref)

    @pl.when(kb == qb)
    def _():
        lane = lax.broadcasted_iota(jnp.int32, (ATT_BLK, LANES), 1)
        for g in range(A_HEADS // 2):
            out = jnp.where(lane < A_HEAD_DIM, _flash_result(2 * g, acc_ref), _flash_result(2 * g + 1, acc_ref))
            o_ref[:, g * LANES:(g + 1) * LANES] = out.astype(o_ref.dtype)


def _sparse_attn(p16, keys, tau, ntake, tieflag, bias_tiles):
    bsz, tp, _ = p16.shape
    nq = tp // ATT_BLK
    qmap, kmap, kind = _triangle_steps(nq)
    qblk = lambda c: pl.BlockSpec((None, ATT_BLK, A_W), lambda b, s, qm, km, kd, tf: (b, qm[s], c))
    kblk = lambda c: pl.BlockSpec((None, ATT_BLK, A_W), lambda b, s, qm, km, kd, tf: (b, km[s], c))
    colblk = pl.BlockSpec((None, ATT_BLK, 1), lambda b, s, qm, km, kd, tf: (b, qm[s], 0))
    grid_spec = pltpu.PrefetchScalarGridSpec(
        num_scalar_prefetch=4,
        grid=(bsz, int(qmap.shape[0])),
        in_specs=[
            qblk(0), kblk(1), kblk(2),
            pl.BlockSpec((None, ATT_BLK, ATT_BLK), lambda b, s, qm, km, kd, tf: (b, qm[s], km[s])),
            colblk, colblk,
            pl.BlockSpec((None, A_HEADS, ATT_BLK, ATT_BLK), lambda b, s, qm, km, kd, tf: (kd[s], 0, 0, 0)),
        ],
        out_specs=pl.BlockSpec((None, ATT_BLK, A_W), lambda b, s, qm, km, kd, tf: (b, qm[s], 0)),
        scratch_shapes=[
            pltpu.VMEM((A_HEADS, ATT_BLK, LANES), BF16),
            pltpu.VMEM((A_HEADS, ATT_BLK, LANES), F32),
            pltpu.VMEM((A_HEADS, ATT_BLK, 2 * LANES), F32),
            pltpu.VMEM((ATT_BLK, ATT_BLK), F32),
            pltpu.VMEM((ATT_BLK, 1), F32),
        ],
    )
    return pl.pallas_call(
        functools.partial(_sparse_attn_kernel, nq=nq),
        grid_spec=grid_spec,
        out_shape=jax.ShapeDtypeStruct((bsz, tp, A_W), BF16),
        compiler_params=_cparams(("arbitrary", "arbitrary")),
        name="sparse_attn",
    )(qmap, kmap, kind, tieflag, p16, p16, p16, keys, tau, ntake, bias_tiles)


def _diff_attn_kernel(qmap, kmap, kind, lam_ref, q_ref, k_ref, v_ref, bias_ref, g_ref,
                      o_ref, qm_ref, m_ref, acc_ref, *, out_scale):
    s = pl.program_id(1)
    qb = qmap[s]
    kb = kmap[s]
    n_slots = 2 * B_HEADS

    @pl.when(kb == 0)
    def _():
        _lane_masked_queries(q_ref, qm_ref, n_slots, B_HALF)
        _flash_init(m_ref, acc_ref)

    scale2 = B_HALF ** -0.5 * LOG2E
    for g in range(B_HEADS // 2):
        k_grp = k_ref[:, g * LANES:(g + 1) * LANES]
        v_ones = _with_ones(v_ref[:, g * LANES:(g + 1) * LANES])
        for slot in range(4 * g, 4 * g + 4):
            logit2 = _dot_nt(qm_ref[slot], k_grp) * scale2 + bias_ref[slot // 2]
            _flash_update(slot, logit2, v_ones, m_ref, acc_ref)

    @pl.when(kb == qb)
    def _():
        lam = lam_ref[0]
        lane = lax.broadcasted_iota(jnp.int32, (ATT_BLK, LANES), 1)
        low = lane < B_HEAD_DIM
        for g in range(B_HEADS // 2):
            heads = []
            for h in (2 * g, 2 * g + 1):
                heads.append(_flash_result(2 * h, acc_ref) - lam * _flash_result(2 * h + 1, acc_ref))
            x = jnp.where(low, heads[0], heads[1])
            sq = x * x
            ss_lo = jnp.sum(jnp.where(low, sq, 0.0), axis=1, keepdims=True)
            ss_hi = jnp.sum(jnp.where(low, 0.0, sq), axis=1, keepdims=True)
            ms = jnp.where(low, ss_lo, ss_hi) * (1.0 / B_HEAD_DIM)
            y = (x * lax.rsqrt(ms + EPS)) * g_ref[...]
            o_ref[:, g * LANES:(g + 1) * LANES] = (y * out_scale).astype(o_ref.dtype)


def _diff_attn(p16, lam, bias_tiles, g_pair, out_scale):
    bsz, tp, _ = p16.shape
    nq = tp // ATT_BLK
    qmap, kmap, kind = _triangle_steps(nq)
    base = 4 * A_W // B_W
    qblk = pl.BlockSpec((None, ATT_BLK, B_W), lambda b, s, qm, km, kd, lm: (b, qm[s], base))
    kblk = lambda c: pl.BlockSpec((None, ATT_BLK, B_W), lambda b, s, qm, km, kd, lm: (b, km[s], base + c))
    n_slots = 2 * B_HEADS
    grid_spec = pltpu.PrefetchScalarGridSpec(
        num_scalar_prefetch=4,
        grid=(bsz, int(qmap.shape[0])),
        in_specs=[
            qblk, kblk(1), kblk(2),
            pl.BlockSpec((None, B_HEADS, ATT_BLK, ATT_BLK), lambda b, s, qm, km, kd, lm: (kd[s], 0, 0, 0)),
            pl.BlockSpec((1, LANES), lambda b, s, qm, km, kd, lm: (0, 0)),
        ],
        out_specs=pl.BlockSpec((None, ATT_BLK, B_W), lambda b, s, qm, km, kd, lm: (b, qm[s], 0)),
        scratch_shapes=[
            pltpu.VMEM((n_slots, ATT_BLK, LANES), BF16),
            pltpu.VMEM((n_slots, ATT_BLK, LANES), F32),
            pltpu.VMEM((n_slots, ATT_BLK, 2 * LANES), F32),
        ],
    )
    return pl.pallas_call(
        functools.partial(_diff_attn_kernel, out_scale=out_scale),
        grid_spec=grid_spec,
        out_shape=jax.ShapeDtypeStruct((bsz, tp, B_W), BF16),
        compiler_params=_cparams(("arbitrary", "arbitrary")),
        name="diff_attn",
    )(qmap, kmap, kind, lam, p16, p16, p16, bias_tiles, g_pair)


def _retention_kernel(q_ref, k_ref, v_ref, gate_ref, cos_ref, sin_ref, dmat_ref, qdec_ref, kdec_ref, sdec_ref,
                      g_ref, o_ref, state_ref):
    c = pl.program_id(1)

    @pl.when(c == 0)
    def _():
        state_ref[...] = jnp.zeros(state_ref.shape, F32)

    lane = lax.broadcasted_iota(jnp.int32, (RET_BLK, C_W), 1)
    first_half = (lane % C_HEAD_DIM) < (C_HEAD_DIM // 2)

    def rope(x):
        swapped = jnp.where(first_half,
                            pltpu.roll(x, C_W - C_HEAD_DIM // 2, axis=1),
                            pltpu.roll(x, C_HEAD_DIM // 2, axis=1))
        return x * cos_ref[...] + swapped * sin_ref[...]

    q = rope(q_ref[...])
    k = rope(k_ref[...]) * (C_HEAD_DIM ** -0.5)
    q16 = q.astype(BF16)
    k16 = k.astype(BF16)
    v16 = v_ref[...].astype(BF16)

    state = state_ref[...]
    out = _dot(q16, state.astype(BF16)) * qdec_ref[...]
    for h in range(C_HEADS):
        in_head = (lane >= h * C_HEAD_DIM) & (lane < (h + 1) * C_HEAD_DIM)
        qh = jnp.where(in_head, q, 0.0).astype(BF16)
        inner = _dot_nt(qh, k16) * dmat_ref[h]
        out = out + jnp.where(in_head, _dot(inner.astype(BF16), v16), 0.0)

    kd = (k * kdec_ref[...]).astype(BF16)
    update = _dot(kd.T, v16)
    r = lax.broadcasted_iota(jnp.int32, (C_W, C_W), 0) // C_HEAD_DIM
    cc = lax.broadcasted_iota(jnp.int32, (C_W, C_W), 1) // C_HEAD_DIM
    state_ref[...] = state * sdec_ref[...] + jnp.where(r == cc, update, 0.0)

    sq = out * out
    ms = jnp.zeros_like(out)
    for h in range(C_HEADS):
        in_head = (lane >= h * C_HEAD_DIM) & (lane < (h + 1) * C_HEAD_DIM)
        ss = jnp.sum(jnp.where(in_head, sq, 0.0), axis=1, keepdims=True)
        ms = jnp.where(in_head, ss * (1.0 / C_HEAD_DIM), ms)
    normed = (out * lax.rsqrt(ms + EPS)) * g_ref[...]
    gate = gate_ref[...]
    o_ref[...] = ((gate * jax.nn.sigmoid(gate)) * normed).astype(o_ref.dtype)


def _retention(p32, cos, sin, dmat, qdec, kdec, sdec, g):
    bsz, tp, _ = p32.shape
    blk = lambda c: pl.BlockSpec((None, RET_BLK, C_W), lambda b, i: (b, i, c))
    const = lambda shape: pl.BlockSpec(shape, lambda b, i: (0,) * len(shape))
    return pl.pallas_call(
        _retention_kernel,
        grid=(bsz, tp // RET_BLK),
        in_specs=[
            blk(0), blk(1), blk(2), blk(3),
            pl.BlockSpec((RET_BLK, C_W), lambda b, i: (i, 0)),
            pl.BlockSpec((RET_BLK, C_W), lambda b, i: (i, 0)),
            const((C_HEADS, RET_BLK, RET_BLK)),
            const((RET_BLK, C_W)), const((RET_BLK, C_W)), const((1, C_W)), const((1, C_W)),
        ],
        out_specs=pl.BlockSpec((None, RET_BLK, C_W), lambda b, i: (b, i, 0)),
        out_shape=jax.ShapeDtypeStruct((bsz, tp, C_W), BF16),
        scratch_shapes=[pltpu.VMEM((C_W, C_W), F32)],
        compiler_params=_cparams(("arbitrary", "arbitrary")),
        name="retention",
    )(p32, p32, p32, p32, cos, sin, dmat, qdec, kdec, sdec, g)


def _mix_mlp_kernel(h_ref, a_ref, b_ref, c_ref, wo_ref, g_ref, w1_ref, w2_ref, gf_ref, o_ref, u_ref, acc_ref,
                    *, final_norm):
    j = pl.program_id(1)

    @pl.when(j == 0)
    def _():
        mixed = (_dot(a_ref[...], wo_ref[0:A_W, :])
                 + _dot(b_ref[...], wo_ref[A_W:A_W + B_W, :])
                 + _dot(c_ref[...], wo_ref[A_W + B_W:, :]))
        h1 = h_ref[...] + mixed
        acc_ref[...] = h1
        ms = jnp.mean(h1 * h1, axis=-1, keepdims=True)
        u_ref[...] = ((h1 * lax.rsqrt(ms + EPS)) * g_ref[...]).astype(BF16)

    t = jnp.maximum(_dot(u_ref[...], w1_ref[...]), 0.0)
    acc_ref[...] += _dot((t * t).astype(BF16), w2_ref[...])

    @pl.when(j == pl.num_programs(1) - 1)
    def _():
        y = acc_ref[...]
        if final_norm:
            ms = jnp.mean(y * y, axis=-1, keepdims=True)
            y = (y * lax.rsqrt(ms + EPS)) * gf_ref[...]
        o_ref[...] = y


def _mix_mlp(h2d, a, b, c, wo, g, w1, w2, gf, final_norm):
    rows = h2d.shape[0]
    row_blk = lambda w: pl.BlockSpec((ROW_TILE, w), lambda i, j: (i, 0))
    vec = pl.BlockSpec((1, D_MODEL), lambda i, j: (0, 0))
    return pl.pallas_call(
        functools.partial(_mix_mlp_kernel, final_norm=final_norm),
        grid=(rows // ROW_TILE, D_FF // FF_CHUNK),
        in_specs=[
            row_blk(D_MODEL), row_blk(A_W), row_blk(B_W), row_blk(C_W),
            pl.BlockSpec((D_MODEL, D_MODEL), lambda i, j: (0, 0)),
            vec,
            pl.BlockSpec((D_MODEL, FF_CHUNK), lambda i, j: (0, j)),
            pl.BlockSpec((FF_CHUNK, D_MODEL), lambda i, j: (j, 0)),
            vec,
        ],
        out_specs=row_blk(D_MODEL),
        out_shape=jax.ShapeDtypeStruct((rows, D_MODEL), F32),
        scratch_shapes=[pltpu.VMEM((ROW_TILE, D_MODEL), BF16), pltpu.VMEM((ROW_TILE, D_MODEL), F32)],
        compiler_params=_cparams(("arbitrary", "arbitrary")),
        name="mix_mlp",
    )(h2d, a, b, c, wo, g, w1, w2, gf)


def _t5_bucket(dist):
    n = jnp.maximum(dist, 0)
    max_exact = REL_BUCKETS // 2
    nf = jnp.maximum(n, 1).astype(F32)
    large = max_exact + (jnp.log(nf / max_exact) / math.log(REL_MAX_DIST / max_exact)
                         * (REL_BUCKETS - max_exact)).astype(jnp.int32)
    large = jnp.minimum(large, REL_BUCKETS - 1)
    return jnp.where(n < max_exact, n, large)


def _bias_tiles(table):
    i = jnp.arange(ATT_BLK)[:, None]
    j = jnp.arange(ATT_BLK)[None, :]
    table2 = table.astype(F32).T * LOG2E
    tiles = []
    for offset in (2 * ATT_BLK, ATT_BLK, 0):
        dist = i - j + offset
        bucket = _t5_bucket(dist)
        t = jnp.zeros((table2.shape[0], ATT_BLK, ATT_BLK), F32)
        for b in range(REL_BUCKETS):
            t = jnp.where(bucket[None] == b, table2[:, b, None, None], t)
        tiles.append(jnp.where(dist >= 0, t, NEG))
    return jnp.stack(tiles)


def _retention_constants(tp):
    nh, dk = C_HEADS, C_HEAD_DIM
    log_gamma = jnp.log1p(-jnp.exp2(-5.0 - jnp.arange(nh, dtype=F32)))
    i = jnp.arange(RET_BLK, dtype=F32)
    diff = i[:, None] - i[None, :]
    dmat = jnp.where(diff >= 0, jnp.exp(log_gamma[:, None, None] * jnp.maximum(diff, 0.0)), 0.0)
    per_lane = lambda a: jnp.repeat(a, dk, axis=-1)
    qdec = per_lane(jnp.exp(log_gamma[None, :] * (i[:, None] + 1.0)))
    kdec = per_lane(jnp.exp(log_gamma[None, :] * (RET_BLK - 1.0 - i[:, None])))
    sdec = per_lane(jnp.exp(log_gamma * RET_BLK)[None, :])
    inv = ROPE_BASE ** (-jnp.arange(0, dk, 2, dtype=F32) / dk)
    ang = jnp.arange(tp, dtype=F32)[:, None] * inv[None, :]
    cos, sin = jnp.cos(ang), jnp.sin(ang)
    cos_full = jnp.tile(jnp.concatenate([cos, cos], axis=-1), (1, nh))
    sin_full = jnp.tile(jnp.concatenate([-sin, sin], axis=-1), (1, nh))
    return cos_full, sin_full, dmat, qdec, kdec, sdec


def _split_w_in(w):
    parts, acc = [], 0
    for size in IN_SIZES:
        parts.append(w[:, acc:acc + size])
        acc += size
    return parts


def kernel(x, meta, rel_bias, w_in, norm_mix, diff_lambda, diff_norm, ret_norm, w_out, norm_ff, w_ff1, w_ff2,
           final_norm):
    bsz, s_len, _ = x.shape
    depth = w_in.shape[0]
    t_len = s_len + N_META
    tp = -(-t_len // ATT_BLK) * ATT_BLK
    topk = min(TOPK_MAX, s_len // 4)
    nq = tp // ATT_BLK

    h = jnp.concatenate([
        jnp.broadcast_to(meta.astype(x.dtype)[None], (bsz, N_META, D_MODEL)), x,
        jnp.zeros((bsz, tp - t_len, D_MODEL), x.dtype)], axis=1).reshape(bsz * tp, D_MODEL)

    bias_a = _bias_tiles(rel_bias[:, :A_HEADS])
    bias_b = _bias_tiles(rel_bias[:, A_HEADS:])
    cos, sin, dmat, qdec, kdec, sdec = _retention_constants(tp)
    row = lambda v: v.reshape(1, -1).astype(F32)

    for l in range(depth):
        qa, ka, va, qi, ki, wi, qb, kb, vb, qc, kc, vc, gc = _split_w_in(w_in[l])
        w16 = jnp.concatenate([qa, ka, va, qi, qb, kb, vb, ki, ki], axis=1).astype(BF16)
        w32 = jnp.concatenate([qc, kc, vc, gc, wi, jnp.zeros((D_MODEL, LANES - IDX_HEADS), F32)],
                              axis=1).astype(BF16)
        p16, p32 = _inproj(h, row(norm_mix[l]), w16, w32)
        p16 = p16.reshape(bsz, tp, P16_COLS)
        p32 = p32.reshape(bsz, tp, P32_COLS)

        keys, tau, ntake, tie = _indexer(p16, p32, topk)
        tieflag = jnp.max(tie.reshape(bsz * nq, ATT_BLK), axis=1)
        a_out = _sparse_attn(p16, keys, tau, ntake, tieflag, bias_a)

        lambda_init = 0.8 - 0.6 * math.exp(-0.3 * l)
        lp = diff_lambda[l].astype(F32)
        lam = jnp.exp(jnp.sum(lp[0] * lp[1])) - jnp.exp(jnp.sum(lp[2] * lp[3])) + lambda_init
        g_pair = row(jnp.concatenate([diff_norm[l], diff_norm[l]]))
        b_out = _diff_attn(p16, lam.reshape(1), bias_b, g_pair, 1.0 - lambda_init)

        c_out = _retention(p32, cos, sin, dmat, qdec, kdec, sdec, row(ret_norm[l]))

        last = l == depth - 1
        h = _mix_mlp(h, a_out.reshape(bsz * tp, A_W), b_out.reshape(bsz * tp, B_W), c_out.reshape(bsz * tp, C_W),
                     w_out[l].astype(BF16), row(norm_ff[l]), w_ff1[l].astype(BF16), w_ff2[l].astype(BF16),
                     row(final_norm), final_norm=last)

    return h.reshape(bsz, tp, D_MODEL)[:, N_META:t_len]
```

```python
import functools
import math

import numpy as np
import jax
import jax.numpy as jnp
from jax import lax
from jax.experimental import pallas as pl
from jax.experimental.pallas import tpu as pltpu

D_MODEL = 1024
N_META = 16
A_HEADS = 8
A_HEAD_DIM = 64
A_W = A_HEADS * A_HEAD_DIM
IDX_HEADS = 8
IDX_DIM = 64
TOPK_MAX = 256
B_HEADS = 4
B_HEAD_DIM = 64
B_HALF = 32
B_W = B_HEADS * B_HEAD_DIM
C_HEADS = 4
C_HEAD_DIM = 64
C_W = C_HEADS * C_HEAD_DIM
ROPE_BASE = 10000.0
REL_BUCKETS = 32
REL_MAX_DIST = 128
D_FF = 4 * D_MODEL
EPS = 1e-6
IN_SIZES = (A_W, A_W, A_W, IDX_HEADS * IDX_DIM, IDX_DIM, IDX_HEADS, B_W, B_W, B_W, C_W, C_W, C_W, C_W)

LANES = 128
ATT_BLK = 512
IDX_ROWS = 128
IDX_CHUNK = 512
BISECT_FIXED_STEPS = 20
RET_BLK = 256
ROW_TILE = 512
FF_CHUNK = 1024
VMEM_LIMIT_BYTES = 56 * 1024 * 1024

NEG = -1e30
LOG2E = math.log2(math.e)

P16_COLS = 4 * A_W + 3 * B_W + 2 * IDX_DIM
P32_COLS = 4 * C_W + LANES

BF16 = jnp.bfloat16
F32 = jnp.float32


def _cparams(sem):
    return pltpu.CompilerParams(dimension_semantics=sem, vmem_limit_bytes=VMEM_LIMIT_BYTES)


def _dot_nt(a, b):
    return lax.dot_general(a, b, (((1,), (1,)), ((), ())), preferred_element_type=F32)


def _dot(a, b):
    return jnp.dot(a, b, preferred_element_type=F32)


def _inproj_kernel(h_ref, g_ref, w16_ref, w32_ref, o16_ref, o32_ref):
    x = h_ref[...]
    ms = jnp.mean(x * x, axis=-1, keepdims=True)
    u = ((x * lax.rsqrt(ms + EPS)) * g_ref[...]).astype(BF16)
    o16_ref[...] = _dot(u, w16_ref[...]).astype(BF16)
    o32_ref[...] = _dot(u, w32_ref[...])


def _inproj(h2d, g, w16, w32):
    rows = h2d.shape[0]
    return pl.pallas_call(
        _inproj_kernel,
        grid=(rows // ROW_TILE,),
        in_specs=[
            pl.BlockSpec((ROW_TILE, D_MODEL), lambda i: (i, 0)),
            pl.BlockSpec((1, D_MODEL), lambda i: (0, 0)),
            pl.BlockSpec((D_MODEL, P16_COLS), lambda i: (0, 0)),
            pl.BlockSpec((D_MODEL, P32_COLS), lambda i: (0, 0)),
        ],
        out_specs=[
            pl.BlockSpec((ROW_TILE, P16_COLS), lambda i: (i, 0)),
            pl.BlockSpec((ROW_TILE, P32_COLS), lambda i: (i, 0)),
        ],
        out_shape=[
            jax.ShapeDtypeStruct((rows, P16_COLS), BF16),
            jax.ShapeDtypeStruct((rows, P32_COLS), F32),
        ],
        compiler_params=_cparams(("arbitrary",)),
        name="inproj",
    )(h2d, g, w16, w32)


def _indexer_kernel(qi_ref, ki_ref, wi_ref, sc_ref, tau_ref, ntake_ref, tie_ref, qm_ref, *, topk, n_chunks):
    i = pl.program_id(1)
    row0 = i * IDX_ROWS
    n_live = (row0 + IDX_ROWS - 1) // IDX_CHUNK + 1
    n_tiles = IDX_CHUNK // LANES
    k_f = float(topk)

    lane = lax.broadcasted_iota(jnp.int32, (IDX_ROWS, LANES), 1)
    for h in range(IDX_HEADS):
        pair = qi_ref[:, (h // 2) * LANES:(h // 2 + 1) * LANES]
        keep = (lane >= IDX_DIM) if h % 2 else (lane < IDX_DIM)
        qm_ref[h] = jnp.where(keep, pair, jnp.zeros_like(pair))

    w = wi_ref[...] * (IDX_HEADS ** -0.5 * IDX_DIM ** -0.5)
    wcol = [w[:, h:h + 1] for h in range(IDX_HEADS)]
    row = row0 + lax.broadcasted_iota(jnp.int32, (IDX_ROWS, 1), 0)
    col_in_chunk = lax.broadcasted_iota(jnp.int32, (IDX_ROWS, IDX_CHUNK), 1)

    def score_chunk(c, carry):
        lo, hi = carry
        start = pl.multiple_of(c * IDX_CHUNK, IDX_CHUNK)
        kc = ki_ref[pl.ds(start, IDX_CHUNK), :]
        acc = jnp.zeros((IDX_ROWS, IDX_CHUNK), F32)
        for h in range(IDX_HEADS):
            acc = acc + wcol[h] * jnp.maximum(_dot_nt(qm_ref[h], kc), 0.0)
        causal = col_in_chunk + start <= row
        sc = jnp.where(causal, acc, -jnp.inf)
        sc_ref[:, pl.ds(start, IDX_CHUNK)] = sc
        for_min = jnp.where(causal, acc, jnp.inf)
        for t in range(n_tiles):
            lo = jnp.minimum(lo, for_min[:, t * LANES:(t + 1) * LANES])
            hi = jnp.maximum(hi, sc[:, t * LANES:(t + 1) * LANES])
        return lo, hi

    lo_l, hi_l = lax.fori_loop(0, n_live, score_chunk, (jnp.full((IDX_ROWS, LANES), jnp.inf, F32),
                                                        jnp.full((IDX_ROWS, LANES), -jnp.inf, F32)))
    lo0 = jnp.min(lo_l, axis=1, keepdims=True)
    hi0 = jnp.max(hi_l, axis=1, keepdims=True)

    def fill_chunk(c, carry):
        start = pl.multiple_of(c * IDX_CHUNK, IDX_CHUNK)
        sc_ref[:, pl.ds(start, IDX_CHUNK)] = jnp.full((IDX_ROWS, IDX_CHUNK), -jnp.inf, F32)
        return carry

    lax.fori_loop(n_live, n_chunks, fill_chunk, 0)

    def count(pred):
        def body(c, acc):
            start = pl.multiple_of(c * IDX_CHUNK, IDX_CHUNK)
            m = jnp.where(pred(sc_ref[:, pl.ds(start, IDX_CHUNK)]), 1.0, 0.0)
            for t in range(n_tiles):
                acc = acc + m[:, t * LANES:(t + 1) * LANES]
            return acc
        acc = lax.fori_loop(0, n_live, body, jnp.zeros((IDX_ROWS, LANES), F32))
        return jnp.sum(acc, axis=1, keepdims=True)

    n_valid = (row + 1).astype(F32)
    keeps_all = n_valid <= k_f
    tied_at_max = count(lambda s: s >= hi0) >= k_f
    tau0 = jnp.where(keeps_all, jnp.float32(jnp.finfo(jnp.float32).min), hi0)
    active0 = jnp.where(keeps_all | tied_at_max, 0.0, 1.0)

    def unfinished(state):
        return jnp.sum(state[3]) > 0.0

    def bisect(state):
        lo, hi, tau, active = state
        mid = 0.5 * lo + 0.5 * hi
        collapsed = (mid <= lo) | (mid >= hi)
        cnt = count(lambda s: s >= mid)
        hit = cnt == k_f
        is_active = active > 0.0
        tau = jnp.where(is_active, jnp.where(collapsed, lo, jnp.where(hit, mid, tau)), tau)
        go_on = is_active & jnp.logical_not(collapsed | hit)
        lo = jnp.where(go_on & (cnt > k_f), mid, lo)
        hi = jnp.where(go_on & (cnt < k_f), mid, hi)
        return lo, hi, tau, jnp.where(go_on, 1.0, 0.0)

    state = lax.fori_loop(0, BISECT_FIXED_STEPS, lambda _, st: bisect(st), (lo0, hi0, tau0, active0))
    _, _, tau, _ = lax.while_loop(unfinished, bisect, state)
    n_gt = count(lambda s: s > tau)
    n_ge = count(lambda s: s >= tau)
    tau_ref[...] = tau
    ntake_ref[...] = k_f - n_gt
    tie_ref[...] = (n_ge > k_f).astype(jnp.int32)


def _indexer(p16, p32, topk):
    bsz, tp, _ = p16.shape
    n_chunks = tp // IDX_CHUNK
    kern = functools.partial(_indexer_kernel, topk=topk, n_chunks=n_chunks)
    col = lambda n: pl.BlockSpec((None, IDX_ROWS, 1), lambda b, i: (b, i, 0))
    return pl.pallas_call(
        kern,
        grid=(bsz, tp // IDX_ROWS),
        in_specs=[
            pl.BlockSpec((None, IDX_ROWS, A_W), lambda b, i: (b, i, 3)),
            pl.BlockSpec((None, tp, LANES), lambda b, i: (b, 0, (P16_COLS - LANES) // LANES)),
            pl.BlockSpec((None, IDX_ROWS, LANES), lambda b, i: (b, i, 4 * C_W // LANES)),
        ],
        out_specs=[
            pl.BlockSpec((None, IDX_ROWS, tp), lambda b, i: (b, i, 0)),
            col(0), col(1), col(2),
        ],
        out_shape=[
            jax.ShapeDtypeStruct((bsz, tp, tp), F32),
            jax.ShapeDtypeStruct((bsz, tp, 1), F32),
            jax.ShapeDtypeStruct((bsz, tp, 1), F32),
            jax.ShapeDtypeStruct((bsz, tp, 1), jnp.int32),
        ],
        scratch_shapes=[pltpu.VMEM((IDX_HEADS, IDX_ROWS, LANES), BF16)],
        compiler_params=_cparams(("arbitrary", "arbitrary")),
        name="indexer",
    )(p16, p16, p32)


def _lane_masked_queries(q_ref, qm_ref, n_slots, width):
    per_group = LANES // width
    lane = lax.broadcasted_iota(jnp.int32, (ATT_BLK, LANES), 1)
    for s in range(n_slots):
        g, r = divmod(s, per_group)
        grp = q_ref[:, g * LANES:(g + 1) * LANES]
        keep = (lane >= r * width) & (lane < (r + 1) * width)
        qm_ref[s] = jnp.where(keep, grp, jnp.zeros_like(grp))


def _flash_update(s, logit2, v_ones, m_ref, acc_ref):
    m_prev = m_ref[s]
    m_new = jnp.maximum(m_prev, jnp.max(logit2, axis=1, keepdims=True))
    alpha = jnp.exp2(m_prev - m_new)
    p = jnp.concatenate([jnp.exp2(logit2[:, t * LANES:(t + 1) * LANES] - m_new)
                         for t in range(logit2.shape[1] // LANES)], axis=1).astype(BF16)
    acc_ref[s] = jnp.concatenate([alpha, alpha], axis=1) * acc_ref[s] + _dot(p, v_ones)
    m_ref[s] = m_new


def _with_ones(v_grp):
    return jnp.concatenate([v_grp, jnp.ones(v_grp.shape, v_grp.dtype)], axis=1)


def _flash_init(m_ref, acc_ref):
    m_ref[...] = jnp.full(m_ref.shape, NEG, F32)
    acc_ref[...] = jnp.zeros(acc_ref.shape, F32)


def _flash_result(s, acc_ref):
    acc = acc_ref[s]
    return acc[:, :LANES] / acc[:, LANES:]


def _triangle_steps(nq):
    qs, ks, kinds = [], [], []
    for q in range(nq):
        for k in range(q + 1):
            qs.append(q)
            ks.append(k)
            kinds.append(2 if k == q else (1 if k == q - 1 else 0))
    return (jnp.asarray(qs, jnp.int32), jnp.asarray(ks, jnp.int32), jnp.asarray(kinds, jnp.int32))


def _sparse_attn_kernel(qmap, kmap, kind, tieflag, q_ref, k_ref, v_ref, sc_ref, tau_ref, ntake_ref, bias_ref,
                        o_ref, qm_ref, m_ref, acc_ref, mask_ref, carry_ref, *, nq):
    b = pl.program_id(0)
    s = pl.program_id(1)
    qb = qmap[s]
    kb = kmap[s]

    @pl.when(kb == 0)
    def _():
        _lane_masked_queries(q_ref, qm_ref, A_HEADS, A_HEAD_DIM)
        _flash_init(m_ref, acc_ref)
        carry_ref[...] = jnp.zeros(carry_ref.shape, F32)

    has_ties = tieflag[b * nq + qb]

    @pl.when(has_ties == 0)
    def _():
        mask_ref[...] = jnp.where(sc_ref[...] >= tau_ref[...], 0.0, NEG)

    @pl.when(has_ties != 0)
    def _():
        sc = sc_ref[...]
        tau = tau_ref[...]
        eq = sc == tau
        eq_b = jnp.where(eq, 1.0, 0.0).astype(BF16)
        r = lax.broadcasted_iota(jnp.int32, (ATT_BLK, ATT_BLK), 0)
        c = lax.broadcasted_iota(jnp.int32, (ATT_BLK, ATT_BLK), 1)
        before = jnp.where(r < c, 1.0, 0.0).astype(BF16)
        seen = carry_ref[...] + _dot(eq_b, before)
        keep = (sc > tau) | (eq & (seen < ntake_ref[...]))
        mask_ref[...] = jnp.where(keep, 0.0, NEG)
        carry_ref[...] = carry_ref[...] + jnp.sum(eq_b.astype(F32), axis=1, keepdims=True)

    scale2 = A_HEAD_DIM ** -0.5 * LOG2E

    def attend(near_diagonal):
        for g in range(A_HEADS // 2):
            k_grp = k_ref[:, g * LANES:(g + 1) * LANES]
            v_ones = _with_ones(v_ref[:, g * LANES:(g + 1) * LANES])
            for h in (2 * g, 2 * g + 1):
                logit2 = _dot_nt(qm_ref[h], k_grp) * scale2 + mask_ref[...]
                if near_diagonal:
                    logit2 = logit2 + bias_ref[h]
                _flash_update(h, logit2, v_ones, m_ref, acc_ref)

    pl.when(kind[s] == 0)(lambda: attend(False))
    pl.when(kind[s] != 0)(lambda: attend(True))

    @pl.when(kb == qb)
    def _():
        lane = lax.broadcasted_iota(jnp.int32, (ATT_BLK, LANES), 1)
        for g in range(A_HEADS // 2):
            out = jnp.where(lane < A_HEAD_DIM, _flash_result(2 * g, acc_ref), _flash_result(2 * g + 1, acc_ref))
            o_ref[:, g * LANES:(g + 1) * LANES] = out.astype(o_ref.dtype)


def _sparse_attn(p16, keys, tau, ntake, tieflag, bias_tiles):
    bsz, tp, _ = p16.shape
    nq = tp // ATT_BLK
    qmap, kmap, kind = _triangle_steps(nq)
    qblk = lambda c: pl.BlockSpec((None, ATT_BLK, A_W), lambda b, s, qm, km, kd, tf: (b, qm[s], c))
    kblk = lambda c: pl.BlockSpec((None, ATT_BLK, A_W), lambda b, s, qm, km, kd, tf: (b, km[s], c))
    colblk = pl.BlockSpec((None, ATT_BLK, 1), lambda b, s, qm, km, kd, tf: (b, qm[s], 0))
    grid_spec = pltpu.PrefetchScalarGridSpec(
        num_scalar_prefetch=4,
        grid=(bsz, int(qmap.shape[0])),
        in_specs=[
            qblk(0), kblk(1), kblk(2),
            pl.BlockSpec((None, ATT_BLK, ATT_BLK), lambda b, s, qm, km, kd, tf: (b, qm[s], km[s])),
            colblk, colblk,
            pl.BlockSpec((None, A_HEADS, ATT_BLK, ATT_BLK),
                         lambda b, s, qm, km, kd, tf: (jnp.maximum(kd[s] - 1, 0), 0, 0, 0)),
        ],
        out_specs=pl.BlockSpec((None, ATT_BLK, A_W), lambda b, s, qm, km, kd, tf: (b, qm[s], 0)),
        scratch_shapes=[
            pltpu.VMEM((A_HEADS, ATT_BLK, LANES), BF16),
            pltpu.VMEM((A_HEADS, ATT_BLK, LANES), F32),
            pltpu.VMEM((A_HEADS, ATT_BLK, 2 * LANES), F32),
            pltpu.VMEM((ATT_BLK, ATT_BLK), F32),
            pltpu.VMEM((ATT_BLK, 1), F32),
        ],
    )
    return pl.pallas_call(
        functools.partial(_sparse_attn_kernel, nq=nq),
        grid_spec=grid_spec,
        out_shape=jax.ShapeDtypeStruct((bsz, tp, A_W), BF16),
        compiler_params=_cparams(("arbitrary", "arbitrary")),
        name="sparse_attn",
    )(qmap, kmap, kind, tieflag, p16, p16, p16, keys, tau, ntake, bias_tiles)


def _diff_attn_kernel(qmap, kmap, kind, lam_ref, q_ref, k_ref, v_ref, bias_ref, g_ref,
                      o_ref, qm_ref, m_ref, acc_ref, *, out_scale):
    s = pl.program_id(1)
    qb = qmap[s]
    kb = kmap[s]
    n_slots = 2 * B_HEADS

    @pl.when(kb == 0)
    def _():
        _lane_masked_queries(q_ref, qm_ref, n_slots, B_HALF)
        _flash_init(m_ref, acc_ref)

    scale2 = B_HALF ** -0.5 * LOG2E

    def attend(near_diagonal):
        for g in range(B_HEADS // 2):
            k_grp = k_ref[:, g * LANES:(g + 1) * LANES]
            v_ones = _with_ones(v_ref[:, g * LANES:(g + 1) * LANES])
            for slot in range(4 * g, 4 * g + 4):
                logit2 = _dot_nt(qm_ref[slot], k_grp) * scale2
                if near_diagonal:
                    logit2 = logit2 + bias_ref[slot // 2]
                _flash_update(slot, logit2, v_ones, m_ref, acc_ref)

    pl.when(kind[s] == 0)(lambda: attend(False))
    pl.when(kind[s] != 0)(lambda: attend(True))

    @pl.when(kb == qb)
    def _():
        lam = lam_ref[0]
        lane = lax.broadcasted_iota(jnp.int32, (ATT_BLK, LANES), 1)
        low = lane < B_HEAD_DIM
        for g in range(B_HEADS // 2):
            heads = []
            for h in (2 * g, 2 * g + 1):
                heads.append(_flash_result(2 * h, acc_ref) - lam * _flash_result(2 * h + 1, acc_ref))
            x = jnp.where(low, heads[0], heads[1])
            sq = x * x
            ss_lo = jnp.sum(jnp.where(low, sq, 0.0), axis=1, keepdims=True)
            ss_hi = jnp.sum(jnp.where(low, 0.0, sq), axis=1, keepdims=True)
            ms = jnp.where(low, ss_lo, ss_hi) * (1.0 / B_HEAD_DIM)
            y = (x * lax.rsqrt(ms + EPS)) * g_ref[...]
            o_ref[:, g * LANES:(g + 1) * LANES] = (y * out_scale).astype(o_ref.dtype)


def _diff_attn(p16, lam, bias_tiles, g_pair, out_scale):
    bsz, tp, _ = p16.shape
    nq = tp // ATT_BLK
    qmap, kmap, kind = _triangle_steps(nq)
    base = 4 * A_W // B_W
    qblk = pl.BlockSpec((None, ATT_BLK, B_W), lambda b, s, qm, km, kd, lm: (b, qm[s], base))
    kblk = lambda c: pl.BlockSpec((None, ATT_BLK, B_W), lambda b, s, qm, km, kd, lm: (b, km[s], base + c))
    n_slots = 2 * B_HEADS
    grid_spec = pltpu.PrefetchScalarGridSpec(
        num_scalar_prefetch=4,
        grid=(bsz, int(qmap.shape[0])),
        in_specs=[
            qblk, kblk(1), kblk(2),
            pl.BlockSpec((None, B_HEADS, ATT_BLK, ATT_BLK),
                         lambda b, s, qm, km, kd, lm: (jnp.maximum(kd[s] - 1, 0), 0, 0, 0)),
            pl.BlockSpec((1, LANES), lambda b, s, qm, km, kd, lm: (0, 0)),
        ],
        out_specs=pl.BlockSpec((None, ATT_BLK, B_W), lambda b, s, qm, km, kd, lm: (b, qm[s], 0)),
        scratch_shapes=[
            pltpu.VMEM((n_slots, ATT_BLK, LANES), BF16),
            pltpu.VMEM((n_slots, ATT_BLK, LANES), F32),
            pltpu.VMEM((n_slots, ATT_BLK, 2 * LANES), F32),
        ],
    )
    return pl.pallas_call(
        functools.partial(_diff_attn_kernel, out_scale=out_scale),
        grid_spec=grid_spec,
        out_shape=jax.ShapeDtypeStruct((bsz, tp, B_W), BF16),
        compiler_params=_cparams(("arbitrary", "arbitrary")),
        name="diff_attn",
    )(qmap, kmap, kind, lam, p16, p16, p16, bias_tiles, g_pair)


def _retention_kernel(q_ref, k_ref, v_ref, gate_ref, cos_ref, sin_ref, dmat_ref, qdec_ref, kdec_ref, sdec_ref,
                      g_ref, o_ref, state_ref):
    c = pl.program_id(1)

    @pl.when(c == 0)
    def _():
        state_ref[...] = jnp.zeros(state_ref.shape, F32)

    lane = lax.broadcasted_iota(jnp.int32, (RET_BLK, C_W), 1)
    first_half = (lane % C_HEAD_DIM) < (C_HEAD_DIM // 2)

    def rope(x):
        swapped = jnp.where(first_half,
                            pltpu.roll(x, C_W - C_HEAD_DIM // 2, axis=1),
                            pltpu.roll(x, C_HEAD_DIM // 2, axis=1))
        return x * cos_ref[...] + swapped * sin_ref[...]

    q = rope(q_ref[...])
    k = rope(k_ref[...]) * (C_HEAD_DIM ** -0.5)
    q16 = q.astype(BF16)
    k16 = k.astype(BF16)
    v16 = v_ref[...].astype(BF16)

    state = state_ref[...]
    out = _dot(q16, state.astype(BF16)) * qdec_ref[...]
    for h in range(C_HEADS):
        in_head = (lane >= h * C_HEAD_DIM) & (lane < (h + 1) * C_HEAD_DIM)
        qh = jnp.where(in_head, q, 0.0).astype(BF16)
        inner = _dot_nt(qh, k16) * dmat_ref[h]
        out = out + jnp.where(in_head, _dot(inner.astype(BF16), v16), 0.0)

    kd = (k * kdec_ref[...]).astype(BF16)
    update = _dot(kd.T, v16)
    r = lax.broadcasted_iota(jnp.int32, (C_W, C_W), 0) // C_HEAD_DIM
    cc = lax.broadcasted_iota(jnp.int32, (C_W, C_W), 1) // C_HEAD_DIM
    state_ref[...] = state * sdec_ref[...] + jnp.where(r == cc, update, 0.0)

    sq = out * out
    ms = jnp.zeros_like(out)
    for h in range(C_HEADS):
        in_head = (lane >= h * C_HEAD_DIM) & (lane < (h + 1) * C_HEAD_DIM)
        ss = jnp.sum(jnp.where(in_head, sq, 0.0), axis=1, keepdims=True)
        ms = jnp.where(in_head, ss * (1.0 / C_HEAD_DIM), ms)
    normed = (out * lax.rsqrt(ms + EPS)) * g_ref[...]
    gate = gate_ref[...]
    o_ref[...] = ((gate * jax.nn.sigmoid(gate)) * normed).astype(o_ref.dtype)


def _retention(p32, cos, sin, dmat, qdec, kdec, sdec, g):
    bsz, tp, _ = p32.shape
    blk = lambda c: pl.BlockSpec((None, RET_BLK, C_W), lambda b, i: (b, i, c))
    const = lambda shape: pl.BlockSpec(shape, lambda b, i: (0,) * len(shape))
    return pl.pallas_call(
        _retention_kernel,
        grid=(bsz, tp // RET_BLK),
        in_specs=[
            blk(0), blk(1), blk(2), blk(3),
            pl.BlockSpec((RET_BLK, C_W), lambda b, i: (i, 0)),
            pl.BlockSpec((RET_BLK, C_W), lambda b, i: (i, 0)),
            const((C_HEADS, RET_BLK, RET_BLK)),
            const((RET_BLK, C_W)), const((RET_BLK, C_W)), const((1, C_W)), const((1, C_W)),
        ],
        out_specs=pl.BlockSpec((None, RET_BLK, C_W), lambda b, i: (b, i, 0)),
        out_shape=jax.ShapeDtypeStruct((bsz, tp, C_W), BF16),
        scratch_shapes=[pltpu.VMEM((C_W, C_W), F32)],
        compiler_params=_cparams(("arbitrary", "arbitrary")),
        name="retention",
    )(p32, p32, p32, p32, cos, sin, dmat, qdec, kdec, sdec, g)


def _mix_mlp_kernel(h_ref, a_ref, b_ref, c_ref, wo_ref, g_ref, w1_ref, w2_ref, gf_ref, o_ref, u_ref, acc_ref,
                    *, final_norm):
    j = pl.program_id(1)

    @pl.when(j == 0)
    def _():
        mixed = (_dot(a_ref[...], wo_ref[0:A_W, :])
                 + _dot(b_ref[...], wo_ref[A_W:A_W + B_W, :])
                 + _dot(c_ref[...], wo_ref[A_W + B_W:, :]))
        h1 = h_ref[...] + mixed
        acc_ref[...] = h1
        ms = jnp.mean(h1 * h1, axis=-1, keepdims=True)
        u_ref[...] = ((h1 * lax.rsqrt(ms + EPS)) * g_ref[...]).astype(BF16)

    t = jnp.maximum(_dot(u_ref[...], w1_ref[...]), 0.0)
    acc_ref[...] += _dot((t * t).astype(BF16), w2_ref[...])

    @pl.when(j == pl.num_programs(1) - 1)
    def _():
        y = acc_ref[...]
        if final_norm:
            ms = jnp.mean(y * y, axis=-1, keepdims=True)
            y = (y * lax.rsqrt(ms + EPS)) * gf_ref[...]
        o_ref[...] = y


def _mix_mlp(h2d, a, b, c, wo, g, w1, w2, gf, final_norm):
    rows = h2d.shape[0]
    row_blk = lambda w: pl.BlockSpec((ROW_TILE, w), lambda i, j: (i, 0))
    vec = pl.BlockSpec((1, D_MODEL), lambda i, j: (0, 0))
    return pl.pallas_call(
        functools.partial(_mix_mlp_kernel, final_norm=final_norm),
        grid=(rows // ROW_TILE, D_FF // FF_CHUNK),
        in_specs=[
            row_blk(D_MODEL), row_blk(A_W), row_blk(B_W), row_blk(C_W),
            pl.BlockSpec((D_MODEL, D_MODEL), lambda i, j: (0, 0)),
            vec,
            pl.BlockSpec((D_MODEL, FF_CHUNK), lambda i, j: (0, j)),
            pl.BlockSpec((FF_CHUNK, D_MODEL), lambda i, j: (j, 0)),
            vec,
        ],
        out_specs=row_blk(D_MODEL),
        out_shape=jax.ShapeDtypeStruct((rows, D_MODEL), F32),
        scratch_shapes=[pltpu.VMEM((ROW_TILE, D_MODEL), BF16), pltpu.VMEM((ROW_TILE, D_MODEL), F32)],
        compiler_params=_cparams(("arbitrary", "arbitrary")),
        name="mix_mlp",
    )(h2d, a, b, c, wo, g, w1, w2, gf)


def _t5_bucket(dist):
    n = jnp.maximum(dist, 0)
    max_exact = REL_BUCKETS // 2
    nf = jnp.maximum(n, 1).astype(F32)
    large = max_exact + (jnp.log(nf / max_exact) / math.log(REL_MAX_DIST / max_exact)
                         * (REL_BUCKETS - max_exact)).astype(jnp.int32)
    large = jnp.minimum(large, REL_BUCKETS - 1)
    return jnp.where(n < max_exact, n, large)


def _far_bucket_from():
    max_exact = REL_BUCKETS // 2
    n = np.arange(1, 4 * REL_MAX_DIST, dtype=np.float32)
    large = max_exact + (np.log(n / max_exact) / math.log(REL_MAX_DIST / max_exact)
                         * (REL_BUCKETS - max_exact)).astype(np.int32)
    bucket = np.where(n < max_exact, n, np.minimum(large, REL_BUCKETS - 1))
    return int(np.nonzero(bucket < REL_BUCKETS - 1)[0].max()) + 2


_FAR_BUCKET_FROM = _far_bucket_from()


def _bias_tiles(table):
    i = jnp.arange(ATT_BLK)[:, None]
    j = jnp.arange(ATT_BLK)[None, :]
    assert _FAR_BUCKET_FROM <= ATT_BLK + 1
    far = table[REL_BUCKETS - 1].astype(F32)
    table2 = (table.astype(F32) - far[None, :]).T * LOG2E
    tiles = []
    for offset in (ATT_BLK, 0):
        dist = i - j + offset
        bucket = _t5_bucket(dist)
        t = jnp.zeros((table2.shape[0], ATT_BLK, ATT_BLK), F32)
        for b in range(REL_BUCKETS):
            t = jnp.where(bucket[None] == b, table2[:, b, None, None], t)
        tiles.append(jnp.where(dist >= 0, t, NEG))
    return jnp.stack(tiles)


def _retention_constants(tp):
    nh, dk = C_HEADS, C_HEAD_DIM
    log_gamma = jnp.log1p(-jnp.exp2(-5.0 - jnp.arange(nh, dtype=F32)))
    i = jnp.arange(RET_BLK, dtype=F32)
    diff = i[:, None] - i[None, :]
    dmat = jnp.where(diff >= 0, jnp.exp(log_gamma[:, None, None] * jnp.maximum(diff, 0.0)), 0.0)
    per_lane = lambda a: jnp.repeat(a, dk, axis=-1)
    qdec = per_lane(jnp.exp(log_gamma[None, :] * (i[:, None] + 1.0)))
    kdec = per_lane(jnp.exp(log_gamma[None, :] * (RET_BLK - 1.0 - i[:, None])))
    sdec = per_lane(jnp.exp(log_gamma * RET_BLK)[None, :])
    inv = ROPE_BASE ** (-jnp.arange(0, dk, 2, dtype=F32) / dk)
    ang = jnp.arange(tp, dtype=F32)[:, None] * inv[None, :]
    cos, sin = jnp.cos(ang), jnp.sin(ang)
    cos_full = jnp.tile(jnp.concatenate([cos, cos], axis=-1), (1, nh))
    sin_full = jnp.tile(jnp.concatenate([-sin, sin], axis=-1), (1, nh))
    return cos_full, sin_full, dmat, qdec, kdec, sdec


def _split_w_in(w):
    parts, acc = [], 0
    for size in IN_SIZES:
        parts.append(w[:, acc:acc + size])
        acc += size
    return parts


def kernel(x, meta, rel_bias, w_in, norm_mix, diff_lambda, diff_norm, ret_norm, w_out, norm_ff, w_ff1, w_ff2,
           final_norm):
    bsz, s_len, _ = x.shape
    depth = w_in.shape[0]
    t_len = s_len + N_META
    tp = -(-t_len // ATT_BLK) * ATT_BLK
    topk = min(TOPK_MAX, s_len // 4)
    nq = tp // ATT_BLK

    h = jnp.concatenate([
        jnp.broadcast_to(meta.astype(x.dtype)[None], (bsz, N_META, D_MODEL)), x,
        jnp.zeros((bsz, tp - t_len, D_MODEL), x.dtype)], axis=1).reshape(bsz * tp, D_MODEL)

    bias_a = _bias_tiles(rel_bias[:, :A_HEADS])
    bias_b = _bias_tiles(rel_bias[:, A_HEADS:])
    cos, sin, dmat, qdec, kdec, sdec = _retention_constants(tp)
    row = lambda v: v.reshape(1, -1).astype(F32)

    for l in range(depth):
        qa, ka, va, qi, ki, wi, qb, kb, vb, qc, kc, vc, gc = _split_w_in(w_in[l])
        w16 = jnp.concatenate([qa, ka, va, qi, qb, kb, vb, ki, ki], axis=1).astype(BF16)
        w32 = jnp.concatenate([qc, kc, vc, gc, wi, jnp.zeros((D_MODEL, LANES - IDX_HEADS), F32)],
                              axis=1).astype(BF16)
        p16, p32 = _inproj(h, row(norm_mix[l]), w16, w32)
        p16 = p16.reshape(bsz, tp, P16_COLS)
        p32 = p32.reshape(bsz, tp, P32_COLS)

        keys, tau, ntake, tie = _indexer(p16, p32, topk)
        tieflag = jnp.max(tie.reshape(bsz * nq, ATT_BLK), axis=1)
        a_out = _sparse_attn(p16, keys, tau, ntake, tieflag, bias_a)

        lambda_init = 0.8 - 0.6 * math.exp(-0.3 * l)
        lp = diff_lambda[l].astype(F32)
        lam = jnp.exp(jnp.sum(lp[0] * lp[1])) - jnp.exp(jnp.sum(lp[2] * lp[3])) + lambda_init
        g_pair = row(jnp.concatenate([diff_norm[l], diff_norm[l]]))
        b_out = _diff_attn(p16, lam.reshape(1), bias_b, g_pair, 1.0 - lambda_init)

        c_out = _retention(p32, cos, sin, dmat, qdec, kdec, sdec, row(ret_norm[l]))

        last = l == depth - 1
        h = _mix_mlp(h, a_out.reshape(bsz * tp, A_W), b_out.reshape(bsz * tp, B_W), c_out.reshape(bsz * tp, C_W),
                     w_out[l].astype(BF16), row(norm_ff[l]), w_ff1[l].astype(BF16), w_ff2[l].astype(BF16),
                     row(final_norm), final_norm=last)

    return h.reshape(bsz, tp, D_MODEL)[:, N_META:t_len]
```

```python
import functools
import math

import numpy as np
import jax
import jax.numpy as jnp
from jax import lax
from jax.experimental import pallas as pl
from jax.experimental.pallas import tpu as pltpu

D_MODEL = 1024
N_META = 16
A_HEADS = 8
A_HEAD_DIM = 64
A_W = A_HEADS * A_HEAD_DIM
IDX_HEADS = 8
IDX_DIM = 64
TOPK_MAX = 256
B_HEADS = 4
B_HEAD_DIM = 64
B_HALF = 32
B_W = B_HEADS * B_HEAD_DIM
C_HEADS = 4
C_HEAD_DIM = 64
C_W = C_HEADS * C_HEAD_DIM
ROPE_BASE = 10000.0
REL_BUCKETS = 32
REL_MAX_DIST = 128
D_FF = 4 * D_MODEL
EPS = 1e-6
IN_SIZES = (A_W, A_W, A_W, IDX_HEADS * IDX_DIM, IDX_DIM, IDX_HEADS, B_W, B_W, B_W, C_W, C_W, C_W, C_W)

LANES = 128
ATT_BLK = 512
IDX_ROWS = 128
IDX_CHUNK = 512
BISECT_FIXED_STEPS = 16
SMALLEST_NORMAL = float(np.finfo(np.float32).tiny)
RET_BLK = 256
ROW_TILE = 512
FF_CHUNK = 1024
VMEM_LIMIT_BYTES = 56 * 1024 * 1024

NEG = -1e30
LOG2E = math.log2(math.e)

P16_COLS = 4 * A_W + 3 * B_W + 2 * IDX_DIM
P32_COLS = 4 * C_W + LANES

BF16 = jnp.bfloat16
F32 = jnp.float32


def _cparams(sem):
    return pltpu.CompilerParams(dimension_semantics=sem, vmem_limit_bytes=VMEM_LIMIT_BYTES)


def _dot_nt(a, b):
    return lax.dot_general(a, b, (((1,), (1,)), ((), ())), preferred_element_type=F32)


def _dot(a, b):
    return jnp.dot(a, b, preferred_element_type=F32)


def _inproj_kernel(h_ref, g_ref, w16_ref, w32_ref, o16_ref, o32_ref):
    x = h_ref[...]
    ms = jnp.mean(x * x, axis=-1, keepdims=True)
    u = ((x * lax.rsqrt(ms + EPS)) * g_ref[...]).astype(BF16)
    o16_ref[...] = _dot(u, w16_ref[...]).astype(BF16)
    o32_ref[...] = _dot(u, w32_ref[...])


def _inproj(h2d, g, w16, w32):
    rows = h2d.shape[0]
    return pl.pallas_call(
        _inproj_kernel,
        grid=(rows // ROW_TILE,),
        in_specs=[
            pl.BlockSpec((ROW_TILE, D_MODEL), lambda i: (i, 0)),
            pl.BlockSpec((1, D_MODEL), lambda i: (0, 0)),
            pl.BlockSpec((D_MODEL, P16_COLS), lambda i: (0, 0)),
            pl.BlockSpec((D_MODEL, P32_COLS), lambda i: (0, 0)),
        ],
        out_specs=[
            pl.BlockSpec((ROW_TILE, P16_COLS), lambda i: (i, 0)),
            pl.BlockSpec((ROW_TILE, P32_COLS), lambda i: (i, 0)),
        ],
        out_shape=[
            jax.ShapeDtypeStruct((rows, P16_COLS), BF16),
            jax.ShapeDtypeStruct((rows, P32_COLS), F32),
        ],
        compiler_params=_cparams(("arbitrary",)),
        name="inproj",
    )(h2d, g, w16, w32)


def _indexer_kernel(qi_ref, ki_ref, wi_ref, sc_ref, tau_ref, ntake_ref, tie_ref, qm_ref, *, topk, n_chunks):
    i = pl.program_id(1)
    row0 = i * IDX_ROWS
    n_live = (row0 + IDX_ROWS - 1) // IDX_CHUNK + 1
    n_tiles = IDX_CHUNK // LANES
    k_f = float(topk)

    lane = lax.broadcasted_iota(jnp.int32, (IDX_ROWS, LANES), 1)
    for h in range(IDX_HEADS):
        pair = qi_ref[:, (h // 2) * LANES:(h // 2 + 1) * LANES]
        keep = (lane >= IDX_DIM) if h % 2 else (lane < IDX_DIM)
        qm_ref[h] = jnp.where(keep, pair, jnp.zeros_like(pair))

    w = wi_ref[...] * (IDX_HEADS ** -0.5 * IDX_DIM ** -0.5)
    wcol = [w[:, h:h + 1] for h in range(IDX_HEADS)]
    row = row0 + lax.broadcasted_iota(jnp.int32, (IDX_ROWS, 1), 0)
    col_in_chunk = lax.broadcasted_iota(jnp.int32, (IDX_ROWS, IDX_CHUNK), 1)

    def score_chunk(c, carry):
        lo, hi = carry
        start = pl.multiple_of(c * IDX_CHUNK, IDX_CHUNK)
        kc = ki_ref[pl.ds(start, IDX_CHUNK), :]
        acc = jnp.zeros((IDX_ROWS, IDX_CHUNK), F32)
        for h in range(IDX_HEADS):
            acc = acc + wcol[h] * jnp.maximum(_dot_nt(qm_ref[h], kc), 0.0)
        causal = col_in_chunk + start <= row
        sc = jnp.where(causal, acc, -jnp.inf)
        sc_ref[:, pl.ds(start, IDX_CHUNK)] = sc
        for_min = jnp.where(causal, acc, jnp.inf)
        for t in range(n_tiles):
            lo = jnp.minimum(lo, for_min[:, t * LANES:(t + 1) * LANES])
            hi = jnp.maximum(hi, sc[:, t * LANES:(t + 1) * LANES])
        return lo, hi

    lo_l, hi_l = lax.fori_loop(0, n_live, score_chunk, (jnp.full((IDX_ROWS, LANES), jnp.inf, F32),
                                                        jnp.full((IDX_ROWS, LANES), -jnp.inf, F32)))
    lo0 = jnp.min(lo_l, axis=1, keepdims=True)
    hi0 = jnp.max(hi_l, axis=1, keepdims=True)

    def fill_chunk(c, carry):
        start = pl.multiple_of(c * IDX_CHUNK, IDX_CHUNK)
        sc_ref[:, pl.ds(start, IDX_CHUNK)] = jnp.full((IDX_ROWS, IDX_CHUNK), -jnp.inf, F32)
        return carry

    lax.fori_loop(n_live, n_chunks, fill_chunk, 0)

    def count(pred):
        def body(c, acc):
            start = pl.multiple_of(c * IDX_CHUNK, IDX_CHUNK)
            m = jnp.where(pred(sc_ref[:, pl.ds(start, IDX_CHUNK)]), 1.0, 0.0)
            for t in range(n_tiles):
                acc = acc + m[:, t * LANES:(t + 1) * LANES]
            return acc
        acc = lax.fori_loop(0, n_live, body, jnp.zeros((IDX_ROWS, LANES), F32))
        return jnp.sum(acc, axis=1, keepdims=True)

    n_valid = (row + 1).astype(F32)
    keeps_all = n_valid <= k_f
    tied_at_max = count(lambda s: s >= hi0) >= k_f
    tau0 = jnp.where(keeps_all, jnp.float32(jnp.finfo(jnp.float32).min), hi0)
    active0 = jnp.where(keeps_all | tied_at_max, 0.0, 1.0)

    def unfinished(state):
        return jnp.sum(state[3]) > 0.0

    def bisect(state):
        lo, hi, tau, active = state
        mid = 0.5 * lo + 0.5 * hi
        mid = jnp.where((lo < 0.0) & (hi > 0.0), 0.0, mid)
        mid = jnp.where((lo == 0.0) & (hi > SMALLEST_NORMAL), SMALLEST_NORMAL, mid)
        collapsed = (mid <= lo) | (mid >= hi)
        cnt = count(lambda s: s >= mid)
        hit = cnt == k_f
        is_active = active > 0.0
        tau = jnp.where(is_active, jnp.where(collapsed, lo, jnp.where(hit, mid, tau)), tau)
        go_on = is_active & jnp.logical_not(collapsed | hit)
        lo = jnp.where(go_on & (cnt > k_f), mid, lo)
        hi = jnp.where(go_on & (cnt < k_f), mid, hi)
        return lo, hi, tau, jnp.where(go_on, 1.0, 0.0)

    state = lax.fori_loop(0, BISECT_FIXED_STEPS, lambda _, st: bisect(st), (lo0, hi0, tau0, active0))
    _, _, tau, _ = lax.while_loop(unfinished, bisect, state)
    n_gt = count(lambda s: s > tau)
    n_ge = count(lambda s: s >= tau)
    tau_ref[...] = tau
    ntake_ref[...] = k_f - n_gt
    tie_ref[...] = (n_ge > k_f).astype(jnp.int32)


def _indexer(p16, p32, topk):
    bsz, tp, _ = p16.shape
    n_chunks = tp // IDX_CHUNK
    kern = functools.partial(_indexer_kernel, topk=topk, n_chunks=n_chunks)
    col = lambda n: pl.BlockSpec((None, IDX_ROWS, 1), lambda b, i: (b, i, 0))
    return pl.pallas_call(
        kern,
        grid=(bsz, tp // IDX_ROWS),
        in_specs=[
            pl.BlockSpec((None, IDX_ROWS, A_W), lambda b, i: (b, i, 3)),
            pl.BlockSpec((None, tp, LANES), lambda b, i: (b, 0, (P16_COLS - LANES) // LANES)),
            pl.BlockSpec((None, IDX_ROWS, LANES), lambda b, i: (b, i, 4 * C_W // LANES)),
        ],
        out_specs=[
            pl.BlockSpec((None, IDX_ROWS, tp), lambda b, i: (b, i, 0)),
            col(0), col(1), col(2),
        ],
        out_shape=[
            jax.ShapeDtypeStruct((bsz, tp, tp), F32),
            jax.ShapeDtypeStruct((bsz, tp, 1), F32),
            jax.ShapeDtypeStruct((bsz, tp, 1), F32),
            jax.ShapeDtypeStruct((bsz, tp, 1), jnp.int32),
        ],
        scratch_shapes=[pltpu.VMEM((IDX_HEADS, IDX_ROWS, LANES), BF16)],
        compiler_params=_cparams(("arbitrary", "arbitrary")),
        name="indexer",
    )(p16, p16, p32)


def _lane_masked_queries(q_ref, qm_ref, n_slots, width, scale2):
    per_group = LANES // width
    lane = lax.broadcasted_iota(jnp.int32, (ATT_BLK, LANES), 1)
    for s in range(n_slots):
        g, r = divmod(s, per_group)
        grp = q_ref[:, g * LANES:(g + 1) * LANES].astype(F32) * scale2
        keep = (lane >= r * width) & (lane < (r + 1) * width)
        qm_ref[s] = jnp.where(keep, grp, 0.0).astype(BF16)


def _flash_update(s, logit2, v_ones, m_ref, acc_ref):
    m_prev = m_ref[s]
    m_new = jnp.maximum(m_prev, jnp.max(logit2, axis=1, keepdims=True))
    alpha = jnp.exp2(m_prev - m_new)
    p = jnp.concatenate([jnp.exp2(logit2[:, t * LANES:(t + 1) * LANES] - m_new)
                         for t in range(logit2.shape[1] // LANES)], axis=1).astype(BF16)
    acc_ref[s] = alpha * acc_ref[s] + _dot(p, v_ones)
    m_ref[s] = m_new


def _head_values_and_ones(v_grp, upper_half):
    lane = lax.broadcasted_iota(jnp.int32, v_grp.shape, 1)
    own = (lane >= LANES // 2) if upper_half else (lane < LANES // 2)
    return jnp.where(own, v_grp, jnp.ones_like(v_grp))


def _flash_init(m_ref, acc_ref):
    m_ref[...] = jnp.full(m_ref.shape, NEG, F32)
    acc_ref[...] = jnp.zeros(acc_ref.shape, F32)


def _flash_result(s, acc_ref):
    acc = acc_ref[s]
    return acc / pltpu.roll(acc, LANES // 2, axis=1)


def _triangle_steps(nq):
    qs, ks, kinds = [], [], []
    for q in range(nq):
        for k in range(q + 1):
            qs.append(q)
            ks.append(k)
            kinds.append(2 if k == q else (1 if k == q - 1 else 0))
    return (jnp.asarray(qs, jnp.int32), jnp.asarray(ks, jnp.int32), jnp.asarray(kinds, jnp.int32))


def _sparse_attn_kernel(qmap, kmap, kind, tieflag, q_ref, k_ref, v_ref, sc_ref, tau_ref, ntake_ref, bias_ref,
                        o_ref, qm_ref, m_ref, acc_ref, mask_ref, carry_ref, *, nq):
    b = pl.program_id(0)
    s = pl.program_id(1)
    qb = qmap[s]
    kb = kmap[s]

    @pl.when(kb == 0)
    def _():
        _lane_masked_queries(q_ref, qm_ref, A_HEADS, A_HEAD_DIM, A_HEAD_DIM ** -0.5 * LOG2E)
        _flash_init(m_ref, acc_ref)
        carry_ref[...] = jnp.zeros(carry_ref.shape, F32)

    has_ties = tieflag[b * nq + qb]

    @pl.when(has_ties == 0)
    def _():
        mask_ref[...] = jnp.where(sc_ref[...] >= tau_ref[...], 0.0, NEG)

    @pl.when(has_ties != 0)
    def _():
        sc = sc_ref[...]
        tau = tau_ref[...]
        eq = sc == tau
        eq_b = jnp.where(eq, 1.0, 0.0).astype(BF16)
        r = lax.broadcasted_iota(jnp.int32, (ATT_BLK, ATT_BLK), 0)
        c = lax.broadcasted_iota(jnp.int32, (ATT_BLK, ATT_BLK), 1)
        before = jnp.where(r < c, 1.0, 0.0).astype(BF16)
        seen = carry_ref[...] + _dot(eq_b, before)
        keep = (sc > tau) | (eq & (seen < ntake_ref[...]))
        mask_ref[...] = jnp.where(keep, 0.0, NEG)
        carry_ref[...] = carry_ref[...] + jnp.sum(eq_b.astype(F32), axis=1, keepdims=True)

    def attend(near_diagonal):
        for g in range(A_HEADS // 2):
            k_grp = k_ref[:, g * LANES:(g + 1) * LANES]
            v_grp = v_ref[:, g * LANES:(g + 1) * LANES]
            for h in (2 * g, 2 * g + 1):
                logit2 = _dot_nt(qm_ref[h], k_grp) + mask_ref[...]
                if near_diagonal:
                    logit2 = logit2 + bias_ref[h]
                _flash_update(h, logit2, _head_values_and_ones(v_grp, h % 2 == 1), m_ref, acc_ref)

    pl.when(kind[s] == 0)(lambda: attend(False))
    pl.when(kind[s] != 0)(lambda: attend(True))

    @pl.when(kb == qb)
    def _():
        lane = lax.broadcasted_iota(jnp.int32, (ATT_BLK, LANES), 1)
        for g in range(A_HEADS // 2):
            out = jnp.where(lane < A_HEAD_DIM, _flash_result(2 * g, acc_ref), _flash_result(2 * g + 1, acc_ref))
            o_ref[:, g * LANES:(g + 1) * LANES] = out.astype(o_ref.dtype)


def _sparse_attn(p16, keys, tau, ntake, tieflag, bias_tiles):
    bsz, tp, _ = p16.shape
    nq = tp // ATT_BLK
    qmap, kmap, kind = _triangle_steps(nq)
    qblk = lambda c: pl.BlockSpec((None, ATT_BLK, A_W), lambda b, s, qm, km, kd, tf: (b, qm[s], c))
    kblk = lambda c: pl.BlockSpec((None, ATT_BLK, A_W), lambda b, s, qm, km, kd, tf: (b, km[s], c))
    colblk = pl.BlockSpec((None, ATT_BLK, 1), lambda b, s, qm, km, kd, tf: (b, qm[s], 0))
    grid_spec = pltpu.PrefetchScalarGridSpec(
        num_scalar_prefetch=4,
        grid=(bsz, int(qmap.shape[0])),
        in_specs=[
            qblk(0), kblk(1), kblk(2),
            pl.BlockSpec((None, ATT_BLK, ATT_BLK), lambda b, s, qm, km, kd, tf: (b, qm[s], km[s])),
            colblk, colblk,
            pl.BlockSpec((None, A_HEADS, ATT_BLK, ATT_BLK),
                         lambda b, s, qm, km, kd, tf: (jnp.maximum(kd[s] - 1, 0), 0, 0, 0)),
        ],
        out_specs=pl.BlockSpec((None, ATT_BLK, A_W), lambda b, s, qm, km, kd, tf: (b, qm[s], 0)),
        scratch_shapes=[
            pltpu.VMEM((A_HEADS, ATT_BLK, LANES), BF16),
            pltpu.VMEM((A_HEADS, ATT_BLK, LANES), F32),
            pltpu.VMEM((A_HEADS, ATT_BLK, LANES), F32),
            pltpu.VMEM((ATT_BLK, ATT_BLK), F32),
            pltpu.VMEM((ATT_BLK, 1), F32),
        ],
    )
    return pl.pallas_call(
        functools.partial(_sparse_attn_kernel, nq=nq),
        grid_spec=grid_spec,
        out_shape=jax.ShapeDtypeStruct((bsz, tp, A_W), BF16),
        compiler_params=_cparams(("arbitrary", "arbitrary")),
        name="sparse_attn",
    )(qmap, kmap, kind, tieflag, p16, p16, p16, keys, tau, ntake, bias_tiles)


def _diff_attn_kernel(qmap, kmap, kind, lam_ref, q_ref, k_ref, v_ref, bias_ref, g_ref,
                      o_ref, qm_ref, m_ref, acc_ref, *, out_scale):
    s = pl.program_id(1)
    qb = qmap[s]
    kb = kmap[s]
    n_slots = 2 * B_HEADS

    @pl.when(kb == 0)
    def _():
        _lane_masked_queries(q_ref, qm_ref, n_slots, B_HALF, B_HALF ** -0.5 * LOG2E)
        _flash_init(m_ref, acc_ref)

    def attend(near_diagonal):
        for g in range(B_HEADS // 2):
            k_grp = k_ref[:, g * LANES:(g + 1) * LANES]
            v_grp = v_ref[:, g * LANES:(g + 1) * LANES]
            v_ones = [_head_values_and_ones(v_grp, False), _head_values_and_ones(v_grp, True)]
            for slot in range(4 * g, 4 * g + 4):
                head = slot // 2
                logit2 = _dot_nt(qm_ref[slot], k_grp)
                if near_diagonal:
                    logit2 = logit2 + bias_ref[head]
                _flash_update(slot, logit2, v_ones[head % 2], m_ref, acc_ref)

    pl.when(kind[s] == 0)(lambda: attend(False))
    pl.when(kind[s] != 0)(lambda: attend(True))

    @pl.when(kb == qb)
    def _():
        lam = lam_ref[0]
        lane = lax.broadcasted_iota(jnp.int32, (ATT_BLK, LANES), 1)
        low = lane < B_HEAD_DIM
        for g in range(B_HEADS // 2):
            heads = []
            for h in (2 * g, 2 * g + 1):
                heads.append(_flash_result(2 * h, acc_ref) - lam * _flash_result(2 * h + 1, acc_ref))
            x = jnp.where(low, heads[0], heads[1])
            sq = x * x
            ss_lo = jnp.sum(jnp.where(low, sq, 0.0), axis=1, keepdims=True)
            ss_hi = jnp.sum(jnp.where(low, 0.0, sq), axis=1, keepdims=True)
            ms = jnp.where(low, ss_lo, ss_hi) * (1.0 / B_HEAD_DIM)
            y = (x * lax.rsqrt(ms + EPS)) * g_ref[...]
            o_ref[:, g * LANES:(g + 1) * LANES] = (y * out_scale).astype(o_ref.dtype)


def _diff_attn(p16, lam, bias_tiles, g_pair, out_scale):
    bsz, tp, _ = p16.shape
    nq = tp // ATT_BLK
    qmap, kmap, kind = _triangle_steps(nq)
    base = 4 * A_W // B_W
    qblk = pl.BlockSpec((None, ATT_BLK, B_W), lambda b, s, qm, km, kd, lm: (b, qm[s], base))
    kblk = lambda c: pl.BlockSpec((None, ATT_BLK, B_W), lambda b, s, qm, km, kd, lm: (b, km[s], base + c))
    n_slots = 2 * B_HEADS
    grid_spec = pltpu.PrefetchScalarGridSpec(
        num_scalar_prefetch=4,
        grid=(bsz, int(qmap.shape[0])),
        in_specs=[
            qblk, kblk(1), kblk(2),
            pl.BlockSpec((None, B_HEADS, ATT_BLK, ATT_BLK),
                         lambda b, s, qm, km, kd, lm: (jnp.maximum(kd[s] - 1, 0), 0, 0, 0)),
            pl.BlockSpec((1, LANES), lambda b, s, qm, km, kd, lm: (0, 0)),
        ],
        out_specs=pl.BlockSpec((None, ATT_BLK, B_W), lambda b, s, qm, km, kd, lm: (b, qm[s], 0)),
        scratch_shapes=[
            pltpu.VMEM((n_slots, ATT_BLK, LANES), BF16),
            pltpu.VMEM((n_slots, ATT_BLK, LANES), F32),
            pltpu.VMEM((n_slots, ATT_BLK, LANES), F32),
        ],
    )
    return pl.pallas_call(
        functools.partial(_diff_attn_kernel, out_scale=out_scale),
        grid_spec=grid_spec,
        out_shape=jax.ShapeDtypeStruct((bsz, tp, B_W), BF16),
        compiler_params=_cparams(("arbitrary", "arbitrary")),
        name="diff_attn",
    )(qmap, kmap, kind, lam, p16, p16, p16, bias_tiles, g_pair)


def _retention_kernel(q_ref, k_ref, v_ref, gate_ref, cos_ref, sin_ref, dmat_ref, qdec_ref, kdec_ref, sdec_ref,
                      g_ref, o_ref, state_ref):
    c = pl.program_id(1)

    @pl.when(c == 0)
    def _():
        state_ref[...] = jnp.zeros(state_ref.shape, F32)

    lane = lax.broadcasted_iota(jnp.int32, (RET_BLK, C_W), 1)
    first_half = (lane % C_HEAD_DIM) < (C_HEAD_DIM // 2)

    def rope(x):
        swapped = jnp.where(first_half,
                            pltpu.roll(x, C_W - C_HEAD_DIM // 2, axis=1),
                            pltpu.roll(x, C_HEAD_DIM // 2, axis=1))
        return x * cos_ref[...] + swapped * sin_ref[...]

    q = rope(q_ref[...])
    k = rope(k_ref[...]) * (C_HEAD_DIM ** -0.5)
    q16 = q.astype(BF16)
    k16 = k.astype(BF16)
    v16 = v_ref[...].astype(BF16)

    state = state_ref[...]
    out = _dot(q16, state.astype(BF16)) * qdec_ref[...]
    for h in range(C_HEADS):
        in_head = (lane >= h * C_HEAD_DIM) & (lane < (h + 1) * C_HEAD_DIM)
        qh = jnp.where(in_head, q, 0.0).astype(BF16)
        inner = _dot_nt(qh, k16) * dmat_ref[h]
        out = out + jnp.where(in_head, _dot(inner.astype(BF16), v16), 0.0)

    kd = (k * kdec_ref[...]).astype(BF16)
    update = _dot(kd.T, v16)
    r = lax.broadcasted_iota(jnp.int32, (C_W, C_W), 0) // C_HEAD_DIM
    cc = lax.broadcasted_iota(jnp.int32, (C_W, C_W), 1) // C_HEAD_DIM
    state_ref[...] = state * sdec_ref[...] + jnp.where(r == cc, update, 0.0)

    sq = out * out
    ms = jnp.zeros_like(out)
    for h in range(C_HEADS):
        in_head = (lane >= h * C_HEAD_DIM) & (lane < (h + 1) * C_HEAD_DIM)
        ss = jnp.sum(jnp.where(in_head, sq, 0.0), axis=1, keepdims=True)
        ms = jnp.where(in_head, ss * (1.0 / C_HEAD_DIM), ms)
    normed = (out * lax.rsqrt(ms + EPS)) * g_ref[...]
    gate = gate_ref[...]
    o_ref[...] = ((gate * jax.nn.sigmoid(gate)) * normed).astype(o_ref.dtype)


def _retention(p32, cos, sin, dmat, qdec, kdec, sdec, g):
    bsz, tp, _ = p32.shape
    blk = lambda c: pl.BlockSpec((None, RET_BLK, C_W), lambda b, i: (b, i, c))
    const = lambda shape: pl.BlockSpec(shape, lambda b, i: (0,) * len(shape))
    return pl.pallas_call(
        _retention_kernel,
        grid=(bsz, tp // RET_BLK),
        in_specs=[
            blk(0), blk(1), blk(2), blk(3),
            pl.BlockSpec((RET_BLK, C_W), lambda b, i: (i, 0)),
            pl.BlockSpec((RET_BLK, C_W), lambda b, i: (i, 0)),
            const((C_HEADS, RET_BLK, RET_BLK)),
            const((RET_BLK, C_W)), const((RET_BLK, C_W)), const((1, C_W)), const((1, C_W)),
        ],
        out_specs=pl.BlockSpec((None, RET_BLK, C_W), lambda b, i: (b, i, 0)),
        out_shape=jax.ShapeDtypeStruct((bsz, tp, C_W), BF16),
        scratch_shapes=[pltpu.VMEM((C_W, C_W), F32)],
        compiler_params=_cparams(("arbitrary", "arbitrary")),
        name="retention",
    )(p32, p32, p32, p32, cos, sin, dmat, qdec, kdec, sdec, g)


def _mix_mlp_kernel(h_ref, a_ref, b_ref, c_ref, wo_ref, g_ref, w1_ref, w2_ref, gf_ref, o_ref, u_ref, acc_ref,
                    *, final_norm):
    j = pl.program_id(1)

    @pl.when(j == 0)
    def _():
        mixed = (_dot(a_ref[...], wo_ref[0:A_W, :])
                 + _dot(b_ref[...], wo_ref[A_W:A_W + B_W, :])
                 + _dot(c_ref[...], wo_ref[A_W + B_W:, :]))
        h1 = h_ref[...] + mixed
        acc_ref[...] = h1
        ms = jnp.mean(h1 * h1, axis=-1, keepdims=True)
        u_ref[...] = ((h1 * lax.rsqrt(ms + EPS)) * g_ref[...]).astype(BF16)

    t = jnp.maximum(_dot(u_ref[...], w1_ref[...]), 0.0)
    acc_ref[...] += _dot((t * t).astype(BF16), w2_ref[...])

    @pl.when(j == pl.num_programs(1) - 1)
    def _():
        y = acc_ref[...]
        if final_norm:
            ms = jnp.mean(y * y, axis=-1, keepdims=True)
            y = (y * lax.rsqrt(ms + EPS)) * gf_ref[...]
        o_ref[...] = y


def _mix_mlp(h2d, a, b, c, wo, g, w1, w2, gf, final_norm):
    rows = h2d.shape[0]
    row_blk = lambda w: pl.BlockSpec((ROW_TILE, w), lambda i, j: (i, 0))
    vec = pl.BlockSpec((1, D_MODEL), lambda i, j: (0, 0))
    return pl.pallas_call(
        functools.partial(_mix_mlp_kernel, final_norm=final_norm),
        grid=(rows // ROW_TILE, D_FF // FF_CHUNK),
        in_specs=[
            row_blk(D_MODEL), row_blk(A_W), row_blk(B_W), row_blk(C_W),
            pl.BlockSpec((D_MODEL, D_MODEL), lambda i, j: (0, 0)),
            vec,
            pl.BlockSpec((D_MODEL, FF_CHUNK), lambda i, j: (0, j)),
            pl.BlockSpec((FF_CHUNK, D_MODEL), lambda i, j: (j, 0)),
            vec,
        ],
        out_specs=row_blk(D_MODEL),
        out_shape=jax.ShapeDtypeStruct((rows, D_MODEL), F32),
        scratch_shapes=[pltpu.VMEM((ROW_TILE, D_MODEL), BF16), pltpu.VMEM((ROW_TILE, D_MODEL), F32)],
        compiler_params=_cparams(("arbitrary", "arbitrary")),
        name="mix_mlp",
    )(h2d, a, b, c, wo, g, w1, w2, gf)


def _t5_bucket(dist):
    n = jnp.maximum(dist, 0)
    max_exact = REL_BUCKETS // 2
    nf = jnp.maximum(n, 1).astype(F32)
    large = max_exact + (jnp.log(nf / max_exact) / math.log(REL_MAX_DIST / max_exact)
                         * (REL_BUCKETS - max_exact)).astype(jnp.int32)
    large = jnp.minimum(large, REL_BUCKETS - 1)
    return jnp.where(n < max_exact, n, large)


def _far_bucket_from():
    max_exact = REL_BUCKETS // 2
    n = np.arange(1, 4 * REL_MAX_DIST, dtype=np.float32)
    large = max_exact + (np.log(n / max_exact) / math.log(REL_MAX_DIST / max_exact)
                         * (REL_BUCKETS - max_exact)).astype(np.int32)
    bucket = np.where(n < max_exact, n, np.minimum(large, REL_BUCKETS - 1))
    return int(np.nonzero(bucket < REL_BUCKETS - 1)[0].max()) + 2


_FAR_BUCKET_FROM = _far_bucket_from()


def _bias_tiles(table):
    i = jnp.arange(ATT_BLK)[:, None]
    j = jnp.arange(ATT_BLK)[None, :]
    assert _FAR_BUCKET_FROM <= ATT_BLK + 1
    far = table[REL_BUCKETS - 1].astype(F32)
    table2 = (table.astype(F32) - far[None, :]).T * LOG2E
    tiles = []
    for offset in (ATT_BLK, 0):
        dist = i - j + offset
        bucket = _t5_bucket(dist)
        t = jnp.zeros((table2.shape[0], ATT_BLK, ATT_BLK), F32)
        for b in range(REL_BUCKETS):
            t = jnp.where(bucket[None] == b, table2[:, b, None, None], t)
        tiles.append(jnp.where(dist >= 0, t, NEG))
    return jnp.stack(tiles)


def _retention_constants(tp):
    nh, dk = C_HEADS, C_HEAD_DIM
    log_gamma = jnp.log1p(-jnp.exp2(-5.0 - jnp.arange(nh, dtype=F32)))
    i = jnp.arange(RET_BLK, dtype=F32)
    diff = i[:, None] - i[None, :]
    dmat = jnp.where(diff >= 0, jnp.exp(log_gamma[:, None, None] * jnp.maximum(diff, 0.0)), 0.0)
    per_lane = lambda a: jnp.repeat(a, dk, axis=-1)
    qdec = per_lane(jnp.exp(log_gamma[None, :] * (i[:, None] + 1.0)))
    kdec = per_lane(jnp.exp(log_gamma[None, :] * (RET_BLK - 1.0 - i[:, None])))
    sdec = per_lane(jnp.exp(log_gamma * RET_BLK)[None, :])
    inv = ROPE_BASE ** (-jnp.arange(0, dk, 2, dtype=F32) / dk)
    ang = jnp.arange(tp, dtype=F32)[:, None] * inv[None, :]
    cos, sin = jnp.cos(ang), jnp.sin(ang)
    cos_full = jnp.tile(jnp.concatenate([cos, cos], axis=-1), (1, nh))
    sin_full = jnp.tile(jnp.concatenate([-sin, sin], axis=-1), (1, nh))
    return cos_full, sin_full, dmat, qdec, kdec, sdec


def _split_w_in(w):
    parts, acc = [], 0
    for size in IN_SIZES:
        parts.append(w[:, acc:acc + size])
        acc += size
    return parts


def kernel(x, meta, rel_bias, w_in, norm_mix, diff_lambda, diff_norm, ret_norm, w_out, norm_ff, w_ff1, w_ff2,
           final_norm):
    bsz, s_len, _ = x.shape
    depth = w_in.shape[0]
    t_len = s_len + N_META
    tp = -(-t_len // ATT_BLK) * ATT_BLK
    topk = min(TOPK_MAX, s_len // 4)
    nq = tp // ATT_BLK

    h = jnp.concatenate([
        jnp.broadcast_to(meta.astype(x.dtype)[None], (bsz, N_META, D_MODEL)), x,
        jnp.zeros((bsz, tp - t_len, D_MODEL), x.dtype)], axis=1).reshape(bsz * tp, D_MODEL)

    bias_a = _bias_tiles(rel_bias[:, :A_HEADS])
    bias_b = _bias_tiles(rel_bias[:, A_HEADS:])
    cos, sin, dmat, qdec, kdec, sdec = _retention_constants(tp)
    row = lambda v: v.reshape(1, -1).astype(F32)

    for l in range(depth):
        qa, ka, va, qi, ki, wi, qb, kb, vb, qc, kc, vc, gc = _split_w_in(w_in[l])
        w16 = jnp.concatenate([qa, ka, va, qi, qb, kb, vb, ki, ki], axis=1).astype(BF16)
        w32 = jnp.concatenate([qc, kc, vc, gc, wi, jnp.zeros((D_MODEL, LANES - IDX_HEADS), F32)],
                              axis=1).astype(BF16)
        p16, p32 = _inproj(h, row(norm_mix[l]), w16, w32)
        p16 = p16.reshape(bsz, tp, P16_COLS)
        p32 = p32.reshape(bsz, tp, P32_COLS)

        keys, tau, ntake, tie = _indexer(p16, p32, topk)
        tieflag = jnp.max(tie.reshape(bsz * nq, ATT_BLK), axis=1)
        a_out = _sparse_attn(p16, keys, tau, ntake, tieflag, bias_a)

        lambda_init = 0.8 - 0.6 * math.exp(-0.3 * l)
        lp = diff_lambda[l].astype(F32)
        lam = jnp.exp(jnp.sum(lp[0] * lp[1])) - jnp.exp(jnp.sum(lp[2] * lp[3])) + lambda_init
        g_pair = row(jnp.concatenate([diff_norm[l], diff_norm[l]]))
        b_out = _diff_attn(p16, lam.reshape(1), bias_b, g_pair, 1.0 - lambda_init)

        c_out = _retention(p32, cos, sin, dmat, qdec, kdec, sdec, row(ret_norm[l]))

        last = l == depth - 1
        h = _mix_mlp(h, a_out.reshape(bsz * tp, A_W), b_out.reshape(bsz * tp, B_W), c_out.reshape(bsz * tp, C_W),
                     w_out[l].astype(BF16), row(norm_ff[l]), w_ff1[l].astype(BF16), w_ff2[l].astype(BF16),
                     row(final_norm), final_norm=last)

    return h.reshape(bsz, tp, D_MODEL)[:, N_META:t_len]
```

```python
import functools
import math

import numpy as np
import jax
import jax.numpy as jnp
from jax import lax
from jax.experimental import pallas as pl
from jax.experimental.pallas import tpu as pltpu

D_MODEL = 1024
N_META = 16
A_HEADS = 8
A_HEAD_DIM = 64
A_W = A_HEADS * A_HEAD_DIM
IDX_HEADS = 8
IDX_DIM = 64
TOPK_MAX = 256
B_HEADS = 4
B_HEAD_DIM = 64
B_HALF = 32
B_W = B_HEADS * B_HEAD_DIM
C_HEADS = 4
C_HEAD_DIM = 64
C_W = C_HEADS * C_HEAD_DIM
ROPE_BASE = 10000.0
REL_BUCKETS = 32
REL_MAX_DIST = 128
D_FF = 4 * D_MODEL
EPS = 1e-6
IN_SIZES = (A_W, A_W, A_W, IDX_HEADS * IDX_DIM, IDX_DIM, IDX_HEADS, B_W, B_W, B_W, C_W, C_W, C_W, C_W)

LANES = 128
ATT_BLK = 512
IDX_ROWS = 128
IDX_CHUNK = 512
BISECT_FIXED_STEPS = 16
SMALLEST_NORMAL = float(np.finfo(np.float32).tiny)
RET_BLK = 256
ROW_TILE = 512
FF_CHUNK = 1024
VMEM_LIMIT_BYTES = 56 * 1024 * 1024

NEG = -1e30
LOG2E = math.log2(math.e)

P16_COLS = 4 * A_W + 3 * B_W + 2 * IDX_DIM
P32_COLS = 4 * C_W + LANES

BF16 = jnp.bfloat16
F32 = jnp.float32


def _cparams(sem):
    return pltpu.CompilerParams(dimension_semantics=sem, vmem_limit_bytes=VMEM_LIMIT_BYTES)


def _dot_nt(a, b):
    return lax.dot_general(a, b, (((1,), (1,)), ((), ())), preferred_element_type=F32)


def _dot(a, b):
    return jnp.dot(a, b, preferred_element_type=F32)


def _inproj_kernel(h_ref, g_ref, w16_ref, w32_ref, o16_ref, o32_ref):
    x = h_ref[...]
    ms = jnp.mean(x * x, axis=-1, keepdims=True)
    u = ((x * lax.rsqrt(ms + EPS)) * g_ref[...]).astype(BF16)
    o16_ref[...] = _dot(u, w16_ref[...]).astype(BF16)
    o32_ref[...] = _dot(u, w32_ref[...])


def _inproj(h2d, g, w16, w32):
    rows = h2d.shape[0]
    return pl.pallas_call(
        _inproj_kernel,
        grid=(rows // ROW_TILE,),
        in_specs=[
            pl.BlockSpec((ROW_TILE, D_MODEL), lambda i: (i, 0)),
            pl.BlockSpec((1, D_MODEL), lambda i: (0, 0)),
            pl.BlockSpec((D_MODEL, P16_COLS), lambda i: (0, 0)),
            pl.BlockSpec((D_MODEL, P32_COLS), lambda i: (0, 0)),
        ],
        out_specs=[
            pl.BlockSpec((ROW_TILE, P16_COLS), lambda i: (i, 0)),
            pl.BlockSpec((ROW_TILE, P32_COLS), lambda i: (i, 0)),
        ],
        out_shape=[
            jax.ShapeDtypeStruct((rows, P16_COLS), BF16),
            jax.ShapeDtypeStruct((rows, P32_COLS), F32),
        ],
        compiler_params=_cparams(("arbitrary",)),
        name="inproj",
    )(h2d, g, w16, w32)


def _indexer_kernel(qi_ref, ki_ref, wi_ref, sc_ref, tau_ref, ntake_ref, tie_ref, qm_ref, *, topk, n_chunks):
    i = pl.program_id(1)
    row0 = i * IDX_ROWS
    n_live = (row0 + IDX_ROWS - 1) // IDX_CHUNK + 1
    n_tiles = IDX_CHUNK // LANES
    k_f = float(topk)

    lane = lax.broadcasted_iota(jnp.int32, (IDX_ROWS, LANES), 1)
    for h in range(IDX_HEADS):
        pair = qi_ref[:, (h // 2) * LANES:(h // 2 + 1) * LANES]
        keep = (lane >= IDX_DIM) if h % 2 else (lane < IDX_DIM)
        qm_ref[h] = jnp.where(keep, pair, jnp.zeros_like(pair))

    w = wi_ref[...] * (IDX_HEADS ** -0.5 * IDX_DIM ** -0.5)
    wcol = [w[:, h:h + 1] for h in range(IDX_HEADS)]
    row = row0 + lax.broadcasted_iota(jnp.int32, (IDX_ROWS, 1), 0)
    col_in_chunk = lax.broadcasted_iota(jnp.int32, (IDX_ROWS, IDX_CHUNK), 1)

    def score_chunk(c, carry):
        lo, hi = carry
        start = pl.multiple_of(c * IDX_CHUNK, IDX_CHUNK)
        kc = ki_ref[pl.ds(start, IDX_CHUNK), :]
        acc = jnp.zeros((IDX_ROWS, IDX_CHUNK), F32)
        for h in range(IDX_HEADS):
            acc = acc + wcol[h] * jnp.maximum(_dot_nt(qm_ref[h], kc), 0.0)
        causal = col_in_chunk + start <= row
        sc = jnp.where(causal, acc, -jnp.inf)
        sc_ref[:, pl.ds(start, IDX_CHUNK)] = sc
        for_min = jnp.where(causal, acc, jnp.inf)
        for t in range(n_tiles):
            lo = jnp.minimum(lo, for_min[:, t * LANES:(t + 1) * LANES])
            hi = jnp.maximum(hi, sc[:, t * LANES:(t + 1) * LANES])
        return lo, hi

    lo_l, hi_l = lax.fori_loop(0, n_live, score_chunk, (jnp.full((IDX_ROWS, LANES), jnp.inf, F32),
                                                        jnp.full((IDX_ROWS, LANES), -jnp.inf, F32)))
    lo0 = jnp.min(lo_l, axis=1, keepdims=True)
    hi0 = jnp.max(hi_l, axis=1, keepdims=True)

    def fill_chunk(c, carry):
        start = pl.multiple_of(c * IDX_CHUNK, IDX_CHUNK)
        sc_ref[:, pl.ds(start, IDX_CHUNK)] = jnp.full((IDX_ROWS, IDX_CHUNK), -jnp.inf, F32)
        return carry

    lax.fori_loop(n_live, n_chunks, fill_chunk, 0)

    def count(pred):
        def body(c, acc):
            start = pl.multiple_of(c * IDX_CHUNK, IDX_CHUNK)
            sc = sc_ref[:, pl.ds(start, IDX_CHUNK)]
            for t in range(n_tiles):
                acc = jnp.where(pred(sc[:, t * LANES:(t + 1) * LANES]), acc + 1.0, acc)
            return acc
        acc = lax.fori_loop(0, n_live, body, jnp.zeros((IDX_ROWS, LANES), F32))
        return jnp.sum(acc, axis=1, keepdims=True)

    n_valid = (row + 1).astype(F32)
    keeps_all = n_valid <= k_f
    tied_at_max = count(lambda s: s >= hi0) >= k_f
    tau0 = jnp.where(keeps_all, jnp.float32(jnp.finfo(jnp.float32).min), hi0)
    active0 = jnp.where(keeps_all | tied_at_max, 0.0, 1.0)

    def unfinished(state):
        return jnp.sum(state[3]) > 0.0

    def bisect(state):
        lo, hi, tau, active = state
        mid = 0.5 * lo + 0.5 * hi
        mid = jnp.where((lo < 0.0) & (hi > 0.0), 0.0, mid)
        mid = jnp.where((lo == 0.0) & (hi > SMALLEST_NORMAL), SMALLEST_NORMAL, mid)
        collapsed = (mid <= lo) | (mid >= hi)
        cnt = count(lambda s: s >= mid)
        hit = cnt == k_f
        is_active = active > 0.0
        tau = jnp.where(is_active, jnp.where(collapsed, lo, jnp.where(hit, mid, tau)), tau)
        go_on = is_active & jnp.logical_not(collapsed | hit)
        lo = jnp.where(go_on & (cnt > k_f), mid, lo)
        hi = jnp.where(go_on & (cnt < k_f), mid, hi)
        return lo, hi, tau, jnp.where(go_on, 1.0, 0.0)

    state = lax.fori_loop(0, BISECT_FIXED_STEPS, lambda _, st: bisect(st), (lo0, hi0, tau0, active0))
    _, _, tau, _ = lax.while_loop(unfinished, bisect, state)
    n_gt = count(lambda s: s > tau)
    n_ge = count(lambda s: s >= tau)
    tau_ref[...] = tau
    ntake_ref[...] = k_f - n_gt
    tie_ref[...] = (n_ge > k_f).astype(jnp.int32)


def _indexer(p16, p32, topk):
    bsz, tp, _ = p16.shape
    n_chunks = tp // IDX_CHUNK
    kern = functools.partial(_indexer_kernel, topk=topk, n_chunks=n_chunks)
    col = lambda n: pl.BlockSpec((None, IDX_ROWS, 1), lambda b, i: (b, i, 0))
    return pl.pallas_call(
        kern,
        grid=(bsz, tp // IDX_ROWS),
        in_specs=[
            pl.BlockSpec((None, IDX_ROWS, A_W), lambda b, i: (b, i, 3)),
            pl.BlockSpec((None, tp, LANES), lambda b, i: (b, 0, (P16_COLS - LANES) // LANES)),
            pl.BlockSpec((None, IDX_ROWS, LANES), lambda b, i: (b, i, 4 * C_W // LANES)),
        ],
        out_specs=[
            pl.BlockSpec((None, IDX_ROWS, tp), lambda b, i: (b, i, 0)),
            col(0), col(1), col(2),
        ],
        out_shape=[
            jax.ShapeDtypeStruct((bsz, tp, tp), F32),
            jax.ShapeDtypeStruct((bsz, tp, 1), F32),
            jax.ShapeDtypeStruct((bsz, tp, 1), F32),
            jax.ShapeDtypeStruct((bsz, tp, 1), jnp.int32),
        ],
        scratch_shapes=[pltpu.VMEM((IDX_HEADS, IDX_ROWS, LANES), BF16)],
        compiler_params=_cparams(("arbitrary", "arbitrary")),
        name="indexer",
    )(p16, p16, p32)


def _lane_masked_queries(q_ref, qm_ref, n_slots, width, scale2):
    per_group = LANES // width
    lane = lax.broadcasted_iota(jnp.int32, (ATT_BLK, LANES), 1)
    for s in range(n_slots):
        g, r = divmod(s, per_group)
        grp = q_ref[:, g * LANES:(g + 1) * LANES].astype(F32) * scale2
        keep = (lane >= r * width) & (lane < (r + 1) * width)
        qm_ref[s] = jnp.where(keep, grp, 0.0).astype(BF16)


def _flash_update(s, logit2, v_ones, m_ref, acc_ref):
    m_prev = m_ref[s]
    m_new = jnp.maximum(m_prev, jnp.max(logit2, axis=1, keepdims=True))
    alpha = jnp.exp2(m_prev - m_new)
    p = jnp.concatenate([jnp.exp2(logit2[:, t * LANES:(t + 1) * LANES] - m_new)
                         for t in range(logit2.shape[1] // LANES)], axis=1).astype(BF16)
    acc_ref[s] = jnp.concatenate([alpha, alpha], axis=1) * acc_ref[s] + _dot(p, v_ones)
    m_ref[s] = m_new


def _with_ones(v_grp):
    return jnp.concatenate([v_grp, jnp.ones(v_grp.shape, v_grp.dtype)], axis=1)


def _flash_init(m_ref, acc_ref):
    m_ref[...] = jnp.full(m_ref.shape, NEG, F32)
    acc_ref[...] = jnp.zeros(acc_ref.shape, F32)


def _flash_result(s, acc_ref):
    acc = acc_ref[s]
    return acc[:, :LANES] / acc[:, LANES:]


def _triangle_steps(nq):
    qs, ks, kinds = [], [], []
    for q in range(nq):
        for k in range(q + 1):
            qs.append(q)
            ks.append(k)
            kinds.append(2 if k == q else (1 if k == q - 1 else 0))
    return (jnp.asarray(qs, jnp.int32), jnp.asarray(ks, jnp.int32), jnp.asarray(kinds, jnp.int32))


def _sparse_attn_kernel(qmap, kmap, kind, tieflag, q_ref, k_ref, v_ref, sc_ref, tau_ref, ntake_ref, bias_ref,
                        o_ref, qm_ref, m_ref, acc_ref, mask_ref, carry_ref, *, nq):
    b = pl.program_id(0)
    s = pl.program_id(1)
    qb = qmap[s]
    kb = kmap[s]

    @pl.when(kb == 0)
    def _():
        _lane_masked_queries(q_ref, qm_ref, A_HEADS, A_HEAD_DIM, A_HEAD_DIM ** -0.5 * LOG2E)
        _flash_init(m_ref, acc_ref)
        carry_ref[...] = jnp.zeros(carry_ref.shape, F32)

    has_ties = tieflag[b * nq + qb]

    @pl.when(has_ties == 0)
    def _():
        mask_ref[...] = jnp.where(sc_ref[...] >= tau_ref[...], 0.0, NEG)

    @pl.when(has_ties != 0)
    def _():
        sc = sc_ref[...]
        tau = tau_ref[...]
        eq = sc == tau
        eq_b = jnp.where(eq, 1.0, 0.0).astype(BF16)
        r = lax.broadcasted_iota(jnp.int32, (ATT_BLK, ATT_BLK), 0)
        c = lax.broadcasted_iota(jnp.int32, (ATT_BLK, ATT_BLK), 1)
        before = jnp.where(r < c, 1.0, 0.0).astype(BF16)
        seen = carry_ref[...] + _dot(eq_b, before)
        keep = (sc > tau) | (eq & (seen < ntake_ref[...]))
        mask_ref[...] = jnp.where(keep, 0.0, NEG)
        carry_ref[...] = carry_ref[...] + jnp.sum(eq_b.astype(F32), axis=1, keepdims=True)

    def attend(near_diagonal):
        for g in range(A_HEADS // 2):
            k_grp = k_ref[:, g * LANES:(g + 1) * LANES]
            v_ones = _with_ones(v_ref[:, g * LANES:(g + 1) * LANES])
            for h in (2 * g, 2 * g + 1):
                logit2 = _dot_nt(qm_ref[h], k_grp) + mask_ref[...]
                if near_diagonal:
                    logit2 = logit2 + bias_ref[h]
                _flash_update(h, logit2, v_ones, m_ref, acc_ref)

    pl.when(kind[s] == 0)(lambda: attend(False))
    pl.when(kind[s] != 0)(lambda: attend(True))

    @pl.when(kb == qb)
    def _():
        lane = lax.broadcasted_iota(jnp.int32, (ATT_BLK, LANES), 1)
        for g in range(A_HEADS // 2):
            out = jnp.where(lane < A_HEAD_DIM, _flash_result(2 * g, acc_ref), _flash_result(2 * g + 1, acc_ref))
            o_ref[:, g * LANES:(g + 1) * LANES] = out.astype(o_ref.dtype)


def _sparse_attn(p16, keys, tau, ntake, tieflag, bias_tiles):
    bsz, tp, _ = p16.shape
    nq = tp // ATT_BLK
    qmap, kmap, kind = _triangle_steps(nq)
    qblk = lambda c: pl.BlockSpec((None, ATT_BLK, A_W), lambda b, s, qm, km, kd, tf: (b, qm[s], c))
    kblk = lambda c: pl.BlockSpec((None, ATT_BLK, A_W), lambda b, s, qm, km, kd, tf: (b, km[s], c))
    colblk = pl.BlockSpec((None, ATT_BLK, 1), lambda b, s, qm, km, kd, tf: (b, qm[s], 0))
    grid_spec = pltpu.PrefetchScalarGridSpec(
        num_scalar_prefetch=4,
        grid=(bsz, int(qmap.shape[0])),
        in_specs=[
            qblk(0), kblk(1), kblk(2),
            pl.BlockSpec((None, ATT_BLK, ATT_BLK), lambda b, s, qm, km, kd, tf: (b, qm[s], km[s])),
            colblk, colblk,
            pl.BlockSpec((None, A_HEADS, ATT_BLK, ATT_BLK),
                         lambda b, s, qm, km, kd, tf: (jnp.maximum(kd[s] - 1, 0), 0, 0, 0)),
        ],
        out_specs=pl.BlockSpec((None, ATT_BLK, A_W), lambda b, s, qm, km, kd, tf: (b, qm[s], 0)),
        scratch_shapes=[
            pltpu.VMEM((A_HEADS, ATT_BLK, LANES), BF16),
            pltpu.VMEM((A_HEADS, ATT_BLK, LANES), F32),
            pltpu.VMEM((A_HEADS, ATT_BLK, 2 * LANES), F32),
            pltpu.VMEM((ATT_BLK, ATT_BLK), F32),
            pltpu.VMEM((ATT_BLK, 1), F32),
        ],
    )
    return pl.pallas_call(
        functools.partial(_sparse_attn_kernel, nq=nq),
        grid_spec=grid_spec,
        out_shape=jax.ShapeDtypeStruct((bsz, tp, A_W), BF16),
        compiler_params=_cparams(("arbitrary", "arbitrary")),
        name="sparse_attn",
    )(qmap, kmap, kind, tieflag, p16, p16, p16, keys, tau, ntake, bias_tiles)


def _diff_attn_kernel(qmap, kmap, kind, lam_ref, q_ref, k_ref, v_ref, bias_ref, g_ref,
                      o_ref, qm_ref, m_ref, acc_ref, *, out_scale):
    s = pl.program_id(1)
    qb = qmap[s]
    kb = kmap[s]
    n_slots = 2 * B_HEADS

    @pl.when(kb == 0)
    def _():
        _lane_masked_queries(q_ref, qm_ref, n_slots, B_HALF, B_HALF ** -0.5 * LOG2E)
        _flash_init(m_ref, acc_ref)

    def attend(near_diagonal):
        for g in range(B_HEADS // 2):
            k_grp = k_ref[:, g * LANES:(g + 1) * LANES]
            v_ones = _with_ones(v_ref[:, g * LANES:(g + 1) * LANES])
            for slot in range(4 * g, 4 * g + 4):
                logit2 = _dot_nt(qm_ref[slot], k_grp)
                if near_diagonal:
                    logit2 = logit2 + bias_ref[slot // 2]
                _flash_update(slot, logit2, v_ones, m_ref, acc_ref)

    pl.when(kind[s] == 0)(lambda: attend(False))
    pl.when(kind[s] != 0)(lambda: attend(True))

    @pl.when(kb == qb)
    def _():
        lam = lam_ref[0]
        lane = lax.broadcasted_iota(jnp.int32, (ATT_BLK, LANES), 1)
        low = lane < B_HEAD_DIM
        for g in range(B_HEADS // 2):
            heads = []
            for h in (2 * g, 2 * g + 1):
                heads.append(_flash_result(2 * h, acc_ref) - lam * _flash_result(2 * h + 1, acc_ref))
            x = jnp.where(low, heads[0], heads[1])
            sq = x * x
            ss_lo = jnp.sum(jnp.where(low, sq, 0.0), axis=1, keepdims=True)
            ss_hi = jnp.sum(jnp.where(low, 0.0, sq), axis=1, keepdims=True)
            ms = jnp.where(low, ss_lo, ss_hi) * (1.0 / B_HEAD_DIM)
            y = (x * lax.rsqrt(ms + EPS)) * g_ref[...]
            o_ref[:, g * LANES:(g + 1) * LANES] = (y * out_scale).astype(o_ref.dtype)


def _diff_attn(p16, lam, bias_tiles, g_pair, out_scale):
    bsz, tp, _ = p16.shape
    nq = tp // ATT_BLK
    qmap, kmap, kind = _triangle_steps(nq)
    base = 4 * A_W // B_W
    qblk = pl.BlockSpec((None, ATT_BLK, B_W), lambda b, s, qm, km, kd, lm: (b, qm[s], base))
    kblk = lambda c: pl.BlockSpec((None, ATT_BLK, B_W), lambda b, s, qm, km, kd, lm: (b, km[s], base + c))
    n_slots = 2 * B_HEADS
    grid_spec = pltpu.PrefetchScalarGridSpec(
        num_scalar_prefetch=4,
        grid=(bsz, int(qmap.shape[0])),
        in_specs=[
            qblk, kblk(1), kblk(2),
            pl.BlockSpec((None, B_HEADS, ATT_BLK, ATT_BLK),
                         lambda b, s, qm, km, kd, lm: (jnp.maximum(kd[s] - 1, 0), 0, 0, 0)),
            pl.BlockSpec((1, LANES), lambda b, s, qm, km, kd, lm: (0, 0)),
        ],
        out_specs=pl.BlockSpec((None, ATT_BLK, B_W), lambda b, s, qm, km, kd, lm: (b, qm[s], 0)),
        scratch_shapes=[
            pltpu.VMEM((n_slots, ATT_BLK, LANES), BF16),
            pltpu.VMEM((n_slots, ATT_BLK, LANES), F32),
            pltpu.VMEM((n_slots, ATT_BLK, 2 * LANES), F32),
        ],
    )
    return pl.pallas_call(
        functools.partial(_diff_attn_kernel, out_scale=out_scale),
        grid_spec=grid_spec,
        out_shape=jax.ShapeDtypeStruct((bsz, tp, B_W), BF16),
        compiler_params=_cparams(("arbitrary", "arbitrary")),
        name="diff_attn",
    )(qmap, kmap, kind, lam, p16, p16, p16, bias_tiles, g_pair)


def _retention_kernel(q_ref, k_ref, v_ref, gate_ref, cos_ref, sin_ref, dmat_ref, qdec_ref, kdec_ref, sdec_ref,
                      g_ref, o_ref, state_ref):
    c = pl.program_id(1)

    @pl.when(c == 0)
    def _():
        state_ref[...] = jnp.zeros(state_ref.shape, F32)

    lane = lax.broadcasted_iota(jnp.int32, (RET_BLK, C_W), 1)
    first_half = (lane % C_HEAD_DIM) < (C_HEAD_DIM // 2)

    def rope(x):
        swapped = jnp.where(first_half,
                            pltpu.roll(x, C_W - C_HEAD_DIM // 2, axis=1),
                            pltpu.roll(x, C_HEAD_DIM // 2, axis=1))
        return x * cos_ref[...] + swapped * sin_ref[...]

    q = rope(q_ref[...])
    k = rope(k_ref[...]) * (C_HEAD_DIM ** -0.5)
    q16 = q.astype(BF16)
    k16 = k.astype(BF16)
    v16 = v_ref[...].astype(BF16)

    state = state_ref[...]
    out = _dot(q16, state.astype(BF16)) * qdec_ref[...]
    for h in range(C_HEADS):
        in_head = (lane >= h * C_HEAD_DIM) & (lane < (h + 1) * C_HEAD_DIM)
        qh = jnp.where(in_head, q, 0.0).astype(BF16)
        inner = _dot_nt(qh, k16) * dmat_ref[h]
        out = out + jnp.where(in_head, _dot(inner.astype(BF16), v16), 0.0)

    kd = (k * kdec_ref[...]).astype(BF16)
    update = _dot(kd.T, v16)
    r = lax.broadcasted_iota(jnp.int32, (C_W, C_W), 0) // C_HEAD_DIM
    cc = lax.broadcasted_iota(jnp.int32, (C_W, C_W), 1) // C_HEAD_DIM
    state_ref[...] = state * sdec_ref[...] + jnp.where(r == cc, update, 0.0)

    sq = out * out
    ms = jnp.zeros_like(out)
    for h in range(C_HEADS):
        in_head = (lane >= h * C_HEAD_DIM) & (lane < (h + 1) * C_HEAD_DIM)
        ss = jnp.sum(jnp.where(in_head, sq, 0.0), axis=1, keepdims=True)
        ms = jnp.where(in_head, ss * (1.0 / C_HEAD_DIM), ms)
    normed = (out * lax.rsqrt(ms + EPS)) * g_ref[...]
    gate = gate_ref[...]
    o_ref[...] = ((gate * jax.nn.sigmoid(gate)) * normed).astype(o_ref.dtype)


def _retention(p32, cos, sin, dmat, qdec, kdec, sdec, g):
    bsz, tp, _ = p32.shape
    blk = lambda c: pl.BlockSpec((None, RET_BLK, C_W), lambda b, i: (b, i, c))
    const = lambda shape: pl.BlockSpec(shape, lambda b, i: (0,) * len(shape))
    return pl.pallas_call(
        _retention_kernel,
        grid=(bsz, tp // RET_BLK),
        in_specs=[
            blk(0), blk(1), blk(2), blk(3),
            pl.BlockSpec((RET_BLK, C_W), lambda b, i: (i, 0)),
            pl.BlockSpec((RET_BLK, C_W), lambda b, i: (i, 0)),
            const((C_HEADS, RET_BLK, RET_BLK)),
            const((RET_BLK, C_W)), const((RET_BLK, C_W)), const((1, C_W)), const((1, C_W)),
        ],
        out_specs=pl.BlockSpec((None, RET_BLK, C_W), lambda b, i: (b, i, 0)),
        out_shape=jax.ShapeDtypeStruct((bsz, tp, C_W), BF16),
        scratch_shapes=[pltpu.VMEM((C_W, C_W), F32)],
        compiler_params=_cparams(("arbitrary", "arbitrary")),
        name="retention",
    )(p32, p32, p32, p32, cos, sin, dmat, qdec, kdec, sdec, g)


def _mix_mlp_kernel(h_ref, a_ref, b_ref, c_ref, wo_ref, g_ref, w1_ref, w2_ref, gf_ref, o_ref, u_ref, acc_ref,
                    *, final_norm):
    j = pl.program_id(1)

    @pl.when(j == 0)
    def _():
        mixed = (_dot(a_ref[...], wo_ref[0:A_W, :])
                 + _dot(b_ref[...], wo_ref[A_W:A_W + B_W, :])
                 + _dot(c_ref[...], wo_ref[A_W + B_W:, :]))
        h1 = h_ref[...] + mixed
        acc_ref[...] = h1
        ms = jnp.mean(h1 * h1, axis=-1, keepdims=True)
        u_ref[...] = ((h1 * lax.rsqrt(ms + EPS)) * g_ref[...]).astype(BF16)

    t = jnp.maximum(_dot(u_ref[...], w1_ref[...]), 0.0)
    acc_ref[...] += _dot((t * t).astype(BF16), w2_ref[...])

    @pl.when(j == pl.num_programs(1) - 1)
    def _():
        y = acc_ref[...]
        if final_norm:
            ms = jnp.mean(y * y, axis=-1, keepdims=True)
            y = (y * lax.rsqrt(ms + EPS)) * gf_ref[...]
        o_ref[...] = y


def _mix_mlp(h2d, a, b, c, wo, g, w1, w2, gf, final_norm):
    rows = h2d.shape[0]
    row_blk = lambda w: pl.BlockSpec((ROW_TILE, w), lambda i, j: (i, 0))
    vec = pl.BlockSpec((1, D_MODEL), lambda i, j: (0, 0))
    return pl.pallas_call(
        functools.partial(_mix_mlp_kernel, final_norm=final_norm),
        grid=(rows // ROW_TILE, D_FF // FF_CHUNK),
        in_specs=[
            row_blk(D_MODEL), row_blk(A_W), row_blk(B_W), row_blk(C_W),
            pl.BlockSpec((D_MODEL, D_MODEL), lambda i, j: (0, 0)),
            vec,
            pl.BlockSpec((D_MODEL, FF_CHUNK), lambda i, j: (0, j)),
            pl.BlockSpec((FF_CHUNK, D_MODEL), lambda i, j: (j, 0)),
            vec,
        ],
        out_specs=row_blk(D_MODEL),
        out_shape=jax.ShapeDtypeStruct((rows, D_MODEL), F32),
        scratch_shapes=[pltpu.VMEM((ROW_TILE, D_MODEL), BF16), pltpu.VMEM((ROW_TILE, D_MODEL), F32)],
        compiler_params=_cparams(("arbitrary", "arbitrary")),
        name="mix_mlp",
    )(h2d, a, b, c, wo, g, w1, w2, gf)


def _t5_bucket(dist):
    n = jnp.maximum(dist, 0)
    max_exact = REL_BUCKETS // 2
    nf = jnp.maximum(n, 1).astype(F32)
    large = max_exact + (jnp.log(nf / max_exact) / math.log(REL_MAX_DIST / max_exact)
                         * (REL_BUCKETS - max_exact)).astype(jnp.int32)
    large = jnp.minimum(large, REL_BUCKETS - 1)
    return jnp.where(n < max_exact, n, large)


def _far_bucket_from():
    max_exact = REL_BUCKETS // 2
    n = np.arange(1, 4 * REL_MAX_DIST, dtype=np.float32)
    large = max_exact + (np.log(n / max_exact) / math.log(REL_MAX_DIST / max_exact)
                         * (REL_BUCKETS - max_exact)).astype(np.int32)
    bucket = np.where(n < max_exact, n, np.minimum(large, REL_BUCKETS - 1))
    return int(np.nonzero(bucket < REL_BUCKETS - 1)[0].max()) + 2


_FAR_BUCKET_FROM = _far_bucket_from()


def _bias_tiles(table):
    i = jnp.arange(ATT_BLK)[:, None]
    j = jnp.arange(ATT_BLK)[None, :]
    assert _FAR_BUCKET_FROM <= ATT_BLK + 1
    far = table[REL_BUCKETS - 1].astype(F32)
    table2 = (table.astype(F32) - far[None, :]).T * LOG2E
    tiles = []
    for offset in (ATT_BLK, 0):
        dist = i - j + offset
        bucket = _t5_bucket(dist)
        t = jnp.zeros((table2.shape[0], ATT_BLK, ATT_BLK), F32)
        for b in range(REL_BUCKETS):
            t = jnp.where(bucket[None] == b, table2[:, b, None, None], t)
        tiles.append(jnp.where(dist >= 0, t, NEG))
    return jnp.stack(tiles)


def _retention_constants(tp):
    nh, dk = C_HEADS, C_HEAD_DIM
    log_gamma = jnp.log1p(-jnp.exp2(-5.0 - jnp.arange(nh, dtype=F32)))
    i = jnp.arange(RET_BLK, dtype=F32)
    diff = i[:, None] - i[None, :]
    dmat = jnp.where(diff >= 0, jnp.exp(log_gamma[:, None, None] * jnp.maximum(diff, 0.0)), 0.0)
    per_lane = lambda a: jnp.repeat(a, dk, axis=-1)
    qdec = per_lane(jnp.exp(log_gamma[None, :] * (i[:, None] + 1.0)))
    kdec = per_lane(jnp.exp(log_gamma[None, :] * (RET_BLK - 1.0 - i[:, None])))
    sdec = per_lane(jnp.exp(log_gamma * RET_BLK)[None, :])
    inv = ROPE_BASE ** (-jnp.arange(0, dk, 2, dtype=F32) / dk)
    ang = jnp.arange(tp, dtype=F32)[:, None] * inv[None, :]
    cos, sin = jnp.cos(ang), jnp.sin(ang)
    cos_full = jnp.tile(jnp.concatenate([cos, cos], axis=-1), (1, nh))
    sin_full = jnp.tile(jnp.concatenate([-sin, sin], axis=-1), (1, nh))
    return cos_full, sin_full, dmat, qdec, kdec, sdec


def _split_w_in(w):
    parts, acc = [], 0
    for size in IN_SIZES:
        parts.append(w[:, acc:acc + size])
        acc += size
    return parts


def kernel(x, meta, rel_bias, w_in, norm_mix, diff_lambda, diff_norm, ret_norm, w_out, norm_ff, w_ff1, w_ff2,
           final_norm):
    bsz, s_len, _ = x.shape
    depth = w_in.shape[0]
    t_len = s_len + N_META
    tp = -(-t_len // ATT_BLK) * ATT_BLK
    topk = min(TOPK_MAX, s_len // 4)
    nq = tp // ATT_BLK

    h = jnp.concatenate([
        jnp.broadcast_to(meta.astype(x.dtype)[None], (bsz, N_META, D_MODEL)), x,
        jnp.zeros((bsz, tp - t_len, D_MODEL), x.dtype)], axis=1).reshape(bsz * tp, D_MODEL)

    bias_a = _bias_tiles(rel_bias[:, :A_HEADS])
    bias_b = _bias_tiles(rel_bias[:, A_HEADS:])
    cos, sin, dmat, qdec, kdec, sdec = _retention_constants(tp)
    row = lambda v: v.reshape(1, -1).astype(F32)

    for l in range(depth):
        qa, ka, va, qi, ki, wi, qb, kb, vb, qc, kc, vc, gc = _split_w_in(w_in[l])
        w16 = jnp.concatenate([qa, ka, va, qi, qb, kb, vb, ki, ki], axis=1).astype(BF16)
        w32 = jnp.concatenate([qc, kc, vc, gc, wi, jnp.zeros((D_MODEL, LANES - IDX_HEADS), F32)],
                              axis=1).astype(BF16)
        p16, p32 = _inproj(h, row(norm_mix[l]), w16, w32)
        p16 = p16.reshape(bsz, tp, P16_COLS)
        p32 = p32.reshape(bsz, tp, P32_COLS)

        keys, tau, ntake, tie = _indexer(p16, p32, topk)
        tieflag = jnp.max(tie.reshape(bsz * nq, ATT_BLK), axis=1)
        a_out = _sparse_attn(p16, keys, tau, ntake, tieflag, bias_a)

        lambda_init = 0.8 - 0.6 * math.exp(-0.3 * l)
        lp = diff_lambda[l].astype(F32)
        lam = jnp.exp(jnp.sum(lp[0] * lp[1])) - jnp.exp(jnp.sum(lp[2] * lp[3])) + lambda_init
        g_pair = row(jnp.concatenate([diff_norm[l], diff_norm[l]]))
        b_out = _diff_attn(p16, lam.reshape(1), bias_b, g_pair, 1.0 - lambda_init)

        c_out = _retention(p32, cos, sin, dmat, qdec, kdec, sdec, row(ret_norm[l]))

        last = l == depth - 1
        h = _mix_mlp(h, a_out.reshape(bsz * tp, A_W), b_out.reshape(bsz * tp, B_W), c_out.reshape(bsz * tp, C_W),
                     w_out[l].astype(BF16), row(norm_ff[l]), w_ff1[l].astype(BF16), w_ff2[l].astype(BF16),
                     row(final_norm), final_norm=last)

    return h.reshape(bsz, tp, D_MODEL)[:, N_META:t_len]
```

```python
import functools
import math

import numpy as np
import jax
import jax.numpy as jnp
from jax import lax
from jax.experimental import pallas as pl
from jax.experimental.pallas import tpu as pltpu

D_MODEL = 1024
N_META = 16
A_HEADS = 8
A_HEAD_DIM = 64
A_W = A_HEADS * A_HEAD_DIM
IDX_HEADS = 8
IDX_DIM = 64
TOPK_MAX = 256
B_HEADS = 4
B_HEAD_DIM = 64
B_HALF = 32
B_W = B_HEADS * B_HEAD_DIM
C_HEADS = 4
C_HEAD_DIM = 64
C_W = C_HEADS * C_HEAD_DIM
ROPE_BASE = 10000.0
REL_BUCKETS = 32
REL_MAX_DIST = 128
D_FF = 4 * D_MODEL
EPS = 1e-6
IN_SIZES = (A_W, A_W, A_W, IDX_HEADS * IDX_DIM, IDX_DIM, IDX_HEADS, B_W, B_W, B_W, C_W, C_W, C_W, C_W)

LANES = 128
ATT_BLK = 512
IDX_ROWS = 256
IDX_CHUNK = 512
COUNT_ROWS = 128
BISECT_FIXED_STEPS = 16
SMALLEST_NORMAL = float(np.finfo(np.float32).tiny)
RET_BLK = 256
ROW_TILE = 512
FF_CHUNK = 1024
VMEM_LIMIT_BYTES = 56 * 1024 * 1024

NEG = -1e30
LOG2E = math.log2(math.e)

P16_COLS = 4 * A_W + 3 * B_W + 2 * IDX_DIM
P32_COLS = 4 * C_W + LANES

BF16 = jnp.bfloat16
F32 = jnp.float32


def _cparams(sem):
    return pltpu.CompilerParams(dimension_semantics=sem, vmem_limit_bytes=VMEM_LIMIT_BYTES)


def _dot_nt(a, b):
    return lax.dot_general(a, b, (((1,), (1,)), ((), ())), preferred_element_type=F32)


def _dot(a, b):
    return jnp.dot(a, b, preferred_element_type=F32)


def _inproj_kernel(h_ref, g_ref, w16_ref, w32_ref, o16_ref, o32_ref):
    x = h_ref[...]
    ms = jnp.mean(x * x, axis=-1, keepdims=True)
    u = ((x * lax.rsqrt(ms + EPS)) * g_ref[...]).astype(BF16)
    o16_ref[...] = _dot(u, w16_ref[...]).astype(BF16)
    o32_ref[...] = _dot(u, w32_ref[...])


def _inproj(h2d, g, w16, w32):
    rows = h2d.shape[0]
    return pl.pallas_call(
        _inproj_kernel,
        grid=(rows // ROW_TILE,),
        in_specs=[
            pl.BlockSpec((ROW_TILE, D_MODEL), lambda i: (i, 0)),
            pl.BlockSpec((1, D_MODEL), lambda i: (0, 0)),
            pl.BlockSpec((D_MODEL, P16_COLS), lambda i: (0, 0)),
            pl.BlockSpec((D_MODEL, P32_COLS), lambda i: (0, 0)),
        ],
        out_specs=[
            pl.BlockSpec((ROW_TILE, P16_COLS), lambda i: (i, 0)),
            pl.BlockSpec((ROW_TILE, P32_COLS), lambda i: (i, 0)),
        ],
        out_shape=[
            jax.ShapeDtypeStruct((rows, P16_COLS), BF16),
            jax.ShapeDtypeStruct((rows, P32_COLS), F32),
        ],
        compiler_params=_cparams(("arbitrary",)),
        name="inproj",
    )(h2d, g, w16, w32)


def _indexer_kernel(qi_ref, ki_ref, wi_ref, sc_ref, tau_ref, ntake_ref, tie_ref, qm_ref, *, topk, n_chunks):
    i = pl.program_id(1)
    row0 = i * IDX_ROWS
    n_live = (row0 + IDX_ROWS - 1) // IDX_CHUNK + 1
    n_tiles = IDX_CHUNK // LANES
    k_f = float(topk)

    lane = lax.broadcasted_iota(jnp.int32, (IDX_ROWS, LANES), 1)
    for h in range(IDX_HEADS):
        pair = qi_ref[:, (h // 2) * LANES:(h // 2 + 1) * LANES]
        keep = (lane >= IDX_DIM) if h % 2 else (lane < IDX_DIM)
        qm_ref[h] = jnp.where(keep, pair, jnp.zeros_like(pair))

    w = wi_ref[...] * (IDX_HEADS ** -0.5 * IDX_DIM ** -0.5)
    wcol = [w[:, h:h + 1] for h in range(IDX_HEADS)]
    row = row0 + lax.broadcasted_iota(jnp.int32, (IDX_ROWS, 1), 0)
    col_in_chunk = lax.broadcasted_iota(jnp.int32, (IDX_ROWS, IDX_CHUNK), 1)

    def score_chunk(c, carry):
        lo, hi = carry
        start = pl.multiple_of(c * IDX_CHUNK, IDX_CHUNK)
        kc = ki_ref[pl.ds(start, IDX_CHUNK), :]
        acc = jnp.zeros((IDX_ROWS, IDX_CHUNK), F32)
        for h in range(IDX_HEADS):
            acc = acc + wcol[h] * jnp.maximum(_dot_nt(qm_ref[h], kc), 0.0)
        causal = col_in_chunk + start <= row
        sc = jnp.where(causal, acc, -jnp.inf)
        sc_ref[:, pl.ds(start, IDX_CHUNK)] = sc
        for_min = jnp.where(causal, acc, jnp.inf)
        for t in range(n_tiles):
            lo = jnp.minimum(lo, for_min[:, t * LANES:(t + 1) * LANES])
            hi = jnp.maximum(hi, sc[:, t * LANES:(t + 1) * LANES])
        return lo, hi

    lo_l, hi_l = lax.fori_loop(0, n_live, score_chunk, (jnp.full((IDX_ROWS, LANES), jnp.inf, F32),
                                                        jnp.full((IDX_ROWS, LANES), -jnp.inf, F32)))
    across = lambda col: jnp.broadcast_to(col, (IDX_ROWS, LANES))
    lo0 = across(jnp.min(lo_l, axis=1, keepdims=True))
    hi0 = across(jnp.max(hi_l, axis=1, keepdims=True))

    def fill_chunk(c, carry):
        start = pl.multiple_of(c * IDX_CHUNK, IDX_CHUNK)
        sc_ref[:, pl.ds(start, IDX_CHUNK)] = jnp.full((IDX_ROWS, IDX_CHUNK), -jnp.inf, F32)
        return carry

    lax.fori_loop(n_live, n_chunks, fill_chunk, 0)

    def count(thr, strict=False):
        parts = []
        for r0 in range(0, IDX_ROWS, COUNT_ROWS):
            thr_r = thr[r0:r0 + COUNT_ROWS]

            def body(c, acc, r0=r0, thr_r=thr_r):
                start = pl.multiple_of(c * IDX_CHUNK, IDX_CHUNK)
                sc = sc_ref[r0:r0 + COUNT_ROWS, pl.ds(start, IDX_CHUNK)]
                for t in range(n_tiles):
                    tile = sc[:, t * LANES:(t + 1) * LANES]
                    acc = jnp.where((tile > thr_r) if strict else (tile >= thr_r), acc + 1.0, acc)
                return acc

            acc = lax.fori_loop(0, n_live, body, jnp.zeros((COUNT_ROWS, LANES), F32))
            parts.append(jnp.broadcast_to(jnp.sum(acc, axis=1, keepdims=True), (COUNT_ROWS, LANES)))
        return jnp.concatenate(parts, axis=0)

    n_valid = (row0 + 1 + lax.broadcasted_iota(jnp.int32, (IDX_ROWS, LANES), 0)).astype(F32)
    keeps_all = n_valid <= k_f
    tied_at_max = count(hi0) >= k_f
    tau0 = jnp.where(keeps_all, jnp.float32(jnp.finfo(jnp.float32).min), hi0)
    active0 = jnp.where(keeps_all | tied_at_max, 0.0, 1.0)

    def unfinished(state):
        return jnp.sum(state[3]) > 0.0

    def bisect(state):
        lo, hi, tau, active = state
        mid = 0.5 * lo + 0.5 * hi
        mid = jnp.where((lo < 0.0) & (hi > 0.0), 0.0, mid)
        mid = jnp.where((lo == 0.0) & (hi > SMALLEST_NORMAL), SMALLEST_NORMAL, mid)
        collapsed = (mid <= lo) | (mid >= hi)
        cnt = count(mid)
        hit = cnt == k_f
        is_active = active > 0.0
        tau = jnp.where(is_active, jnp.where(collapsed, lo, jnp.where(hit, mid, tau)), tau)
        go_on = is_active & jnp.logical_not(collapsed | hit)
        lo = jnp.where(go_on & (cnt > k_f), mid, lo)
        hi = jnp.where(go_on & (cnt < k_f), mid, hi)
        return lo, hi, tau, jnp.where(go_on, 1.0, 0.0)

    state = lax.fori_loop(0, BISECT_FIXED_STEPS, lambda _, st: bisect(st), (lo0, hi0, tau0, active0))
    _, _, tau, _ = lax.while_loop(unfinished, bisect, state)
    n_gt = count(tau, strict=True)
    n_ge = count(tau)
    tau_ref[...] = tau[:, :1]
    ntake_ref[...] = k_f - n_gt[:, :1]
    tie_ref[...] = (n_ge[:, :1] > k_f).astype(jnp.int32)


def _indexer(p16, p32, topk):
    bsz, tp, _ = p16.shape
    n_chunks = tp // IDX_CHUNK
    kern = functools.partial(_indexer_kernel, topk=topk, n_chunks=n_chunks)
    col = lambda n: pl.BlockSpec((None, IDX_ROWS, 1), lambda b, i: (b, i, 0))
    return pl.pallas_call(
        kern,
        grid=(bsz, tp // IDX_ROWS),
        in_specs=[
            pl.BlockSpec((None, IDX_ROWS, A_W), lambda b, i: (b, i, 3)),
            pl.BlockSpec((None, tp, LANES), lambda b, i: (b, 0, (P16_COLS - LANES) // LANES)),
            pl.BlockSpec((None, IDX_ROWS, LANES), lambda b, i: (b, i, 4 * C_W // LANES)),
        ],
        out_specs=[
            pl.BlockSpec((None, IDX_ROWS, tp), lambda b, i: (b, i, 0)),
            col(0), col(1), col(2),
        ],
        out_shape=[
            jax.ShapeDtypeStruct((bsz, tp, tp), F32),
            jax.ShapeDtypeStruct((bsz, tp, 1), F32),
            jax.ShapeDtypeStruct((bsz, tp, 1), F32),
            jax.ShapeDtypeStruct((bsz, tp, 1), jnp.int32),
        ],
        scratch_shapes=[pltpu.VMEM((IDX_HEADS, IDX_ROWS, LANES), BF16)],
        compiler_params=_cparams(("arbitrary", "arbitrary")),
        name="indexer",
    )(p16, p16, p32)


def _lane_masked_queries(q_ref, qm_ref, n_slots, width, scale2):
    per_group = LANES // width
    lane = lax.broadcasted_iota(jnp.int32, (ATT_BLK, LANES), 1)
    for s in range(n_slots):
        g, r = divmod(s, per_group)
        grp = q_ref[:, g * LANES:(g + 1) * LANES].astype(F32) * scale2
        keep = (lane >= r * width) & (lane < (r + 1) * width)
        qm_ref[s] = jnp.where(keep, grp, 0.0).astype(BF16)


def _flash_update(s, logit2, v_ones, m_ref, acc_ref):
    m_prev = m_ref[s]
    m_new = jnp.maximum(m_prev, jnp.max(logit2, axis=1, keepdims=True))
    alpha = jnp.exp2(m_prev - m_new)
    p = jnp.concatenate([jnp.exp2(logit2[:, t * LANES:(t + 1) * LANES] - m_new)
                         for t in range(logit2.shape[1] // LANES)], axis=1).astype(BF16)
    acc_ref[s] = jnp.concatenate([alpha, alpha], axis=1) * acc_ref[s] + _dot(p, v_ones)
    m_ref[s] = m_new


def _with_ones(v_grp):
    return jnp.concatenate([v_grp, jnp.ones(v_grp.shape, v_grp.dtype)], axis=1)


def _flash_init(m_ref, acc_ref):
    m_ref[...] = jnp.full(m_ref.shape, NEG, F32)
    acc_ref[...] = jnp.zeros(acc_ref.shape, F32)


def _flash_result(s, acc_ref):
    acc = acc_ref[s]
    return acc[:, :LANES] / acc[:, LANES:]


def _triangle_steps(nq):
    qs, ks, kinds = [], [], []
    for q in range(nq):
        for k in range(q + 1):
            qs.append(q)
            ks.append(k)
            kinds.append(2 if k == q else (1 if k == q - 1 else 0))
    return (jnp.asarray(qs, jnp.int32), jnp.asarray(ks, jnp.int32), jnp.asarray(kinds, jnp.int32))


def _sparse_attn_kernel(qmap, kmap, kind, tieflag, q_ref, k_ref, v_ref, sc_ref, tau_ref, ntake_ref, bias_ref,
                        o_ref, qm_ref, m_ref, acc_ref, mask_ref, carry_ref, *, nq):
    b = pl.program_id(0)
    s = pl.program_id(1)
    qb = qmap[s]
    kb = kmap[s]

    @pl.when(kb == 0)
    def _():
        _lane_masked_queries(q_ref, qm_ref, A_HEADS, A_HEAD_DIM, A_HEAD_DIM ** -0.5 * LOG2E)
        _flash_init(m_ref, acc_ref)
        carry_ref[...] = jnp.zeros(carry_ref.shape, F32)

    has_ties = tieflag[b * nq + qb]

    @pl.when(has_ties == 0)
    def _():
        mask_ref[...] = jnp.where(sc_ref[...] >= tau_ref[...], 0.0, NEG)

    @pl.when(has_ties != 0)
    def _():
        sc = sc_ref[...]
        tau = tau_ref[...]
        eq = sc == tau
        eq_b = jnp.where(eq, 1.0, 0.0).astype(BF16)
        r = lax.broadcasted_iota(jnp.int32, (ATT_BLK, ATT_BLK), 0)
        c = lax.broadcasted_iota(jnp.int32, (ATT_BLK, ATT_BLK), 1)
        before = jnp.where(r < c, 1.0, 0.0).astype(BF16)
        seen = carry_ref[...] + _dot(eq_b, before)
        keep = (sc > tau) | (eq & (seen < ntake_ref[...]))
        mask_ref[...] = jnp.where(keep, 0.0, NEG)
        carry_ref[...] = carry_ref[...] + jnp.sum(eq_b.astype(F32), axis=1, keepdims=True)

    def attend(near_diagonal):
        for g in range(A_HEADS // 2):
            k_grp = k_ref[:, g * LANES:(g + 1) * LANES]
            v_ones = _with_ones(v_ref[:, g * LANES:(g + 1) * LANES])
            for h in (2 * g, 2 * g + 1):
                logit2 = _dot_nt(qm_ref[h], k_grp) + mask_ref[...]
                if near_diagonal:
                    logit2 = logit2 + bias_ref[h]
                _flash_update(h, logit2, v_ones, m_ref, acc_ref)

    pl.when(kind[s] == 0)(lambda: attend(False))
    pl.when(kind[s] != 0)(lambda: attend(True))

    @pl.when(kb == qb)
    def _():
        lane = lax.broadcasted_iota(jnp.int32, (ATT_BLK, LANES), 1)
        for g in range(A_HEADS // 2):
            out = jnp.where(lane < A_HEAD_DIM, _flash_result(2 * g, acc_ref), _flash_result(2 * g + 1, acc_ref))
            o_ref[:, g * LANES:(g + 1) * LANES] = out.astype(o_ref.dtype)


def _sparse_attn(p16, keys, tau, ntake, tieflag, bias_tiles):
    bsz, tp, _ = p16.shape
    nq = tp // ATT_BLK
    qmap, kmap, kind = _triangle_steps(nq)
    qblk = lambda c: pl.BlockSpec((None, ATT_BLK, A_W), lambda b, s, qm, km, kd, tf: (b, qm[s], c))
    kblk = lambda c: pl.BlockSpec((None, ATT_BLK, A_W), lambda b, s, qm, km, kd, tf: (b, km[s], c))
    colblk = pl.BlockSpec((None, ATT_BLK, 1), lambda b, s, qm, km, kd, tf: (b, qm[s], 0))
    grid_spec = pltpu.PrefetchScalarGridSpec(
        num_scalar_prefetch=4,
        grid=(bsz, int(qmap.shape[0])),
        in_specs=[
            qblk(0), kblk(1), kblk(2),
            pl.BlockSpec((None, ATT_BLK, ATT_BLK), lambda b, s, qm, km, kd, tf: (b, qm[s], km[s])),
            colblk, colblk,
            pl.BlockSpec((None, A_HEADS, ATT_BLK, ATT_BLK),
                         lambda b, s, qm, km, kd, tf: (jnp.maximum(kd[s] - 1, 0), 0, 0, 0)),
        ],
        out_specs=pl.BlockSpec((None, ATT_BLK, A_W), lambda b, s, qm, km, kd, tf: (b, qm[s], 0)),
        scratch_shapes=[
            pltpu.VMEM((A_HEADS, ATT_BLK, LANES), BF16),
            pltpu.VMEM((A_HEADS, ATT_BLK, LANES), F32),
            pltpu.VMEM((A_HEADS, ATT_BLK, 2 * LANES), F32),
            pltpu.VMEM((ATT_BLK, ATT_BLK), F32),
            pltpu.VMEM((ATT_BLK, 1), F32),
        ],
    )
    return pl.pallas_call(
        functools.partial(_sparse_attn_kernel, nq=nq),
        grid_spec=grid_spec,
        out_shape=jax.ShapeDtypeStruct((bsz, tp, A_W), BF16),
        compiler_params=_cparams(("arbitrary", "arbitrary")),
        name="sparse_attn",
    )(qmap, kmap, kind, tieflag, p16, p16, p16, keys, tau, ntake, bias_tiles)


def _diff_attn_kernel(qmap, kmap, kind, lam_ref, q_ref, k_ref, v_ref, bias_ref, g_ref,
                      o_ref, qm_ref, m_ref, acc_ref, *, out_scale):
    s = pl.program_id(1)
    qb = qmap[s]
    kb = kmap[s]
    n_slots = 2 * B_HEADS

    @pl.when(kb == 0)
    def _():
        _lane_masked_queries(q_ref, qm_ref, n_slots, B_HALF, B_HALF ** -0.5 * LOG2E)
        _flash_init(m_ref, acc_ref)

    def attend(near_diagonal):
        for g in range(B_HEADS // 2):
            k_grp = k_ref[:, g * LANES:(g + 1) * LANES]
            v_ones = _with_ones(v_ref[:, g * LANES:(g + 1) * LANES])
            for slot in range(4 * g, 4 * g + 4):
                logit2 = _dot_nt(qm_ref[slot], k_grp)
                if near_diagonal:
                    logit2 = logit2 + bias_ref[slot // 2]
                _flash_update(slot, logit2, v_ones, m_ref, acc_ref)

    pl.when(kind[s] == 0)(lambda: attend(False))
    pl.when(kind[s] != 0)(lambda: attend(True))

    @pl.when(kb == qb)
    def _():
        lam = lam_ref[0]
        lane = lax.broadcasted_iota(jnp.int32, (ATT_BLK, LANES), 1)
        low = lane < B_HEAD_DIM
        for g in range(B_HEADS // 2):
            heads = []
            for h in (2 * g, 2 * g + 1):
                heads.append(_flash_result(2 * h, acc_ref) - lam * _flash_result(2 * h + 1, acc_ref))
            x = jnp.where(low, heads[0], heads[1])
            sq = x * x
            ss_lo = jnp.sum(jnp.where(low, sq, 0.0), axis=1, keepdims=True)
            ss_hi = jnp.sum(jnp.where(low, 0.0, sq), axis=1, keepdims=True)
            ms = jnp.where(low, ss_lo, ss_hi) * (1.0 / B_HEAD_DIM)
            y = (x * lax.rsqrt(ms + EPS)) * g_ref[...]
            o_ref[:, g * LANES:(g + 1) * LANES] = (y * out_scale).astype(o_ref.dtype)


def _diff_attn(p16, lam, bias_tiles, g_pair, out_scale):
    bsz, tp, _ = p16.shape
    nq = tp // ATT_BLK
    qmap, kmap, kind = _triangle_steps(nq)
    base = 4 * A_W // B_W
    qblk = pl.BlockSpec((None, ATT_BLK, B_W), lambda b, s, qm, km, kd, lm: (b, qm[s], base))
    kblk = lambda c: pl.BlockSpec((None, ATT_BLK, B_W), lambda b, s, qm, km, kd, lm: (b, km[s], base + c))
    n_slots = 2 * B_HEADS
    grid_spec = pltpu.PrefetchScalarGridSpec(
        num_scalar_prefetch=4,
        grid=(bsz, int(qmap.shape[0])),
        in_specs=[
            qblk, kblk(1), kblk(2),
            pl.BlockSpec((None, B_HEADS, ATT_BLK, ATT_BLK),
                         lambda b, s, qm, km, kd, lm: (jnp.maximum(kd[s] - 1, 0), 0, 0, 0)),
            pl.BlockSpec((1, LANES), lambda b, s, qm, km, kd, lm: (0, 0)),
        ],
        out_specs=pl.BlockSpec((None, ATT_BLK, B_W), lambda b, s, qm, km, kd, lm: (b, qm[s], 0)),
        scratch_shapes=[
            pltpu.VMEM((n_slots, ATT_BLK, LANES), BF16),
            pltpu.VMEM((n_slots, ATT_BLK, LANES), F32),
            pltpu.VMEM((n_slots, ATT_BLK, 2 * LANES), F32),
        ],
    )
    return pl.pallas_call(
        functools.partial(_diff_attn_kernel, out_scale=out_scale),
        grid_spec=grid_spec,
        out_shape=jax.ShapeDtypeStruct((bsz, tp, B_W), BF16),
        compiler_params=_cparams(("arbitrary", "arbitrary")),
        name="diff_attn",
    )(qmap, kmap, kind, lam, p16, p16, p16, bias_tiles, g_pair)


def _retention_kernel(q_ref, k_ref, v_ref, gate_ref, cos_ref, sin_ref, dmat_ref, qdec_ref, kdec_ref, sdec_ref,
                      g_ref, o_ref, state_ref):
    c = pl.program_id(1)

    @pl.when(c == 0)
    def _():
        state_ref[...] = jnp.zeros(state_ref.shape, F32)

    lane = lax.broadcasted_iota(jnp.int32, (RET_BLK, C_W), 1)
    first_half = (lane % C_HEAD_DIM) < (C_HEAD_DIM // 2)

    def rope(x):
        swapped = jnp.where(first_half,
                            pltpu.roll(x, C_W - C_HEAD_DIM // 2, axis=1),
                            pltpu.roll(x, C_HEAD_DIM // 2, axis=1))
        return x * cos_ref[...] + swapped * sin_ref[...]

    q = rope(q_ref[...])
    k = rope(k_ref[...]) * (C_HEAD_DIM ** -0.5)
    q16 = q.astype(BF16)
    k16 = k.astype(BF16)
    v16 = v_ref[...].astype(BF16)

    state = state_ref[...]
    out = _dot(q16, state.astype(BF16)) * qdec_ref[...]
    for h in range(C_HEADS):
        in_head = (lane >= h * C_HEAD_DIM) & (lane < (h + 1) * C_HEAD_DIM)
        qh = jnp.where(in_head, q, 0.0).astype(BF16)
        inner = _dot_nt(qh, k16) * dmat_ref[h]
        out = out + jnp.where(in_head, _dot(inner.astype(BF16), v16), 0.0)

    kd = (k * kdec_ref[...]).astype(BF16)
    update = _dot(kd.T, v16)
    r = lax.broadcasted_iota(jnp.int32, (C_W, C_W), 0) // C_HEAD_DIM
    cc = lax.broadcasted_iota(jnp.int32, (C_W, C_W), 1) // C_HEAD_DIM
    state_ref[...] = state * sdec_ref[...] + jnp.where(r == cc, update, 0.0)

    sq = out * out
    ms = jnp.zeros_like(out)
    for h in range(C_HEADS):
        in_head = (lane >= h * C_HEAD_DIM) & (lane < (h + 1) * C_HEAD_DIM)
        ss = jnp.sum(jnp.where(in_head, sq, 0.0), axis=1, keepdims=True)
        ms = jnp.where(in_head, ss * (1.0 / C_HEAD_DIM), ms)
    normed = (out * lax.rsqrt(ms + EPS)) * g_ref[...]
    gate = gate_ref[...]
    o_ref[...] = ((gate * jax.nn.sigmoid(gate)) * normed).astype(o_ref.dtype)


def _retention(p32, cos, sin, dmat, qdec, kdec, sdec, g):
    bsz, tp, _ = p32.shape
    blk = lambda c: pl.BlockSpec((None, RET_BLK, C_W), lambda b, i: (b, i, c))
    const = lambda shape: pl.BlockSpec(shape, lambda b, i: (0,) * len(shape))
    return pl.pallas_call(
        _retention_kernel,
        grid=(bsz, tp // RET_BLK),
        in_specs=[
            blk(0), blk(1), blk(2), blk(3),
            pl.BlockSpec((RET_BLK, C_W), lambda b, i: (i, 0)),
            pl.BlockSpec((RET_BLK, C_W), lambda b, i: (i, 0)),
            const((C_HEADS, RET_BLK, RET_BLK)),
            const((RET_BLK, C_W)), const((RET_BLK, C_W)), const((1, C_W)), const((1, C_W)),
        ],
        out_specs=pl.BlockSpec((None, RET_BLK, C_W), lambda b, i: (b, i, 0)),
        out_shape=jax.ShapeDtypeStruct((bsz, tp, C_W), BF16),
        scratch_shapes=[pltpu.VMEM((C_W, C_W), F32)],
        compiler_params=_cparams(("arbitrary", "arbitrary")),
        name="retention",
    )(p32, p32, p32, p32, cos, sin, dmat, qdec, kdec, sdec, g)


def _mix_mlp_kernel(h_ref, a_ref, b_ref, c_ref, wo_ref, g_ref, w1_ref, w2_ref, gf_ref, o_ref, u_ref, acc_ref,
                    *, final_norm):
    j = pl.program_id(1)

    @pl.when(j == 0)
    def _():
        mixed = (_dot(a_ref[...], wo_ref[0:A_W, :])
                 + _dot(b_ref[...], wo_ref[A_W:A_W + B_W, :])
                 + _dot(c_ref[...], wo_ref[A_W + B_W:, :]))
        h1 = h_ref[...] + mixed
        acc_ref[...] = h1
        ms = jnp.mean(h1 * h1, axis=-1, keepdims=True)
        u_ref[...] = ((h1 * lax.rsqrt(ms + EPS)) * g_ref[...]).astype(BF16)

    t = jnp.maximum(_dot(u_ref[...], w1_ref[...]), 0.0)
    acc_ref[...] += _dot((t * t).astype(BF16), w2_ref[...])

    @pl.when(j == pl.num_programs(1) - 1)
    def _():
        y = acc_ref[...]
        if final_norm:
            ms = jnp.mean(y * y, axis=-1, keepdims=True)
            y = (y * lax.rsqrt(ms + EPS)) * gf_ref[...]
        o_ref[...] = y


def _mix_mlp(h2d, a, b, c, wo, g, w1, w2, gf, final_norm):
    rows = h2d.shape[0]
    row_blk = lambda w: pl.BlockSpec((ROW_TILE, w), lambda i, j: (i, 0))
    vec = pl.BlockSpec((1, D_MODEL), lambda i, j: (0, 0))
    return pl.pallas_call(
        functools.partial(_mix_mlp_kernel, final_norm=final_norm),
        grid=(rows // ROW_TILE, D_FF // FF_CHUNK),
        in_specs=[
            row_blk(D_MODEL), row_blk(A_W), row_blk(B_W), row_blk(C_W),
            pl.BlockSpec((D_MODEL, D_MODEL), lambda i, j: (0, 0)),
            vec,
            pl.BlockSpec((D_MODEL, FF_CHUNK), lambda i, j: (0, j)),
            pl.BlockSpec((FF_CHUNK, D_MODEL), lambda i, j: (j, 0)),
            vec,
        ],
        out_specs=row_blk(D_MODEL),
        out_shape=jax.ShapeDtypeStruct((rows, D_MODEL), F32),
        scratch_shapes=[pltpu.VMEM((ROW_TILE, D_MODEL), BF16), pltpu.VMEM((ROW_TILE, D_MODEL), F32)],
        compiler_params=_cparams(("arbitrary", "arbitrary")),
        name="mix_mlp",
    )(h2d, a, b, c, wo, g, w1, w2, gf)


def _t5_bucket(dist):
    n = jnp.maximum(dist, 0)
    max_exact = REL_BUCKETS // 2
    nf = jnp.maximum(n, 1).astype(F32)
    large = max_exact + (jnp.log(nf / max_exact) / math.log(REL_MAX_DIST / max_exact)
                         * (REL_BUCKETS - max_exact)).astype(jnp.int32)
    large = jnp.minimum(large, REL_BUCKETS - 1)
    return jnp.where(n < max_exact, n, large)


def _far_bucket_from():
    max_exact = REL_BUCKETS // 2
    n = np.arange(1, 4 * REL_MAX_DIST, dtype=np.float32)
    large = max_exact + (np.log(n / max_exact) / math.log(REL_MAX_DIST / max_exact)
                         * (REL_BUCKETS - max_exact)).astype(np.int32)
    bucket = np.where(n < max_exact, n, np.minimum(large, REL_BUCKETS - 1))
    return int(np.nonzero(bucket < REL_BUCKETS - 1)[0].max()) + 2


_FAR_BUCKET_FROM = _far_bucket_from()


def _bias_tiles(table):
    i = jnp.arange(ATT_BLK)[:, None]
    j = jnp.arange(ATT_BLK)[None, :]
    assert _FAR_BUCKET_FROM <= ATT_BLK + 1
    far = table[REL_BUCKETS - 1].astype(F32)
    table2 = (table.astype(F32) - far[None, :]).T * LOG2E
    tiles = []
    for offset in (ATT_BLK, 0):
        dist = i - j + offset
        bucket = _t5_bucket(dist)
        t = jnp.zeros((table2.shape[0], ATT_BLK, ATT_BLK), F32)
        for b in range(REL_BUCKETS):
            t = jnp.where(bucket[None] == b, table2[:, b, None, None], t)
        tiles.append(jnp.where(dist >= 0, t, NEG))
    return jnp.stack(tiles)


def _retention_constants(tp):
    nh, dk = C_HEADS, C_HEAD_DIM
    log_gamma = jnp.log1p(-jnp.exp2(-5.0 - jnp.arange(nh, dtype=F32)))
    i = jnp.arange(RET_BLK, dtype=F32)
    diff = i[:, None] - i[None, :]
    dmat = jnp.where(diff >= 0, jnp.exp(log_gamma[:, None, None] * jnp.maximum(diff, 0.0)), 0.0)
    per_lane = lambda a: jnp.repeat(a, dk, axis=-1)
    qdec = per_lane(jnp.exp(log_gamma[None, :] * (i[:, None] + 1.0)))
    kdec = per_lane(jnp.exp(log_gamma[None, :] * (RET_BLK - 1.0 - i[:, None])))
    sdec = per_lane(jnp.exp(log_gamma * RET_BLK)[None, :])
    inv = ROPE_BASE ** (-jnp.arange(0, dk, 2, dtype=F32) / dk)
    ang = jnp.arange(tp, dtype=F32)[:, None] * inv[None, :]
    cos, sin = jnp.cos(ang), jnp.sin(ang)
    cos_full = jnp.tile(jnp.concatenate([cos, cos], axis=-1), (1, nh))
    sin_full = jnp.tile(jnp.concatenate([-sin, sin], axis=-1), (1, nh))
    return cos_full, sin_full, dmat, qdec, kdec, sdec


def _split_w_in(w):
    parts, acc = [], 0
    for size in IN_SIZES:
        parts.append(w[:, acc:acc + size])
        acc += size
    return parts


def kernel(x, meta, rel_bias, w_in, norm_mix, diff_lambda, diff_norm, ret_norm, w_out, norm_ff, w_ff1, w_ff2,
           final_norm):
    bsz, s_len, _ = x.shape
    depth = w_in.shape[0]
    t_len = s_len + N_META
    tp = -(-t_len // ATT_BLK) * ATT_BLK
    topk = min(TOPK_MAX, s_len // 4)
    nq = tp // ATT_BLK

    h = jnp.concatenate([
        jnp.broadcast_to(meta.astype(x.dtype)[None], (bsz, N_META, D_MODEL)), x,
        jnp.zeros((bsz, tp - t_len, D_MODEL), x.dtype)], axis=1).reshape(bsz * tp, D_MODEL)

    bias_a = _bias_tiles(rel_bias[:, :A_HEADS])
    bias_b = _bias_tiles(rel_bias[:, A_HEADS:])
    cos, sin, dmat, qdec, kdec, sdec = _retention_constants(tp)
    row = lambda v: v.reshape(1, -1).astype(F32)

    for l in range(depth):
        qa, ka, va, qi, ki, wi, qb, kb, vb, qc, kc, vc, gc = _split_w_in(w_in[l])
        w16 = jnp.concatenate([qa, ka, va, qi, qb, kb, vb, ki, ki], axis=1).astype(BF16)
        w32 = jnp.concatenate([qc, kc, vc, gc, wi, jnp.zeros((D_MODEL, LANES - IDX_HEADS), F32)],
                              axis=1).astype(BF16)
        p16, p32 = _inproj(h, row(norm_mix[l]), w16, w32)
        p16 = p16.reshape(bsz, tp, P16_COLS)
        p32 = p32.reshape(bsz, tp, P32_COLS)

        keys, tau, ntake, tie = _indexer(p16, p32, topk)
        tieflag = jnp.max(tie.reshape(bsz * nq, ATT_BLK), axis=1)
        a_out = _sparse_attn(p16, keys, tau, ntake, tieflag, bias_a)

        lambda_init = 0.8 - 0.6 * math.exp(-0.3 * l)
        lp = diff_lambda[l].astype(F32)
        lam = jnp.exp(jnp.sum(lp[0] * lp[1])) - jnp.exp(jnp.sum(lp[2] * lp[3])) + lambda_init
        g_pair = row(jnp.concatenate([diff_norm[l], diff_norm[l]]))
        b_out = _diff_attn(p16, lam.reshape(1), bias_b, g_pair, 1.0 - lambda_init)

        c_out = _retention(p32, cos, sin, dmat, qdec, kdec, sdec, row(ret_norm[l]))

        last = l == depth - 1
        h = _mix_mlp(h, a_out.reshape(bsz * tp, A_W), b_out.reshape(bsz * tp, B_W), c_out.reshape(bsz * tp, C_W),
                     w_out[l].astype(BF16), row(norm_ff[l]), w_ff1[l].astype(BF16), w_ff2[l].astype(BF16),
                     row(final_norm), final_norm=last)

    return h.reshape(bsz, tp, D_MODEL)[:, N_META:t_len]
```

```python
import functools
import math

import numpy as np
import jax
import jax.numpy as jnp
from jax import lax
from jax.experimental import pallas as pl
from jax.experimental.pallas import tpu as pltpu

D_MODEL = 1024
N_META = 16
A_HEADS = 8
A_HEAD_DIM = 64
A_W = A_HEADS * A_HEAD_DIM
IDX_HEADS = 8
IDX_DIM = 64
TOPK_MAX = 256
B_HEADS = 4
B_HEAD_DIM = 64
B_HALF = 32
B_W = B_HEADS * B_HEAD_DIM
C_HEADS = 4
C_HEAD_DIM = 64
C_W = C_HEADS * C_HEAD_DIM
ROPE_BASE = 10000.0
REL_BUCKETS = 32
REL_MAX_DIST = 128
D_FF = 4 * D_MODEL
EPS = 1e-6
IN_SIZES = (A_W, A_W, A_W, IDX_HEADS * IDX_DIM, IDX_DIM, IDX_HEADS, B_W, B_W, B_W, C_W, C_W, C_W, C_W)

LANES = 128
ATT_BLK = 512
IDX_ROWS = 256
IDX_CHUNK = 512
COUNT_ROWS = 128
BISECT_FIXED_STEPS = 16
SMALLEST_NORMAL = float(np.finfo(np.float32).tiny)
RET_BLK = 256
ROW_TILE = 512
FF_CHUNK = 1024
VMEM_LIMIT_BYTES = 56 * 1024 * 1024

NEG = -1e30
LOG2E = math.log2(math.e)

P16_COLS = 4 * A_W + 3 * B_W + 2 * IDX_DIM
P32_COLS = 4 * C_W + LANES

BF16 = jnp.bfloat16
F32 = jnp.float32


def _cparams(sem):
    return pltpu.CompilerParams(dimension_semantics=sem, vmem_limit_bytes=VMEM_LIMIT_BYTES)


def _dot_nt(a, b):
    return lax.dot_general(a, b, (((1,), (1,)), ((), ())), preferred_element_type=F32)


def _dot(a, b):
    return jnp.dot(a, b, preferred_element_type=F32)


def _inproj_kernel(h_ref, g_ref, w16_ref, w32_ref, o16_ref, o32_ref):
    x = h_ref[...]
    ms = jnp.mean(x * x, axis=-1, keepdims=True)
    u = ((x * lax.rsqrt(ms + EPS)) * g_ref[...]).astype(BF16)
    o16_ref[...] = _dot(u, w16_ref[...]).astype(BF16)
    o32_ref[...] = _dot(u, w32_ref[...])


def _inproj(h2d, g, w16, w32):
    rows = h2d.shape[0]
    return pl.pallas_call(
        _inproj_kernel,
        grid=(rows // ROW_TILE,),
        in_specs=[
            pl.BlockSpec((ROW_TILE, D_MODEL), lambda i: (i, 0)),
            pl.BlockSpec((1, D_MODEL), lambda i: (0, 0)),
            pl.BlockSpec((D_MODEL, P16_COLS), lambda i: (0, 0)),
            pl.BlockSpec((D_MODEL, P32_COLS), lambda i: (0, 0)),
        ],
        out_specs=[
            pl.BlockSpec((ROW_TILE, P16_COLS), lambda i: (i, 0)),
            pl.BlockSpec((ROW_TILE, P32_COLS), lambda i: (i, 0)),
        ],
        out_shape=[
            jax.ShapeDtypeStruct((rows, P16_COLS), BF16),
            jax.ShapeDtypeStruct((rows, P32_COLS), F32),
        ],
        compiler_params=_cparams(("arbitrary",)),
        name="inproj",
    )(h2d, g, w16, w32)


def _indexer_kernel(qi_ref, ki_ref, wi_ref, sc_ref, tau_ref, ntake_ref, tie_ref, qm_ref, *, topk, n_chunks):
    i = pl.program_id(1)
    row0 = i * IDX_ROWS
    n_live = (row0 + IDX_ROWS - 1) // IDX_CHUNK + 1
    n_tiles = IDX_CHUNK // LANES
    k_f = float(topk)

    lane = lax.broadcasted_iota(jnp.int32, (IDX_ROWS, LANES), 1)
    for h in range(IDX_HEADS):
        pair = qi_ref[:, (h // 2) * LANES:(h // 2 + 1) * LANES]
        keep = (lane >= IDX_DIM) if h % 2 else (lane < IDX_DIM)
        qm_ref[h] = jnp.where(keep, pair, jnp.zeros_like(pair))

    w = wi_ref[...] * (IDX_HEADS ** -0.5 * IDX_DIM ** -0.5)
    wcol = [w[:, h:h + 1] for h in range(IDX_HEADS)]
    row = row0 + lax.broadcasted_iota(jnp.int32, (IDX_ROWS, 1), 0)
    col_in_chunk = lax.broadcasted_iota(jnp.int32, (IDX_ROWS, IDX_CHUNK), 1)

    def score_chunk(c, carry):
        lo, hi = carry
        start = pl.multiple_of(c * IDX_CHUNK, IDX_CHUNK)
        kc = ki_ref[pl.ds(start, IDX_CHUNK), :]
        acc = jnp.zeros((IDX_ROWS, IDX_CHUNK), F32)
        for h in range(IDX_HEADS):
            acc = acc + wcol[h] * jnp.maximum(_dot_nt(qm_ref[h], kc), 0.0)
        causal = col_in_chunk + start <= row
        sc = jnp.where(causal, acc, -jnp.inf)
        sc_ref[:, pl.ds(start, IDX_CHUNK)] = sc
        for_min = jnp.where(causal, acc, jnp.inf)
        for t in range(n_tiles):
            lo = jnp.minimum(lo, for_min[:, t * LANES:(t + 1) * LANES])
            hi = jnp.maximum(hi, sc[:, t * LANES:(t + 1) * LANES])
        return lo, hi

    lo_l, hi_l = lax.fori_loop(0, n_live, score_chunk, (jnp.full((IDX_ROWS, LANES), jnp.inf, F32),
                                                        jnp.full((IDX_ROWS, LANES), -jnp.inf, F32)))
    across = lambda col: jnp.broadcast_to(col, (IDX_ROWS, LANES))
    lo0 = across(jnp.min(lo_l, axis=1, keepdims=True))
    hi0 = across(jnp.max(hi_l, axis=1, keepdims=True))

    def fill_chunk(c, carry):
        start = pl.multiple_of(c * IDX_CHUNK, IDX_CHUNK)
        sc_ref[:, pl.ds(start, IDX_CHUNK)] = jnp.full((IDX_ROWS, IDX_CHUNK), -jnp.inf, F32)
        return carry

    lax.fori_loop(n_live, n_chunks, fill_chunk, 0)

    def count(thr, strict=False):
        parts = []
        for r0 in range(0, IDX_ROWS, COUNT_ROWS):
            thr_r = thr[r0:r0 + COUNT_ROWS]

            def body(c, acc, r0=r0, thr_r=thr_r):
                start = pl.multiple_of(c * IDX_CHUNK, IDX_CHUNK)
                sc = sc_ref[r0:r0 + COUNT_ROWS, pl.ds(start, IDX_CHUNK)]
                for t in range(n_tiles):
                    tile = sc[:, t * LANES:(t + 1) * LANES]
                    acc = jnp.where((tile > thr_r) if strict else (tile >= thr_r), acc + 1.0, acc)
                return acc

            acc = lax.fori_loop(0, n_live, body, jnp.zeros((COUNT_ROWS, LANES), F32))
            parts.append(jnp.broadcast_to(jnp.sum(acc, axis=1, keepdims=True), (COUNT_ROWS, LANES)))
        return jnp.concatenate(parts, axis=0)

    n_valid = (row0 + 1 + lax.broadcasted_iota(jnp.int32, (IDX_ROWS, LANES), 0)).astype(F32)
    keeps_all = n_valid <= k_f
    tied_at_max = count(hi0) >= k_f
    lo0 = jnp.where(keeps_all, jnp.float32(jnp.finfo(jnp.float32).min), jnp.where(tied_at_max, hi0, lo0))
    hi0 = jnp.where(keeps_all, lo0, hi0)

    def probe(lo, hi):
        mid = 0.5 * lo + 0.5 * hi
        mid = jnp.where((lo < 0.0) & (hi > 0.0), 0.0, mid)
        mid = jnp.where((lo == 0.0) & (hi > SMALLEST_NORMAL), SMALLEST_NORMAL, mid)
        return mid, (mid > lo) & (mid < hi)

    def unfinished(state):
        return jnp.sum(jnp.where(probe(*state)[1], 1.0, 0.0)) > 0.0

    def bisect(state):
        lo, hi = state
        mid, inside = probe(lo, hi)
        cnt = count(mid)
        hit = cnt == k_f
        lo = jnp.where(inside & (hit | (cnt > k_f)), mid, lo)
        hi = jnp.where(inside & (hit | (cnt < k_f)), mid, hi)
        return lo, hi

    state = lax.fori_loop(0, BISECT_FIXED_STEPS, lambda _, st: bisect(st), (lo0, hi0))
    tau, hi = lax.while_loop(unfinished, bisect, state)
    tau_ref[...] = tau[:, :1]
    ntake_ref[...] = jnp.zeros(ntake_ref.shape, F32)
    tie_ref[...] = jnp.zeros(tie_ref.shape, jnp.int32)

    may_tie = jnp.where((tau != hi) | tied_at_max, 1.0, 0.0)

    @pl.when(jnp.sum(may_tie) > 0.0)
    def _():
        n_gt = count(tau, strict=True)
        n_ge = count(tau)
        ntake_ref[...] = k_f - n_gt[:, :1]
        tie_ref[...] = (n_ge[:, :1] > k_f).astype(jnp.int32)


def _indexer(p16, p32, topk):
    bsz, tp, _ = p16.shape
    n_chunks = tp // IDX_CHUNK
    kern = functools.partial(_indexer_kernel, topk=topk, n_chunks=n_chunks)
    col = lambda n: pl.BlockSpec((None, IDX_ROWS, 1), lambda b, i: (b, i, 0))
    return pl.pallas_call(
        kern,
        grid=(bsz, tp // IDX_ROWS),
        in_specs=[
            pl.BlockSpec((None, IDX_ROWS, A_W), lambda b, i: (b, i, 3)),
            pl.BlockSpec((None, tp, LANES), lambda b, i: (b, 0, (P16_COLS - LANES) // LANES)),
            pl.BlockSpec((None, IDX_ROWS, LANES), lambda b, i: (b, i, 4 * C_W // LANES)),
        ],
        out_specs=[
            pl.BlockSpec((None, IDX_ROWS, tp), lambda b, i: (b, i, 0)),
            col(0), col(1), col(2),
        ],
        out_shape=[
            jax.ShapeDtypeStruct((bsz, tp, tp), F32),
            jax.ShapeDtypeStruct((bsz, tp, 1), F32),
            jax.ShapeDtypeStruct((bsz, tp, 1), F32),
            jax.ShapeDtypeStruct((bsz, tp, 1), jnp.int32),
        ],
        scratch_shapes=[pltpu.VMEM((IDX_HEADS, IDX_ROWS, LANES), BF16)],
        compiler_params=_cparams(("arbitrary", "arbitrary")),
        name="indexer",
    )(p16, p16, p32)


def _lane_masked_queries(q_ref, qm_ref, n_slots, width, scale2):
    per_group = LANES // width
    lane = lax.broadcasted_iota(jnp.int32, (ATT_BLK, LANES), 1)
    for s in range(n_slots):
        g, r = divmod(s, per_group)
        grp = q_ref[:, g * LANES:(g + 1) * LANES].astype(F32) * scale2
        keep = (lane >= r * width) & (lane < (r + 1) * width)
        qm_ref[s] = jnp.where(keep, grp, 0.0).astype(BF16)


def _flash_update(s, logit2, v_ones, m_ref, acc_ref):
    m_prev = m_ref[s]
    m_new = jnp.maximum(m_prev, jnp.max(logit2, axis=1, keepdims=True))
    alpha = jnp.exp2(m_prev - m_new)
    p = jnp.concatenate([jnp.exp2(logit2[:, t * LANES:(t + 1) * LANES] - m_new)
                         for t in range(logit2.shape[1] // LANES)], axis=1).astype(BF16)
    acc_ref[s] = jnp.concatenate([alpha, alpha], axis=1) * acc_ref[s] + _dot(p, v_ones)
    m_ref[s] = m_new


def _with_ones(v_grp):
    return jnp.concatenate([v_grp, jnp.ones(v_grp.shape, v_grp.dtype)], axis=1)


def _flash_init(m_ref, acc_ref):
    m_ref[...] = jnp.full(m_ref.shape, NEG, F32)
    acc_ref[...] = jnp.zeros(acc_ref.shape, F32)


def _flash_result(s, acc_ref):
    acc = acc_ref[s]
    return acc[:, :LANES] / acc[:, LANES:]


def _triangle_steps(nq):
    qs, ks, kinds = [], [], []
    for q in range(nq):
        for k in range(q + 1):
            qs.append(q)
            ks.append(k)
            kinds.append(2 if k == q else (1 if k == q - 1 else 0))
    return (jnp.asarray(qs, jnp.int32), jnp.asarray(ks, jnp.int32), jnp.asarray(kinds, jnp.int32))


def _sparse_attn_kernel(qmap, kmap, kind, tieflag, q_ref, k_ref, v_ref, sc_ref, tau_ref, ntake_ref, bias_ref,
                        o_ref, qm_ref, m_ref, acc_ref, mask_ref, carry_ref, *, nq):
    b = pl.program_id(0)
    s = pl.program_id(1)
    qb = qmap[s]
    kb = kmap[s]

    @pl.when(kb == 0)
    def _():
        _lane_masked_queries(q_ref, qm_ref, A_HEADS, A_HEAD_DIM, A_HEAD_DIM ** -0.5 * LOG2E)
        _flash_init(m_ref, acc_ref)
        carry_ref[...] = jnp.zeros(carry_ref.shape, F32)

    has_ties = tieflag[b * nq + qb]

    @pl.when(has_ties == 0)
    def _():
        mask_ref[...] = jnp.where(sc_ref[...] >= tau_ref[...], 0.0, NEG)

    @pl.when(has_ties != 0)
    def _():
        sc = sc_ref[...]
        tau = tau_ref[...]
        eq = sc == tau
        eq_b = jnp.where(eq, 1.0, 0.0).astype(BF16)
        r = lax.broadcasted_iota(jnp.int32, (ATT_BLK, ATT_BLK), 0)
        c = lax.broadcasted_iota(jnp.int32, (ATT_BLK, ATT_BLK), 1)
        before = jnp.where(r < c, 1.0, 0.0).astype(BF16)
        seen = carry_ref[...] + _dot(eq_b, before)
        keep = (sc > tau) | (eq & (seen < ntake_ref[...]))
        mask_ref[...] = jnp.where(keep, 0.0, NEG)
        carry_ref[...] = carry_ref[...] + jnp.sum(eq_b.astype(F32), axis=1, keepdims=True)

    def attend(near_diagonal):
        for g in range(A_HEADS // 2):
            k_grp = k_ref[:, g * LANES:(g + 1) * LANES]
            v_ones = _with_ones(v_ref[:, g * LANES:(g + 1) * LANES])
            for h in (2 * g, 2 * g + 1):
                logit2 = _dot_nt(qm_ref[h], k_grp) + mask_ref[...]
                if near_diagonal:
                    logit2 = logit2 + bias_ref[h]
                _flash_update(h, logit2, v_ones, m_ref, acc_ref)

    pl.when(kind[s] == 0)(lambda: attend(False))
    pl.when(kind[s] != 0)(lambda: attend(True))

    @pl.when(kb == qb)
    def _():
        lane = lax.broadcasted_iota(jnp.int32, (ATT_BLK, LANES), 1)
        for g in range(A_HEADS // 2):
            out = jnp.where(lane < A_HEAD_DIM, _flash_result(2 * g, acc_ref), _flash_result(2 * g + 1, acc_ref))
            o_ref[:, g * LANES:(g + 1) * LANES] = out.astype(o_ref.dtype)


def _sparse_attn(p16, keys, tau, ntake, tieflag, bias_tiles):
    bsz, tp, _ = p16.shape
    nq = tp // ATT_BLK
    qmap, kmap, kind = _triangle_steps(nq)
    qblk = lambda c: pl.BlockSpec((None, ATT_BLK, A_W), lambda b, s, qm, km, kd, tf: (b, qm[s], c))
    kblk = lambda c: pl.BlockSpec((None, ATT_BLK, A_W), lambda b, s, qm, km, kd, tf: (b, km[s], c))
    colblk = pl.BlockSpec((None, ATT_BLK, 1), lambda b, s, qm, km, kd, tf: (b, qm[s], 0))
    grid_spec = pltpu.PrefetchScalarGridSpec(
        num_scalar_prefetch=4,
        grid=(bsz, int(qmap.shape[0])),
        in_specs=[
            qblk(0), kblk(1), kblk(2),
            pl.BlockSpec((None, ATT_BLK, ATT_BLK), lambda b, s, qm, km, kd, tf: (b, qm[s], km[s])),
            colblk, colblk,
            pl.BlockSpec((None, A_HEADS, ATT_BLK, ATT_BLK),
                         lambda b, s, qm, km, kd, tf: (jnp.maximum(kd[s] - 1, 0), 0, 0, 0)),
        ],
        out_specs=pl.BlockSpec((None, ATT_BLK, A_W), lambda b, s, qm, km, kd, tf: (b, qm[s], 0)),
        scratch_shapes=[
            pltpu.VMEM((A_HEADS, ATT_BLK, LANES), BF16),
            pltpu.VMEM((A_HEADS, ATT_BLK, LANES), F32),
            pltpu.VMEM((A_HEADS, ATT_BLK, 2 * LANES), F32),
            pltpu.VMEM((ATT_BLK, ATT_BLK), F32),
            pltpu.VMEM((ATT_BLK, 1), F32),
        ],
    )
    return pl.pallas_call(
        functools.partial(_sparse_attn_kernel, nq=nq),
        grid_spec=grid_spec,
        out_shape=jax.ShapeDtypeStruct((bsz, tp, A_W), BF16),
        compiler_params=_cparams(("arbitrary", "arbitrary")),
        name="sparse_attn",
    )(qmap, kmap, kind, tieflag, p16, p16, p16, keys, tau, ntake, bias_tiles)


def _diff_attn_kernel(qmap, kmap, kind, lam_ref, q_ref, k_ref, v_ref, bias_ref, g_ref,
                      o_ref, qm_ref, m_ref, acc_ref, *, out_scale):
    s = pl.program_id(1)
    qb = qmap[s]
    kb = kmap[s]
    n_slots = 2 * B_HEADS

    @pl.when(kb == 0)
    def _():
        _lane_masked_queries(q_ref, qm_ref, n_slots, B_HALF, B_HALF ** -0.5 * LOG2E)
        _flash_init(m_ref, acc_ref)

    def attend(near_diagonal):
        for g in range(B_HEADS // 2):
            k_grp = k_ref[:, g * LANES:(g + 1) * LANES]
            v_ones = _with_ones(v_ref[:, g * LANES:(g + 1) * LANES])
            for slot in range(4 * g, 4 * g + 4):
                logit2 = _dot_nt(qm_ref[slot], k_grp)
                if near_diagonal:
                    logit2 = logit2 + bias_ref[slot // 2]
                _flash_update(slot, logit2, v_ones, m_ref, acc_ref)

    pl.when(kind[s] == 0)(lambda: attend(False))
    pl.when(kind[s] != 0)(lambda: attend(True))

    @pl.when(kb == qb)
    def _():
        lam = lam_ref[0]
        lane = lax.broadcasted_iota(jnp.int32, (ATT_BLK, LANES), 1)
        low = lane < B_HEAD_DIM
        for g in range(B_HEADS // 2):
            heads = []
            for h in (2 * g, 2 * g + 1):
                heads.append(_flash_result(2 * h, acc_ref) - lam * _flash_result(2 * h + 1, acc_ref))
            x = jnp.where(low, heads[0], heads[1])
            sq = x * x
            ss_lo = jnp.sum(jnp.where(low, sq, 0.0), axis=1, keepdims=True)
            ss_hi = jnp.sum(jnp.where(low, 0.0, sq), axis=1, keepdims=True)
            ms = jnp.where(low, ss_lo, ss_hi) * (1.0 / B_HEAD_DIM)
            y = (x * lax.rsqrt(ms + EPS)) * g_ref[...]
            o_ref[:, g * LANES:(g + 1) * LANES] = (y * out_scale).astype(o_ref.dtype)


def _diff_attn(p16, lam, bias_tiles, g_pair, out_scale):
    bsz, tp, _ = p16.shape
    nq = tp // ATT_BLK
    qmap, kmap, kind = _triangle_steps(nq)
    base = 4 * A_W // B_W
    qblk = pl.BlockSpec((None, ATT_BLK, B_W), lambda b, s, qm, km, kd, lm: (b, qm[s], base))
    kblk = lambda c: pl.BlockSpec((None, ATT_BLK, B_W), lambda b, s, qm, km, kd, lm: (b, km[s], base + c))
    n_slots = 2 * B_HEADS
    grid_spec = pltpu.PrefetchScalarGridSpec(
        num_scalar_prefetch=4,
        grid=(bsz, int(qmap.shape[0])),
        in_specs=[
            qblk, kblk(1), kblk(2),
            pl.BlockSpec((None, B_HEADS, ATT_BLK, ATT_BLK),
                         lambda b, s, qm, km, kd, lm: (jnp.maximum(kd[s] - 1, 0), 0, 0, 0)),
            pl.BlockSpec((1, LANES), lambda b, s, qm, km, kd, lm: (0, 0)),
        ],
        out_specs=pl.BlockSpec((None, ATT_BLK, B_W), lambda b, s, qm, km, kd, lm: (b, qm[s], 0)),
        scratch_shapes=[
            pltpu.VMEM((n_slots, ATT_BLK, LANES), BF16),
            pltpu.VMEM((n_slots, ATT_BLK, LANES), F32),
            pltpu.VMEM((n_slots, ATT_BLK, 2 * LANES), F32),
        ],
    )
    return pl.pallas_call(
        functools.partial(_diff_attn_kernel, out_scale=out_scale),
        grid_spec=grid_spec,
        out_shape=jax.ShapeDtypeStruct((bsz, tp, B_W), BF16),
        compiler_params=_cparams(("arbitrary", "arbitrary")),
        name="diff_attn",
    )(qmap, kmap, kind, lam, p16, p16, p16, bias_tiles, g_pair)


def _retention_kernel(q_ref, k_ref, v_ref, gate_ref, cos_ref, sin_ref, dmat_ref, qdec_ref, kdec_ref, sdec_ref,
                      g_ref, o_ref, state_ref):
    c = pl.program_id(1)

    @pl.when(c == 0)
    def _():
        state_ref[...] = jnp.zeros(state_ref.shape, F32)

    lane = lax.broadcasted_iota(jnp.int32, (RET_BLK, C_W), 1)
    first_half = (lane % C_HEAD_DIM) < (C_HEAD_DIM // 2)

    def rope(x):
        swapped = jnp.where(first_half,
                            pltpu.roll(x, C_W - C_HEAD_DIM // 2, axis=1),
                            pltpu.roll(x, C_HEAD_DIM // 2, axis=1))
        return x * cos_ref[...] + swapped * sin_ref[...]

    q = rope(q_ref[...])
    k = rope(k_ref[...]) * (C_HEAD_DIM ** -0.5)
    q16 = q.astype(BF16)
    k16 = k.astype(BF16)
    v16 = v_ref[...].astype(BF16)

    state = state_ref[...]
    out = _dot(q16, state.astype(BF16)) * qdec_ref[...]
    for h in range(C_HEADS):
        in_head = (lane >= h * C_HEAD_DIM) & (lane < (h + 1) * C_HEAD_DIM)
        qh = jnp.where(in_head, q, 0.0).astype(BF16)
        inner = _dot_nt(qh, k16) * dmat_ref[h]
        out = out + jnp.where(in_head, _dot(inner.astype(BF16), v16), 0.0)

    kd = (k * kdec_ref[...]).astype(BF16)
    update = _dot(kd.T, v16)
    r = lax.broadcasted_iota(jnp.int32, (C_W, C_W), 0) // C_HEAD_DIM
    cc = lax.broadcasted_iota(jnp.int32, (C_W, C_W), 1) // C_HEAD_DIM
    state_ref[...] = state * sdec_ref[...] + jnp.where(r == cc, update, 0.0)

    sq = out * out
    ms = jnp.zeros_like(out)
    for h in range(C_HEADS):
        in_head = (lane >= h * C_HEAD_DIM) & (lane < (h + 1) * C_HEAD_DIM)
        ss = jnp.sum(jnp.where(in_head, sq, 0.0), axis=1, keepdims=True)
        ms = jnp.where(in_head, ss * (1.0 / C_HEAD_DIM), ms)
    normed = (out * lax.rsqrt(ms + EPS)) * g_ref[...]
    gate = gate_ref[...]
    o_ref[...] = ((gate * jax.nn.sigmoid(gate)) * normed).astype(o_ref.dtype)


def _retention(p32, cos, sin, dmat, qdec, kdec, sdec, g):
    bsz, tp, _ = p32.shape
    blk = lambda c: pl.BlockSpec((None, RET_BLK, C_W), lambda b, i: (b, i, c))
    const = lambda shape: pl.BlockSpec(shape, lambda b, i: (0,) * len(shape))
    return pl.pallas_call(
        _retention_kernel,
        grid=(bsz, tp // RET_BLK),
        in_specs=[
            blk(0), blk(1), blk(2), blk(3),
            pl.BlockSpec((RET_BLK, C_W), lambda b, i: (i, 0)),
            pl.BlockSpec((RET_BLK, C_W), lambda b, i: (i, 0)),
            const((C_HEADS, RET_BLK, RET_BLK)),
            const((RET_BLK, C_W)), const((RET_BLK, C_W)), const((1, C_W)), const((1, C_W)),
        ],
        out_specs=pl.BlockSpec((None, RET_BLK, C_W), lambda b, i: (b, i, 0)),
        out_shape=jax.ShapeDtypeStruct((bsz, tp, C_W), BF16),
        scratch_shapes=[pltpu.VMEM((C_W, C_W), F32)],
        compiler_params=_cparams(("arbitrary", "arbitrary")),
        name="retention",
    )(p32, p32, p32, p32, cos, sin, dmat, qdec, kdec, sdec, g)


def _mix_mlp_kernel(h_ref, a_ref, b_ref, c_ref, wo_ref, g_ref, w1_ref, w2_ref, gf_ref, o_ref, u_ref, acc_ref,
                    *, final_norm):
    j = pl.program_id(1)

    @pl.when(j == 0)
    def _():
        mixed = (_dot(a_ref[...], wo_ref[0:A_W, :])
                 + _dot(b_ref[...], wo_ref[A_W:A_W + B_W, :])
                 + _dot(c_ref[...], wo_ref[A_W + B_W:, :]))
        h1 = h_ref[...] + mixed
        acc_ref[...] = h1
        ms = jnp.mean(h1 * h1, axis=-1, keepdims=True)
        u_ref[...] = ((h1 * lax.rsqrt(ms + EPS)) * g_ref[...]).astype(BF16)

    t = jnp.maximum(_dot(u_ref[...], w1_ref[...]), 0.0)
    acc_ref[...] += _dot((t * t).astype(BF16), w2_ref[...])

    @pl.when(j == pl.num_programs(1) - 1)
    def _():
        y = acc_ref[...]
        if final_norm:
            ms = jnp.mean(y * y, axis=-1, keepdims=True)
            y = (y * lax.rsqrt(ms + EPS)) * gf_ref[...]
        o_ref[...] = y


def _mix_mlp(h2d, a, b, c, wo, g, w1, w2, gf, final_norm):
    rows = h2d.shape[0]
    row_blk = lambda w: pl.BlockSpec((ROW_TILE, w), lambda i, j: (i, 0))
    vec = pl.BlockSpec((1, D_MODEL), lambda i, j: (0, 0))
    return pl.pallas_call(
        functools.partial(_mix_mlp_kernel, final_norm=final_norm),
        grid=(rows // ROW_TILE, D_FF // FF_CHUNK),
        in_specs=[
            row_blk(D_MODEL), row_blk(A_W), row_blk(B_W), row_blk(C_W),
            pl.BlockSpec((D_MODEL, D_MODEL), lambda i, j: (0, 0)),
            vec,
            pl.BlockSpec((D_MODEL, FF_CHUNK), lambda i, j: (0, j)),
            pl.BlockSpec((FF_CHUNK, D_MODEL), lambda i, j: (j, 0)),
            vec,
        ],
        out_specs=row_blk(D_MODEL),
        out_shape=jax.ShapeDtypeStruct((rows, D_MODEL), F32),
        scratch_shapes=[pltpu.VMEM((ROW_TILE, D_MODEL), BF16), pltpu.VMEM((ROW_TILE, D_MODEL), F32)],
        compiler_params=_cparams(("arbitrary", "arbitrary")),
        name="mix_mlp",
    )(h2d, a, b, c, wo, g, w1, w2, gf)


def _t5_bucket(dist):
    n = jnp.maximum(dist, 0)
    max_exact = REL_BUCKETS // 2
    nf = jnp.maximum(n, 1).astype(F32)
    large = max_exact + (jnp.log(nf / max_exact) / math.log(REL_MAX_DIST / max_exact)
                         * (REL_BUCKETS - max_exact)).astype(jnp.int32)
    large = jnp.minimum(large, REL_BUCKETS - 1)
    return jnp.where(n < max_exact, n, large)


def _far_bucket_from():
    max_exact = REL_BUCKETS // 2
    n = np.arange(1, 4 * REL_MAX_DIST, dtype=np.float32)
    large = max_exact + (np.log(n / max_exact) / math.log(REL_MAX_DIST / max_exact)
                         * (REL_BUCKETS - max_exact)).astype(np.int32)
    bucket = np.where(n < max_exact, n, np.minimum(large, REL_BUCKETS - 1))
    return int(np.nonzero(bucket < REL_BUCKETS - 1)[0].max()) + 2


_FAR_BUCKET_FROM = _far_bucket_from()


def _bias_tiles(table):
    i = jnp.arange(ATT_BLK)[:, None]
    j = jnp.arange(ATT_BLK)[None, :]
    assert _FAR_BUCKET_FROM <= ATT_BLK + 1
    far = table[REL_BUCKETS - 1].astype(F32)
    table2 = (table.astype(F32) - far[None, :]).T * LOG2E
    tiles = []
    for offset in (ATT_BLK, 0):
        dist = i - j + offset
        bucket = _t5_bucket(dist)
        t = jnp.zeros((table2.shape[0], ATT_BLK, ATT_BLK), F32)
        for b in range(REL_BUCKETS):
            t = jnp.where(bucket[None] == b, table2[:, b, None, None], t)
        tiles.append(jnp.where(dist >= 0, t, NEG))
    return jnp.stack(tiles)


def _retention_constants(tp):
    nh, dk = C_HEADS, C_HEAD_DIM
    log_gamma = jnp.log1p(-jnp.exp2(-5.0 - jnp.arange(nh, dtype=F32)))
    i = jnp.arange(RET_BLK, dtype=F32)
    diff = i[:, None] - i[None, :]
    dmat = jnp.where(diff >= 0, jnp.exp(log_gamma[:, None, None] * jnp.maximum(diff, 0.0)), 0.0)
    per_lane = lambda a: jnp.repeat(a, dk, axis=-1)
    qdec = per_lane(jnp.exp(log_gamma[None, :] * (i[:, None] + 1.0)))
    kdec = per_lane(jnp.exp(log_gamma[None, :] * (RET_BLK - 1.0 - i[:, None])))
    sdec = per_lane(jnp.exp(log_gamma * RET_BLK)[None, :])
    inv = ROPE_BASE ** (-jnp.arange(0, dk, 2, dtype=F32) / dk)
    ang = jnp.arange(tp, dtype=F32)[:, None] * inv[None, :]
    cos, sin = jnp.cos(ang), jnp.sin(ang)
    cos_full = jnp.tile(jnp.concatenate([cos, cos], axis=-1), (1, nh))
    sin_full = jnp.tile(jnp.concatenate([-sin, sin], axis=-1), (1, nh))
    return cos_full, sin_full, dmat, qdec, kdec, sdec


def _split_w_in(w):
    parts, acc = [], 0
    for size in IN_SIZES:
        parts.append(w[:, acc:acc + size])
        acc += size
    return parts


def kernel(x, meta, rel_bias, w_in, norm_mix, diff_lambda, diff_norm, ret_norm, w_out, norm_ff, w_ff1, w_ff2,
           final_norm):
    bsz, s_len, _ = x.shape
    depth = w_in.shape[0]
    t_len = s_len + N_META
    tp = -(-t_len // ATT_BLK) * ATT_BLK
    topk = min(TOPK_MAX, s_len // 4)
    nq = tp // ATT_BLK

    h = jnp.concatenate([
        jnp.broadcast_to(meta.astype(x.dtype)[None], (bsz, N_META, D_MODEL)), x,
        jnp.zeros((bsz, tp - t_len, D_MODEL), x.dtype)], axis=1).reshape(bsz * tp, D_MODEL)

    bias_a = _bias_tiles(rel_bias[:, :A_HEADS])
    bias_b = _bias_tiles(rel_bias[:, A_HEADS:])
    cos, sin, dmat, qdec, kdec, sdec = _retention_constants(tp)
    row = lambda v: v.reshape(1, -1).astype(F32)

    for l in range(depth):
        qa, ka, va, qi, ki, wi, qb, kb, vb, qc, kc, vc, gc = _split_w_in(w_in[l])
        w16 = jnp.concatenate([qa, ka, va, qi, qb, kb, vb, ki, ki], axis=1).astype(BF16)
        w32 = jnp.concatenate([qc, kc, vc, gc, wi, jnp.zeros((D_MODEL, LANES - IDX_HEADS), F32)],
                              axis=1).astype(BF16)
        p16, p32 = _inproj(h, row(norm_mix[l]), w16, w32)
        p16 = p16.reshape(bsz, tp, P16_COLS)
        p32 = p32.reshape(bsz, tp, P32_COLS)

        keys, tau, ntake, tie = _indexer(p16, p32, topk)
        tieflag = jnp.max(tie.reshape(bsz * nq, ATT_BLK), axis=1)
        a_out = _sparse_attn(p16, keys, tau, ntake, tieflag, bias_a)

        lambda_init = 0.8 - 0.6 * math.exp(-0.3 * l)
        lp = diff_lambda[l].astype(F32)
        lam = jnp.exp(jnp.sum(lp[0] * lp[1])) - jnp.exp(jnp.sum(lp[2] * lp[3])) + lambda_init
        g_pair = row(jnp.concatenate([diff_norm[l], diff_norm[l]]))
        b_out = _diff_attn(p16, lam.reshape(1), bias_b, g_pair, 1.0 - lambda_init)

        c_out = _retention(p32, cos, sin, dmat, qdec, kdec, sdec, row(ret_norm[l]))

        last = l == depth - 1
        h = _mix_mlp(h, a_out.reshape(bsz * tp, A_W), b_out.reshape(bsz * tp, B_W), c_out.reshape(bsz * tp, C_W),
                     w_out[l].astype(BF16), row(norm_ff[l]), w_ff1[l].astype(BF16), w_ff2[l].astype(BF16),
                     row(final_norm), final_norm=last)

    return h.reshape(bsz, tp, D_MODEL)[:, N_META:t_len]
```

```python
import functools
import math

import numpy as np
import jax
import jax.numpy as jnp
from jax import lax
from jax.experimental import pallas as pl
from jax.experimental.pallas import tpu as pltpu

D_MODEL = 1024
N_META = 16
A_HEADS = 8
A_HEAD_DIM = 64
A_W = A_HEADS * A_HEAD_DIM
IDX_HEADS = 8
IDX_DIM = 64
TOPK_MAX = 256
B_HEADS = 4
B_HEAD_DIM = 64
B_HALF = 32
B_W = B_HEADS * B_HEAD_DIM
C_HEADS = 4
C_HEAD_DIM = 64
C_W = C_HEADS * C_HEAD_DIM
ROPE_BASE = 10000.0
REL_BUCKETS = 32
REL_MAX_DIST = 128
D_FF = 4 * D_MODEL
EPS = 1e-6
IN_SIZES = (A_W, A_W, A_W, IDX_HEADS * IDX_DIM, IDX_DIM, IDX_HEADS, B_W, B_W, B_W, C_W, C_W, C_W, C_W)

LANES = 128
ATT_BLK = 512
IDX_ROWS = 256
IDX_CHUNK = 512
COUNT_ROWS = 128
BISECT_FIXED_STEPS = 16
SMALLEST_NORMAL = float(np.finfo(np.float32).tiny)
RET_BLK = 256
ROW_TILE = 512
FF_CHUNK = 1024
VMEM_LIMIT_BYTES = 56 * 1024 * 1024

NEG = -1e30
LOG2E = math.log2(math.e)

P16_COLS = 4 * A_W + 3 * B_W + 2 * IDX_DIM
P32_COLS = 4 * C_W + LANES

BF16 = jnp.bfloat16
F32 = jnp.float32


def _cparams(sem):
    return pltpu.CompilerParams(dimension_semantics=sem, vmem_limit_bytes=VMEM_LIMIT_BYTES)


def _dot_nt(a, b):
    return lax.dot_general(a, b, (((1,), (1,)), ((), ())), preferred_element_type=F32)


def _dot(a, b):
    return jnp.dot(a, b, preferred_element_type=F32)


def _inproj_kernel(h_ref, g_ref, w16_ref, w32_ref, o16_ref, o32_ref):
    x = h_ref[...]
    ms = jnp.mean(x * x, axis=-1, keepdims=True)
    u = ((x * lax.rsqrt(ms + EPS)) * g_ref[...]).astype(BF16)
    o16_ref[...] = _dot(u, w16_ref[...]).astype(BF16)
    o32_ref[...] = _dot(u, w32_ref[...])


def _inproj(h2d, g, w16, w32):
    rows = h2d.shape[0]
    return pl.pallas_call(
        _inproj_kernel,
        grid=(rows // ROW_TILE,),
        in_specs=[
            pl.BlockSpec((ROW_TILE, D_MODEL), lambda i: (i, 0)),
            pl.BlockSpec((1, D_MODEL), lambda i: (0, 0)),
            pl.BlockSpec((D_MODEL, P16_COLS), lambda i: (0, 0)),
            pl.BlockSpec((D_MODEL, P32_COLS), lambda i: (0, 0)),
        ],
        out_specs=[
            pl.BlockSpec((ROW_TILE, P16_COLS), lambda i: (i, 0)),
            pl.BlockSpec((ROW_TILE, P32_COLS), lambda i: (i, 0)),
        ],
        out_shape=[
            jax.ShapeDtypeStruct((rows, P16_COLS), BF16),
            jax.ShapeDtypeStruct((rows, P32_COLS), F32),
        ],
        compiler_params=_cparams(("arbitrary",)),
        name="inproj",
    )(h2d, g, w16, w32)


def _indexer_kernel(qi_ref, ki_ref, wi_ref, sc_ref, tau_ref, ntake_ref, tie_ref, qm_ref, *, topk, n_chunks):
    i = pl.program_id(1)
    row0 = i * IDX_ROWS
    n_live = (row0 + IDX_ROWS - 1) // IDX_CHUNK + 1
    n_tiles = IDX_CHUNK // LANES
    k_f = float(topk)

    lane = lax.broadcasted_iota(jnp.int32, (IDX_ROWS, LANES), 1)
    for h in range(IDX_HEADS):
        pair = qi_ref[:, (h // 2) * LANES:(h // 2 + 1) * LANES]
        keep = (lane >= IDX_DIM) if h % 2 else (lane < IDX_DIM)
        qm_ref[h] = jnp.where(keep, pair, jnp.zeros_like(pair))

    w = wi_ref[...] * (IDX_HEADS ** -0.5 * IDX_DIM ** -0.5)
    wcol = [w[:, h:h + 1] for h in range(IDX_HEADS)]
    row = row0 + lax.broadcasted_iota(jnp.int32, (IDX_ROWS, 1), 0)
    col_in_chunk = lax.broadcasted_iota(jnp.int32, (IDX_ROWS, IDX_CHUNK), 1)

    def score_chunk(c, carry):
        lo, hi = carry
        start = pl.multiple_of(c * IDX_CHUNK, IDX_CHUNK)
        kc = ki_ref[pl.ds(start, IDX_CHUNK), :]
        acc = jnp.zeros((IDX_ROWS, IDX_CHUNK), F32)
        for h in range(IDX_HEADS):
            acc = acc + wcol[h] * jnp.maximum(_dot_nt(qm_ref[h], kc), 0.0)
        causal = col_in_chunk + start <= row
        sc = jnp.where(causal, acc, -jnp.inf)
        sc_ref[:, pl.ds(start, IDX_CHUNK)] = sc
        for_min = jnp.where(causal, acc, jnp.inf)
        for t in range(n_tiles):
            lo = jnp.minimum(lo, for_min[:, t * LANES:(t + 1) * LANES])
            hi = jnp.maximum(hi, sc[:, t * LANES:(t + 1) * LANES])
        return lo, hi

    lo_l, hi_l = lax.fori_loop(0, n_live, score_chunk, (jnp.full((IDX_ROWS, LANES), jnp.inf, F32),
                                                        jnp.full((IDX_ROWS, LANES), -jnp.inf, F32)))
    across = lambda col: jnp.broadcast_to(col, (IDX_ROWS, LANES))
    lo0 = across(jnp.min(lo_l, axis=1, keepdims=True))
    hi0 = across(jnp.max(hi_l, axis=1, keepdims=True))

    def fill_chunk(c, carry):
        start = pl.multiple_of(c * IDX_CHUNK, IDX_CHUNK)
        sc_ref[:, pl.ds(start, IDX_CHUNK)] = jnp.full((IDX_ROWS, IDX_CHUNK), -jnp.inf, F32)
        return carry

    lax.fori_loop(n_live, n_chunks, fill_chunk, 0)

    def count(thr, strict=False):
        parts = []
        for r0 in range(0, IDX_ROWS, COUNT_ROWS):
            thr_r = thr[r0:r0 + COUNT_ROWS]

            def body(c, acc, r0=r0, thr_r=thr_r):
                start = pl.multiple_of(c * IDX_CHUNK, IDX_CHUNK)
                sc = sc_ref[r0:r0 + COUNT_ROWS, pl.ds(start, IDX_CHUNK)]
                for t in range(n_tiles):
                    tile = sc[:, t * LANES:(t + 1) * LANES]
                    acc = jnp.where((tile > thr_r) if strict else (tile >= thr_r), acc + 1.0, acc)
                return acc

            acc = lax.fori_loop(0, n_live, body, jnp.zeros((COUNT_ROWS, LANES), F32))
            parts.append(jnp.broadcast_to(jnp.sum(acc, axis=1, keepdims=True), (COUNT_ROWS, LANES)))
        return jnp.concatenate(parts, axis=0)

    n_valid = (row0 + 1 + lax.broadcasted_iota(jnp.int32, (IDX_ROWS, LANES), 0)).astype(F32)
    keeps_all = n_valid <= k_f
    tied_at_max = count(hi0) >= k_f
    lo0 = jnp.where(keeps_all, jnp.float32(jnp.finfo(jnp.float32).min), jnp.where(tied_at_max, hi0, lo0))
    hi0 = jnp.where(keeps_all, lo0, hi0)

    def probe(lo, hi):
        mid = 0.5 * lo + 0.5 * hi
        mid = jnp.where((lo < 0.0) & (hi > 0.0), 0.0, mid)
        mid = jnp.where((lo == 0.0) & (hi > SMALLEST_NORMAL), SMALLEST_NORMAL, mid)
        return mid, (mid > lo) & (mid < hi)

    def unfinished(state):
        return jnp.sum(jnp.where(probe(*state)[1], 1.0, 0.0)) > 0.0

    def bisect(state):
        lo, hi = state
        mid, inside = probe(lo, hi)
        cnt = count(mid)
        hit = cnt == k_f
        lo = jnp.where(inside & (hit | (cnt > k_f)), mid, lo)
        hi = jnp.where(inside & (hit | (cnt < k_f)), mid, hi)
        return lo, hi

    state = lax.fori_loop(0, BISECT_FIXED_STEPS, lambda _, st: bisect(st), (lo0, hi0))
    tau, hi = lax.while_loop(unfinished, bisect, state)
    tau_ref[...] = tau[:, :1]
    ntake_ref[...] = jnp.zeros(ntake_ref.shape, F32)
    tie_ref[...] = jnp.zeros(tie_ref.shape, jnp.int32)

    may_tie = jnp.where((tau != hi) | tied_at_max, 1.0, 0.0)

    @pl.when(jnp.sum(may_tie) > 0.0)
    def _():
        n_gt = count(tau, strict=True)
        n_ge = count(tau)
        ntake_ref[...] = k_f - n_gt[:, :1]
        tie_ref[...] = (n_ge[:, :1] > k_f).astype(jnp.int32)


def _indexer(p16, p32, topk):
    bsz, tp, _ = p16.shape
    n_chunks = tp // IDX_CHUNK
    kern = functools.partial(_indexer_kernel, topk=topk, n_chunks=n_chunks)
    col = lambda n: pl.BlockSpec((None, IDX_ROWS, 1), lambda b, i: (b, i, 0))
    return pl.pallas_call(
        kern,
        grid=(bsz, tp // IDX_ROWS),
        in_specs=[
            pl.BlockSpec((None, IDX_ROWS, A_W), lambda b, i: (b, i, 3)),
            pl.BlockSpec((None, tp, LANES), lambda b, i: (b, 0, (P16_COLS - LANES) // LANES)),
            pl.BlockSpec((None, IDX_ROWS, LANES), lambda b, i: (b, i, 4 * C_W // LANES)),
        ],
        out_specs=[
            pl.BlockSpec((None, IDX_ROWS, tp), lambda b, i: (b, i, 0)),
            col(0), col(1), col(2),
        ],
        out_shape=[
            jax.ShapeDtypeStruct((bsz, tp, tp), F32),
            jax.ShapeDtypeStruct((bsz, tp, 1), F32),
            jax.ShapeDtypeStruct((bsz, tp, 1), F32),
            jax.ShapeDtypeStruct((bsz, tp, 1), jnp.int32),
        ],
        scratch_shapes=[pltpu.VMEM((IDX_HEADS, IDX_ROWS, LANES), BF16)],
        compiler_params=_cparams(("arbitrary", "arbitrary")),
        name="indexer",
    )(p16, p16, p32)


def _lane_masked_queries(q_ref, qm_ref, n_slots, width, scale2):
    per_group = LANES // width
    lane = lax.broadcasted_iota(jnp.int32, (ATT_BLK, LANES), 1)
    for s in range(n_slots):
        g, r = divmod(s, per_group)
        grp = q_ref[:, g * LANES:(g + 1) * LANES].astype(F32) * scale2
        keep = (lane >= r * width) & (lane < (r + 1) * width)
        qm_ref[s] = jnp.where(keep, grp, 0.0).astype(BF16)


def _flash_update(s, logit2, v_ones, m_ref, acc_ref):
    m_prev = m_ref[s]
    m_new = jnp.maximum(m_prev, jnp.max(logit2, axis=1, keepdims=True))
    alpha = jnp.exp2(m_prev - m_new)
    p = jnp.concatenate([jnp.exp2(logit2[:, t * LANES:(t + 1) * LANES] - m_new)
                         for t in range(logit2.shape[1] // LANES)], axis=1).astype(BF16)
    acc_ref[s] = jnp.concatenate([alpha, alpha], axis=1) * acc_ref[s] + _dot(p, v_ones)
    m_ref[s] = m_new


def _with_ones(v_grp):
    return jnp.concatenate([v_grp, jnp.ones(v_grp.shape, v_grp.dtype)], axis=1)


def _flash_init(m_ref, acc_ref):
    m_ref[...] = jnp.full(m_ref.shape, NEG, F32)
    acc_ref[...] = jnp.zeros(acc_ref.shape, F32)


def _flash_result(s, acc_ref):
    acc = acc_ref[s]
    return acc[:, :LANES] / acc[:, LANES:]


def _triangle_steps(nq):
    qs, ks, kinds = [], [], []
    for q in range(nq):
        for k in range(q + 1):
            qs.append(q)
            ks.append(k)
            kinds.append(2 if k == q else (1 if k == q - 1 else 0))
    return (jnp.asarray(qs, jnp.int32), jnp.asarray(ks, jnp.int32), jnp.asarray(kinds, jnp.int32))


def _sparse_attn_steps(has_ties, q_ref, k_ref, v_ref, sc_ref, tau_ref, ntake_ref, bias_ref,
                       o_ref, qm_ref, m_ref, acc_ref, mask_ref, carry_ref):
    def init():
        _lane_masked_queries(q_ref, qm_ref, A_HEADS, A_HEAD_DIM, A_HEAD_DIM ** -0.5 * LOG2E)
        _flash_init(m_ref, acc_ref)
        carry_ref[...] = jnp.zeros(carry_ref.shape, F32)

    def select():
        @pl.when(has_ties == 0)
        def _():
            mask_ref[...] = jnp.where(sc_ref[...] >= tau_ref[...], 0.0, NEG)

        @pl.when(has_ties != 0)
        def _():
            sc = sc_ref[...]
            tau = tau_ref[...]
            eq = sc == tau
            eq_b = jnp.where(eq, 1.0, 0.0).astype(BF16)
            r = lax.broadcasted_iota(jnp.int32, (ATT_BLK, ATT_BLK), 0)
            c = lax.broadcasted_iota(jnp.int32, (ATT_BLK, ATT_BLK), 1)
            before = jnp.where(r < c, 1.0, 0.0).astype(BF16)
            seen = carry_ref[...] + _dot(eq_b, before)
            keep = (sc > tau) | (eq & (seen < ntake_ref[...]))
            mask_ref[...] = jnp.where(keep, 0.0, NEG)
            carry_ref[...] = carry_ref[...] + jnp.sum(eq_b.astype(F32), axis=1, keepdims=True)

    def attend(near_diagonal):
        for g in range(A_HEADS // 2):
            k_grp = k_ref[:, g * LANES:(g + 1) * LANES]
            v_ones = _with_ones(v_ref[:, g * LANES:(g + 1) * LANES])
            for h in (2 * g, 2 * g + 1):
                logit2 = _dot_nt(qm_ref[h], k_grp) + mask_ref[...]
                if near_diagonal:
                    logit2 = logit2 + bias_ref[h]
                _flash_update(h, logit2, v_ones, m_ref, acc_ref)

    def finish():
        lane = lax.broadcasted_iota(jnp.int32, (ATT_BLK, LANES), 1)
        for g in range(A_HEADS // 2):
            out = jnp.where(lane < A_HEAD_DIM, _flash_result(2 * g, acc_ref), _flash_result(2 * g + 1, acc_ref))
            o_ref[:, g * LANES:(g + 1) * LANES] = out.astype(o_ref.dtype)

    return init, select, attend, finish


def _diff_attn_steps(lam_ref, q_ref, k_ref, v_ref, bias_ref, g_ref, o_ref, qm_ref, m_ref, acc_ref, out_scale):
    n_slots = 2 * B_HEADS

    def init():
        _lane_masked_queries(q_ref, qm_ref, n_slots, B_HALF, B_HALF ** -0.5 * LOG2E)
        _flash_init(m_ref, acc_ref)

    def attend(near_diagonal):
        for g in range(B_HEADS // 2):
            k_grp = k_ref[:, g * LANES:(g + 1) * LANES]
            v_ones = _with_ones(v_ref[:, g * LANES:(g + 1) * LANES])
            for slot in range(4 * g, 4 * g + 4):
                logit2 = _dot_nt(qm_ref[slot], k_grp)
                if near_diagonal:
                    logit2 = logit2 + bias_ref[slot // 2]
                _flash_update(slot, logit2, v_ones, m_ref, acc_ref)

    def finish():
        lam = lam_ref[0]
        lane = lax.broadcasted_iota(jnp.int32, (ATT_BLK, LANES), 1)
        low = lane < B_HEAD_DIM
        for g in range(B_HEADS // 2):
            heads = []
            for h in (2 * g, 2 * g + 1):
                heads.append(_flash_result(2 * h, acc_ref) - lam * _flash_result(2 * h + 1, acc_ref))
            x = jnp.where(low, heads[0], heads[1])
            sq = x * x
            ss_lo = jnp.sum(jnp.where(low, sq, 0.0), axis=1, keepdims=True)
            ss_hi = jnp.sum(jnp.where(low, 0.0, sq), axis=1, keepdims=True)
            ms = jnp.where(low, ss_lo, ss_hi) * (1.0 / B_HEAD_DIM)
            y = (x * lax.rsqrt(ms + EPS)) * g_ref[...]
            o_ref[:, g * LANES:(g + 1) * LANES] = (y * out_scale).astype(o_ref.dtype)

    return init, attend, finish


def _attention_kernel(qmap, kmap, kind, tieflag, lam_ref,
                      qa_ref, ka_ref, va_ref, sc_ref, tau_ref, ntake_ref, bias_a_ref,
                      qb_ref, kb_ref, vb_ref, bias_b_ref, gb_ref,
                      oa_ref, ob_ref,
                      qma_ref, ma_ref, acca_ref, mask_ref, carry_ref, qmb_ref, mb_ref, accb_ref, *, nq, out_scale):
    b = pl.program_id(0)
    s = pl.program_id(1)
    q_blk = qmap[s]
    k_blk = kmap[s]
    init_a, select_a, attend_a, finish_a = _sparse_attn_steps(
        tieflag[b * nq + q_blk], qa_ref, ka_ref, va_ref, sc_ref, tau_ref, ntake_ref, bias_a_ref,
        oa_ref, qma_ref, ma_ref, acca_ref, mask_ref, carry_ref)
    init_b, attend_b, finish_b = _diff_attn_steps(
        lam_ref, qb_ref, kb_ref, vb_ref, bias_b_ref, gb_ref, ob_ref, qmb_ref, mb_ref, accb_ref, out_scale)

    @pl.when(k_blk == 0)
    def _():
        init_a()
        init_b()

    select_a()

    @pl.when(kind[s] == 0)
    def _():
        attend_a(False)
        attend_b(False)

    @pl.when(kind[s] != 0)
    def _():
        attend_a(True)
        attend_b(True)

    @pl.when(k_blk == q_blk)
    def _():
        finish_a()
        finish_b()


def _attention(p16, scores, tau, ntake, tieflag, lam, bias_a, bias_b, g_pair, out_scale):
    bsz, tp, _ = p16.shape
    nq = tp // ATT_BLK
    qmap, kmap, kind = _triangle_steps(nq)
    base_b = 4 * A_W // B_W
    at_q = lambda w, c: pl.BlockSpec((None, ATT_BLK, w), lambda b, s, qm, km, kd, tf, lm: (b, qm[s], c))
    at_k = lambda w, c: pl.BlockSpec((None, ATT_BLK, w), lambda b, s, qm, km, kd, tf, lm: (b, km[s], c))
    bias = lambda heads: pl.BlockSpec((None, heads, ATT_BLK, ATT_BLK),
                                      lambda b, s, qm, km, kd, tf, lm: (jnp.maximum(kd[s] - 1, 0), 0, 0, 0))
    n_slots = 2 * B_HEADS
    grid_spec = pltpu.PrefetchScalarGridSpec(
        num_scalar_prefetch=5,
        grid=(bsz, int(qmap.shape[0])),
        in_specs=[
            at_q(A_W, 0), at_k(A_W, 1), at_k(A_W, 2),
            pl.BlockSpec((None, ATT_BLK, ATT_BLK), lambda b, s, qm, km, kd, tf, lm: (b, qm[s], km[s])),
            at_q(1, 0), at_q(1, 0),
            bias(A_HEADS),
            at_q(B_W, base_b), at_k(B_W, base_b + 1), at_k(B_W, base_b + 2),
            bias(B_HEADS),
            pl.BlockSpec((1, LANES), lambda b, s, qm, km, kd, tf, lm: (0, 0)),
        ],
        out_specs=[at_q(A_W, 0), at_q(B_W, 0)],
        scratch_shapes=[
            pltpu.VMEM((A_HEADS, ATT_BLK, LANES), BF16),
            pltpu.VMEM((A_HEADS, ATT_BLK, LANES), F32),
            pltpu.VMEM((A_HEADS, ATT_BLK, 2 * LANES), F32),
            pltpu.VMEM((ATT_BLK, ATT_BLK), F32),
            pltpu.VMEM((ATT_BLK, 1), F32),
            pltpu.VMEM((n_slots, ATT_BLK, LANES), BF16),
            pltpu.VMEM((n_slots, ATT_BLK, LANES), F32),
            pltpu.VMEM((n_slots, ATT_BLK, 2 * LANES), F32),
        ],
    )
    return pl.pallas_call(
        functools.partial(_attention_kernel, nq=nq, out_scale=out_scale),
        grid_spec=grid_spec,
        out_shape=[jax.ShapeDtypeStruct((bsz, tp, A_W), BF16), jax.ShapeDtypeStruct((bsz, tp, B_W), BF16)],
        compiler_params=_cparams(("arbitrary", "arbitrary")),
        name="attention",
    )(qmap, kmap, kind, tieflag, lam, p16, p16, p16, scores, tau, ntake, bias_a, p16, p16, p16, bias_b, g_pair)


def _retention_kernel(q_ref, k_ref, v_ref, gate_ref, cos_ref, sin_ref, dmat_ref, qdec_ref, kdec_ref, sdec_ref,
                      g_ref, o_ref, state_ref):
    c = pl.program_id(1)

    @pl.when(c == 0)
    def _():
        state_ref[...] = jnp.zeros(state_ref.shape, F32)

    lane = lax.broadcasted_iota(jnp.int32, (RET_BLK, C_W), 1)
    first_half = (lane % C_HEAD_DIM) < (C_HEAD_DIM // 2)

    def rope(x):
        swapped = jnp.where(first_half,
                            pltpu.roll(x, C_W - C_HEAD_DIM // 2, axis=1),
                            pltpu.roll(x, C_HEAD_DIM // 2, axis=1))
        return x * cos_ref[...] + swapped * sin_ref[...]

    q = rope(q_ref[...])
    k = rope(k_ref[...]) * (C_HEAD_DIM ** -0.5)
    q16 = q.astype(BF16)
    k16 = k.astype(BF16)
    v16 = v_ref[...].astype(BF16)

    state = state_ref[...]
    out = _dot(q16, state.astype(BF16)) * qdec_ref[...]
    for h in range(C_HEADS):
        in_head = (lane >= h * C_HEAD_DIM) & (lane < (h + 1) * C_HEAD_DIM)
        qh = jnp.where(in_head, q, 0.0).astype(BF16)
        inner = _dot_nt(qh, k16) * dmat_ref[h]
        out = out + jnp.where(in_head, _dot(inner.astype(BF16), v16), 0.0)

    kd = (k * kdec_ref[...]).astype(BF16)
    update = _dot(kd.T, v16)
    r = lax.broadcasted_iota(jnp.int32, (C_W, C_W), 0) // C_HEAD_DIM
    cc = lax.broadcasted_iota(jnp.int32, (C_W, C_W), 1) // C_HEAD_DIM
    state_ref[...] = state * sdec_ref[...] + jnp.where(r == cc, update, 0.0)

    sq = out * out
    ms = jnp.zeros_like(out)
    for h in range(C_HEADS):
        in_head = (lane >= h * C_HEAD_DIM) & (lane < (h + 1) * C_HEAD_DIM)
        ss = jnp.sum(jnp.where(in_head, sq, 0.0), axis=1, keepdims=True)
        ms = jnp.where(in_head, ss * (1.0 / C_HEAD_DIM), ms)
    normed = (out * lax.rsqrt(ms + EPS)) * g_ref[...]
    gate = gate_ref[...]
    o_ref[...] = ((gate * jax.nn.sigmoid(gate)) * normed).astype(o_ref.dtype)


def _retention(p32, cos, sin, dmat, qdec, kdec, sdec, g):
    bsz, tp, _ = p32.shape
    blk = lambda c: pl.BlockSpec((None, RET_BLK, C_W), lambda b, i: (b, i, c))
    const = lambda shape: pl.BlockSpec(shape, lambda b, i: (0,) * len(shape))
    return pl.pallas_call(
        _retention_kernel,
        grid=(bsz, tp // RET_BLK),
        in_specs=[
            blk(0), blk(1), blk(2), blk(3),
            pl.BlockSpec((RET_BLK, C_W), lambda b, i: (i, 0)),
            pl.BlockSpec((RET_BLK, C_W), lambda b, i: (i, 0)),
            const((C_HEADS, RET_BLK, RET_BLK)),
            const((RET_BLK, C_W)), const((RET_BLK, C_W)), const((1, C_W)), const((1, C_W)),
        ],
        out_specs=pl.BlockSpec((None, RET_BLK, C_W), lambda b, i: (b, i, 0)),
        out_shape=jax.ShapeDtypeStruct((bsz, tp, C_W), BF16),
        scratch_shapes=[pltpu.VMEM((C_W, C_W), F32)],
        compiler_params=_cparams(("arbitrary", "arbitrary")),
        name="retention",
    )(p32, p32, p32, p32, cos, sin, dmat, qdec, kdec, sdec, g)


def _mix_mlp_kernel(h_ref, a_ref, b_ref, c_ref, wo_ref, g_ref, w1_ref, w2_ref, gf_ref, o_ref, u_ref, acc_ref,
                    *, final_norm):
    j = pl.program_id(1)

    @pl.when(j == 0)
    def _():
        mixed = (_dot(a_ref[...], wo_ref[0:A_W, :])
                 + _dot(b_ref[...], wo_ref[A_W:A_W + B_W, :])
                 + _dot(c_ref[...], wo_ref[A_W + B_W:, :]))
        h1 = h_ref[...] + mixed
        acc_ref[...] = h1
        ms = jnp.mean(h1 * h1, axis=-1, keepdims=True)
        u_ref[...] = ((h1 * lax.rsqrt(ms + EPS)) * g_ref[...]).astype(BF16)

    t = jnp.maximum(_dot(u_ref[...], w1_ref[...]), 0.0)
    acc_ref[...] += _dot((t * t).astype(BF16), w2_ref[...])

    @pl.when(j == pl.num_programs(1) - 1)
    def _():
        y = acc_ref[...]
        if final_norm:
            ms = jnp.mean(y * y, axis=-1, keepdims=True)
            y = (y * lax.rsqrt(ms + EPS)) * gf_ref[...]
        o_ref[...] = y


def _mix_mlp(h2d, a, b, c, wo, g, w1, w2, gf, final_norm):
    rows = h2d.shape[0]
    row_blk = lambda w: pl.BlockSpec((ROW_TILE, w), lambda i, j: (i, 0))
    vec = pl.BlockSpec((1, D_MODEL), lambda i, j: (0, 0))
    return pl.pallas_call(
        functools.partial(_mix_mlp_kernel, final_norm=final_norm),
        grid=(rows // ROW_TILE, D_FF // FF_CHUNK),
        in_specs=[
            row_blk(D_MODEL), row_blk(A_W), row_blk(B_W), row_blk(C_W),
            pl.BlockSpec((D_MODEL, D_MODEL), lambda i, j: (0, 0)),
            vec,
            pl.BlockSpec((D_MODEL, FF_CHUNK), lambda i, j: (0, j)),
            pl.BlockSpec((FF_CHUNK, D_MODEL), lambda i, j: (j, 0)),
            vec,
        ],
        out_specs=row_blk(D_MODEL),
        out_shape=jax.ShapeDtypeStruct((rows, D_MODEL), F32),
        scratch_shapes=[pltpu.VMEM((ROW_TILE, D_MODEL), BF16), pltpu.VMEM((ROW_TILE, D_MODEL), F32)],
        compiler_params=_cparams(("arbitrary", "arbitrary")),
        name="mix_mlp",
    )(h2d, a, b, c, wo, g, w1, w2, gf)


def _t5_bucket(dist):
    n = jnp.maximum(dist, 0)
    max_exact = REL_BUCKETS // 2
    nf = jnp.maximum(n, 1).astype(F32)
    large = max_exact + (jnp.log(nf / max_exact) / math.log(REL_MAX_DIST / max_exact)
                         * (REL_BUCKETS - max_exact)).astype(jnp.int32)
    large = jnp.minimum(large, REL_BUCKETS - 1)
    return jnp.where(n < max_exact, n, large)


def _far_bucket_from():
    max_exact = REL_BUCKETS // 2
    n = np.arange(1, 4 * REL_MAX_DIST, dtype=np.float32)
    large = max_exact + (np.log(n / max_exact) / math.log(REL_MAX_DIST / max_exact)
                         * (REL_BUCKETS - max_exact)).astype(np.int32)
    bucket = np.where(n < max_exact, n, np.minimum(large, REL_BUCKETS - 1))
    return int(np.nonzero(bucket < REL_BUCKETS - 1)[0].max()) + 2


_FAR_BUCKET_FROM = _far_bucket_from()


def _bias_tiles(table):
    i = jnp.arange(ATT_BLK)[:, None]
    j = jnp.arange(ATT_BLK)[None, :]
    assert _FAR_BUCKET_FROM <= ATT_BLK + 1
    far = table[REL_BUCKETS - 1].astype(F32)
    table2 = (table.astype(F32) - far[None, :]).T * LOG2E
    tiles = []
    for offset in (ATT_BLK, 0):
        dist = i - j + offset
        bucket = _t5_bucket(dist)
        t = jnp.zeros((table2.shape[0], ATT_BLK, ATT_BLK), F32)
        for b in range(REL_BUCKETS):
            t = jnp.where(bucket[None] == b, table2[:, b, None, None], t)
        tiles.append(jnp.where(dist >= 0, t, NEG))
    return jnp.stack(tiles)


def _retention_constants(tp):
    nh, dk = C_HEADS, C_HEAD_DIM
    log_gamma = jnp.log1p(-jnp.exp2(-5.0 - jnp.arange(nh, dtype=F32)))
    i = jnp.arange(RET_BLK, dtype=F32)
    diff = i[:, None] - i[None, :]
    dmat = jnp.where(diff >= 0, jnp.exp(log_gamma[:, None, None] * jnp.maximum(diff, 0.0)), 0.0)
    per_lane = lambda a: jnp.repeat(a, dk, axis=-1)
    qdec = per_lane(jnp.exp(log_gamma[None, :] * (i[:, None] + 1.0)))
    kdec = per_lane(jnp.exp(log_gamma[None, :] * (RET_BLK - 1.0 - i[:, None])))
    sdec = per_lane(jnp.exp(log_gamma * RET_BLK)[None, :])
    inv = ROPE_BASE ** (-jnp.arange(0, dk, 2, dtype=F32) / dk)
    ang = jnp.arange(tp, dtype=F32)[:, None] * inv[None, :]
    cos, sin = jnp.cos(ang), jnp.sin(ang)
    cos_full = jnp.tile(jnp.concatenate([cos, cos], axis=-1), (1, nh))
    sin_full = jnp.tile(jnp.concatenate([-sin, sin], axis=-1), (1, nh))
    return cos_full, sin_full, dmat, qdec, kdec, sdec


def _split_w_in(w):
    parts, acc = [], 0
    for size in IN_SIZES:
        parts.append(w[:, acc:acc + size])
        acc += size
    return parts


def kernel(x, meta, rel_bias, w_in, norm_mix, diff_lambda, diff_norm, ret_norm, w_out, norm_ff, w_ff1, w_ff2,
           final_norm):
    bsz, s_len, _ = x.shape
    depth = w_in.shape[0]
    t_len = s_len + N_META
    tp = -(-t_len // ATT_BLK) * ATT_BLK
    topk = min(TOPK_MAX, s_len // 4)
    nq = tp // ATT_BLK

    h = jnp.concatenate([
        jnp.broadcast_to(meta.astype(x.dtype)[None], (bsz, N_META, D_MODEL)), x,
        jnp.zeros((bsz, tp - t_len, D_MODEL), x.dtype)], axis=1).reshape(bsz * tp, D_MODEL)

    bias_a = _bias_tiles(rel_bias[:, :A_HEADS])
    bias_b = _bias_tiles(rel_bias[:, A_HEADS:])
    cos, sin, dmat, qdec, kdec, sdec = _retention_constants(tp)
    row = lambda v: v.reshape(1, -1).astype(F32)

    for l in range(depth):
        qa, ka, va, qi, ki, wi, qb, kb, vb, qc, kc, vc, gc = _split_w_in(w_in[l])
        w16 = jnp.concatenate([qa, ka, va, qi, qb, kb, vb, ki, ki], axis=1).astype(BF16)
        w32 = jnp.concatenate([qc, kc, vc, gc, wi, jnp.zeros((D_MODEL, LANES - IDX_HEADS), F32)],
                              axis=1).astype(BF16)
        p16, p32 = _inproj(h, row(norm_mix[l]), w16, w32)
        p16 = p16.reshape(bsz, tp, P16_COLS)
        p32 = p32.reshape(bsz, tp, P32_COLS)

        scores, tau, ntake, tie = _indexer(p16, p32, topk)
        tieflag = jnp.max(tie.reshape(bsz * nq, ATT_BLK), axis=1)

        lambda_init = 0.8 - 0.6 * math.exp(-0.3 * l)
        lp = diff_lambda[l].astype(F32)
        lam = jnp.exp(jnp.sum(lp[0] * lp[1])) - jnp.exp(jnp.sum(lp[2] * lp[3])) + lambda_init
        g_pair = row(jnp.concatenate([diff_norm[l], diff_norm[l]]))
        a_out, b_out = _attention(p16, scores, tau, ntake, tieflag, lam.reshape(1), bias_a, bias_b, g_pair,
                                  1.0 - lambda_init)

        c_out = _retention(p32, cos, sin, dmat, qdec, kdec, sdec, row(ret_norm[l]))

        last = l == depth - 1
        h = _mix_mlp(h, a_out.reshape(bsz * tp, A_W), b_out.reshape(bsz * tp, B_W), c_out.reshape(bsz * tp, C_W),
                     w_out[l].astype(BF16), row(norm_ff[l]), w_ff1[l].astype(BF16), w_ff2[l].astype(BF16),
                     row(final_norm), final_norm=last)

    return h.reshape(bsz, tp, D_MODEL)[:, N_META:t_len]
```

```python
import functools
import math

import numpy as np
import jax
import jax.numpy as jnp
from jax import lax
from jax.experimental import pallas as pl
from jax.experimental.pallas import tpu as pltpu

D_MODEL = 1024
N_META = 16
A_HEADS = 8
A_HEAD_DIM = 64
A_W = A_HEADS * A_HEAD_DIM
IDX_HEADS = 8
IDX_DIM = 64
TOPK_MAX = 256
B_HEADS = 4
B_HEAD_DIM = 64
B_HALF = 32
B_W = B_HEADS * B_HEAD_DIM
C_HEADS = 4
C_HEAD_DIM = 64
C_W = C_HEADS * C_HEAD_DIM
ROPE_BASE = 10000.0
REL_BUCKETS = 32
REL_MAX_DIST = 128
D_FF = 4 * D_MODEL
EPS = 1e-6
IN_SIZES = (A_W, A_W, A_W, IDX_HEADS * IDX_DIM, IDX_DIM, IDX_HEADS, B_W, B_W, B_W, C_W, C_W, C_W, C_W)

LANES = 128
ATT_BLK = 512
IDX_ROWS = 256
IDX_CHUNK = 512
COUNT_ROWS = 128
BISECT_FIXED_STEPS = 16
SMALLEST_NORMAL = float(np.finfo(np.float32).tiny)
RET_BLK = 256
ROW_TILE = 512
FF_CHUNK = 1024
VMEM_LIMIT_BYTES = 56 * 1024 * 1024

NEG = -1e30
LOG2E = math.log2(math.e)

P16_COLS = 4 * A_W + 3 * B_W + 2 * IDX_DIM
P32_COLS = 4 * C_W + LANES

BF16 = jnp.bfloat16
F32 = jnp.float32


def _cparams(sem):
    return pltpu.CompilerParams(dimension_semantics=sem, vmem_limit_bytes=VMEM_LIMIT_BYTES)


def _dot_nt(a, b):
    return lax.dot_general(a, b, (((1,), (1,)), ((), ())), preferred_element_type=F32)


def _dot(a, b):
    return jnp.dot(a, b, preferred_element_type=F32)


def _inproj_kernel(h_ref, g_ref, w16_ref, w32_ref, o16_ref, o32_ref):
    x = h_ref[...]
    ms = jnp.mean(x * x, axis=-1, keepdims=True)
    u = ((x * lax.rsqrt(ms + EPS)) * g_ref[...]).astype(BF16)
    o16_ref[...] = _dot(u, w16_ref[...]).astype(BF16)
    o32_ref[...] = _dot(u, w32_ref[...])


def _inproj(h2d, g, w16, w32):
    rows = h2d.shape[0]
    return pl.pallas_call(
        _inproj_kernel,
        grid=(rows // ROW_TILE,),
        in_specs=[
            pl.BlockSpec((ROW_TILE, D_MODEL), lambda i: (i, 0)),
            pl.BlockSpec((1, D_MODEL), lambda i: (0, 0)),
            pl.BlockSpec((D_MODEL, P16_COLS), lambda i: (0, 0)),
            pl.BlockSpec((D_MODEL, P32_COLS), lambda i: (0, 0)),
        ],
        out_specs=[
            pl.BlockSpec((ROW_TILE, P16_COLS), lambda i: (i, 0)),
            pl.BlockSpec((ROW_TILE, P32_COLS), lambda i: (i, 0)),
        ],
        out_shape=[
            jax.ShapeDtypeStruct((rows, P16_COLS), BF16),
            jax.ShapeDtypeStruct((rows, P32_COLS), F32),
        ],
        compiler_params=_cparams(("arbitrary",)),
        name="inproj",
    )(h2d, g, w16, w32)


def _indexer_kernel(qi_ref, ki_ref, wi_ref, sc_ref, tau_ref, ntake_ref, tie_ref, qm_ref, *, topk, n_chunks):
    i = pl.program_id(1)
    row0 = i * IDX_ROWS
    n_live = (row0 + IDX_ROWS - 1) // IDX_CHUNK + 1
    n_tiles = IDX_CHUNK // LANES
    k_f = float(topk)

    lane = lax.broadcasted_iota(jnp.int32, (IDX_ROWS, LANES), 1)
    for h in range(IDX_HEADS):
        pair = qi_ref[:, (h // 2) * LANES:(h // 2 + 1) * LANES]
        keep = (lane >= IDX_DIM) if h % 2 else (lane < IDX_DIM)
        qm_ref[h] = jnp.where(keep, pair, jnp.zeros_like(pair))

    w = wi_ref[...] * (IDX_HEADS ** -0.5 * IDX_DIM ** -0.5)
    wcol = [w[:, h:h + 1] for h in range(IDX_HEADS)]
    row = row0 + lax.broadcasted_iota(jnp.int32, (IDX_ROWS, 1), 0)
    col_in_chunk = lax.broadcasted_iota(jnp.int32, (IDX_ROWS, IDX_CHUNK), 1)

    def score_chunk(c, carry):
        lo, hi = carry
        start = pl.multiple_of(c * IDX_CHUNK, IDX_CHUNK)
        kc = ki_ref[pl.ds(start, IDX_CHUNK), :]
        acc = jnp.zeros((IDX_ROWS, IDX_CHUNK), F32)
        for h in range(IDX_HEADS):
            acc = acc + wcol[h] * jnp.maximum(_dot_nt(qm_ref[h], kc), 0.0)
        causal = col_in_chunk + start <= row
        sc = jnp.where(causal, acc, -jnp.inf)
        sc_ref[:, pl.ds(start, IDX_CHUNK)] = sc
        for_min = jnp.where(causal, acc, jnp.inf)
        for t in range(n_tiles):
            lo = jnp.minimum(lo, for_min[:, t * LANES:(t + 1) * LANES])
            hi = jnp.maximum(hi, sc[:, t * LANES:(t + 1) * LANES])
        return lo, hi

    lo_l, hi_l = lax.fori_loop(0, n_live, score_chunk, (jnp.full((IDX_ROWS, LANES), jnp.inf, F32),
                                                        jnp.full((IDX_ROWS, LANES), -jnp.inf, F32)))
    across = lambda col: jnp.broadcast_to(col, (IDX_ROWS, LANES))
    lo0 = across(jnp.min(lo_l, axis=1, keepdims=True))
    hi0 = across(jnp.max(hi_l, axis=1, keepdims=True))

    def fill_chunk(c, carry):
        start = pl.multiple_of(c * IDX_CHUNK, IDX_CHUNK)
        sc_ref[:, pl.ds(start, IDX_CHUNK)] = jnp.full((IDX_ROWS, IDX_CHUNK), -jnp.inf, F32)
        return carry

    lax.fori_loop(n_live, n_chunks, fill_chunk, 0)

    def count(thr, strict=False):
        parts = []
        for r0 in range(0, IDX_ROWS, COUNT_ROWS):
            thr_r = thr[r0:r0 + COUNT_ROWS]

            def body(c, acc, r0=r0, thr_r=thr_r):
                start = pl.multiple_of(c * IDX_CHUNK, IDX_CHUNK)
                sc = sc_ref[r0:r0 + COUNT_ROWS, pl.ds(start, IDX_CHUNK)]
                for t in range(n_tiles):
                    tile = sc[:, t * LANES:(t + 1) * LANES]
                    acc = jnp.where((tile > thr_r) if strict else (tile >= thr_r), acc + 1.0, acc)
                return acc

            acc = lax.fori_loop(0, n_live, body, jnp.zeros((COUNT_ROWS, LANES), F32))
            parts.append(jnp.broadcast_to(jnp.sum(acc, axis=1, keepdims=True), (COUNT_ROWS, LANES)))
        return jnp.concatenate(parts, axis=0)

    n_valid = (row0 + 1 + lax.broadcasted_iota(jnp.int32, (IDX_ROWS, LANES), 0)).astype(F32)
    keeps_all = n_valid <= k_f
    tied_at_max = count(hi0) >= k_f
    lo0 = jnp.where(keeps_all, jnp.float32(jnp.finfo(jnp.float32).min), jnp.where(tied_at_max, hi0, lo0))
    hi0 = jnp.where(keeps_all, lo0, hi0)

    def probe(lo, hi):
        mid = 0.5 * lo + 0.5 * hi
        mid = jnp.where((lo <= 0.0) & (hi > SMALLEST_NORMAL), SMALLEST_NORMAL, mid)
        mid = jnp.where((lo < 0.0) & (hi > 0.0) & (hi <= SMALLEST_NORMAL), 0.0, mid)
        return mid, (mid > lo) & (mid < hi)

    def unfinished(state):
        return jnp.sum(jnp.where(probe(*state)[1], 1.0, 0.0)) > 0.0

    def bisect(state):
        lo, hi = state
        mid, inside = probe(lo, hi)
        cnt = count(mid)
        hit = cnt == k_f
        lo = jnp.where(inside & (hit | (cnt > k_f)), mid, lo)
        hi = jnp.where(inside & (hit | (cnt < k_f)), mid, hi)
        return lo, hi

    state = lax.fori_loop(0, BISECT_FIXED_STEPS, lambda _, st: bisect(st), (lo0, hi0))
    tau, hi = lax.while_loop(unfinished, bisect, state)
    tau_ref[...] = tau[:, :1]
    ntake_ref[...] = jnp.zeros(ntake_ref.shape, F32)
    tie_ref[...] = jnp.zeros(tie_ref.shape, jnp.int32)

    may_tie = jnp.where((tau != hi) | tied_at_max, 1.0, 0.0)

    @pl.when(jnp.sum(may_tie) > 0.0)
    def _():
        n_gt = count(tau, strict=True)
        n_ge = count(tau)
        ntake_ref[...] = k_f - n_gt[:, :1]
        tie_ref[...] = (n_ge[:, :1] > k_f).astype(jnp.int32)


def _indexer(p16, p32, topk):
    bsz, tp, _ = p16.shape
    n_chunks = tp // IDX_CHUNK
    kern = functools.partial(_indexer_kernel, topk=topk, n_chunks=n_chunks)
    col = lambda n: pl.BlockSpec((None, IDX_ROWS, 1), lambda b, i: (b, i, 0))
    return pl.pallas_call(
        kern,
        grid=(bsz, tp // IDX_ROWS),
        in_specs=[
            pl.BlockSpec((None, IDX_ROWS, A_W), lambda b, i: (b, i, 3)),
            pl.BlockSpec((None, tp, LANES), lambda b, i: (b, 0, (P16_COLS - LANES) // LANES)),
            pl.BlockSpec((None, IDX_ROWS, LANES), lambda b, i: (b, i, 4 * C_W // LANES)),
        ],
        out_specs=[
            pl.BlockSpec((None, IDX_ROWS, tp), lambda b, i: (b, i, 0)),
            col(0), col(1), col(2),
        ],
        out_shape=[
            jax.ShapeDtypeStruct((bsz, tp, tp), F32),
            jax.ShapeDtypeStruct((bsz, tp, 1), F32),
            jax.ShapeDtypeStruct((bsz, tp, 1), F32),
            jax.ShapeDtypeStruct((bsz, tp, 1), jnp.int32),
        ],
        scratch_shapes=[pltpu.VMEM((IDX_HEADS, IDX_ROWS, LANES), BF16)],
        compiler_params=_cparams(("arbitrary", "arbitrary")),
        name="indexer",
    )(p16, p16, p32)


def _lane_masked_queries(q_ref, qm_ref, n_slots, width, scale2):
    per_group = LANES // width
    lane = lax.broadcasted_iota(jnp.int32, (ATT_BLK, LANES), 1)
    for s in range(n_slots):
        g, r = divmod(s, per_group)
        grp = q_ref[:, g * LANES:(g + 1) * LANES].astype(F32) * scale2
        keep = (lane >= r * width) & (lane < (r + 1) * width)
        qm_ref[s] = jnp.where(keep, grp, 0.0).astype(BF16)


def _flash_update(s, logit2, v_ones, m_ref, acc_ref):
    m_prev = m_ref[s]
    m_new = jnp.maximum(m_prev, jnp.max(logit2, axis=1, keepdims=True))
    alpha = jnp.exp2(m_prev - m_new)
    p = jnp.concatenate([jnp.exp2(logit2[:, t * LANES:(t + 1) * LANES] - m_new)
                         for t in range(logit2.shape[1] // LANES)], axis=1).astype(BF16)
    acc_ref[s] = jnp.concatenate([alpha, alpha], axis=1) * acc_ref[s] + _dot(p, v_ones)
    m_ref[s] = m_new


def _with_ones(v_grp):
    return jnp.concatenate([v_grp, jnp.ones(v_grp.shape, v_grp.dtype)], axis=1)


def _flash_init(m_ref, acc_ref):
    m_ref[...] = jnp.full(m_ref.shape, NEG, F32)
    acc_ref[...] = jnp.zeros(acc_ref.shape, F32)


def _flash_result(s, acc_ref):
    acc = acc_ref[s]
    return acc[:, :LANES] / acc[:, LANES:]


def _triangle_steps(nq):
    qs, ks, kinds = [], [], []
    for q in range(nq):
        for k in range(q + 1):
            qs.append(q)
            ks.append(k)
            kinds.append(2 if k == q else (1 if k == q - 1 else 0))
    return (jnp.asarray(qs, jnp.int32), jnp.asarray(ks, jnp.int32), jnp.asarray(kinds, jnp.int32))


def _sparse_attn_steps(has_ties, q_ref, k_ref, v_ref, sc_ref, tau_ref, ntake_ref, bias_ref,
                       o_ref, qm_ref, m_ref, acc_ref, mask_ref, carry_ref):
    def init():
        _lane_masked_queries(q_ref, qm_ref, A_HEADS, A_HEAD_DIM, A_HEAD_DIM ** -0.5 * LOG2E)
        _flash_init(m_ref, acc_ref)
        carry_ref[...] = jnp.zeros(carry_ref.shape, F32)

    def select():
        @pl.when(has_ties == 0)
        def _():
            mask_ref[...] = jnp.where(sc_ref[...] >= tau_ref[...], 0.0, NEG)

        @pl.when(has_ties != 0)
        def _():
            sc = sc_ref[...]
            tau = tau_ref[...]
            eq = sc == tau
            eq_b = jnp.where(eq, 1.0, 0.0).astype(BF16)
            r = lax.broadcasted_iota(jnp.int32, (ATT_BLK, ATT_BLK), 0)
            c = lax.broadcasted_iota(jnp.int32, (ATT_BLK, ATT_BLK), 1)
            before = jnp.where(r < c, 1.0, 0.0).astype(BF16)
            seen = carry_ref[...] + _dot(eq_b, before)
            keep = (sc > tau) | (eq & (seen < ntake_ref[...]))
            mask_ref[...] = jnp.where(keep, 0.0, NEG)
            carry_ref[...] = carry_ref[...] + jnp.sum(eq_b.astype(F32), axis=1, keepdims=True)

    def attend(near_diagonal):
        for g in range(A_HEADS // 2):
            k_grp = k_ref[:, g * LANES:(g + 1) * LANES]
            v_ones = _with_ones(v_ref[:, g * LANES:(g + 1) * LANES])
            for h in (2 * g, 2 * g + 1):
                logit2 = _dot_nt(qm_ref[h], k_grp) + mask_ref[...]
                if near_diagonal:
                    logit2 = logit2 + bias_ref[h]
                _flash_update(h, logit2, v_ones, m_ref, acc_ref)

    def finish():
        lane = lax.broadcasted_iota(jnp.int32, (ATT_BLK, LANES), 1)
        for g in range(A_HEADS // 2):
            out = jnp.where(lane < A_HEAD_DIM, _flash_result(2 * g, acc_ref), _flash_result(2 * g + 1, acc_ref))
            o_ref[:, g * LANES:(g + 1) * LANES] = out.astype(o_ref.dtype)

    return init, select, attend, finish


def _diff_attn_steps(lam_ref, q_ref, k_ref, v_ref, bias_ref, g_ref, o_ref, qm_ref, m_ref, acc_ref, out_scale):
    n_slots = 2 * B_HEADS

    def init():
        _lane_masked_queries(q_ref, qm_ref, n_slots, B_HALF, B_HALF ** -0.5 * LOG2E)
        _flash_init(m_ref, acc_ref)

    def attend(near_diagonal):
        for g in range(B_HEADS // 2):
            k_grp = k_ref[:, g * LANES:(g + 1) * LANES]
            v_ones = _with_ones(v_ref[:, g * LANES:(g + 1) * LANES])
            for slot in range(4 * g, 4 * g + 4):
                logit2 = _dot_nt(qm_ref[slot], k_grp)
                if near_diagonal:
                    logit2 = logit2 + bias_ref[slot // 2]
                _flash_update(slot, logit2, v_ones, m_ref, acc_ref)

    def finish():
        lam = lam_ref[0]
        lane = lax.broadcasted_iota(jnp.int32, (ATT_BLK, LANES), 1)
        low = lane < B_HEAD_DIM
        for g in range(B_HEADS // 2):
            heads = []
            for h in (2 * g, 2 * g + 1):
                heads.append(_flash_result(2 * h, acc_ref) - lam * _flash_result(2 * h + 1, acc_ref))
            x = jnp.where(low, heads[0], heads[1])
            sq = x * x
            ss_lo = jnp.sum(jnp.where(low, sq, 0.0), axis=1, keepdims=True)
            ss_hi = jnp.sum(jnp.where(low, 0.0, sq), axis=1, keepdims=True)
            ms = jnp.where(low, ss_lo, ss_hi) * (1.0 / B_HEAD_DIM)
            y = (x * lax.rsqrt(ms + EPS)) * g_ref[...]
            o_ref[:, g * LANES:(g + 1) * LANES] = (y * out_scale).astype(o_ref.dtype)

    return init, attend, finish


def _attention_kernel(qmap, kmap, kind, tieflag, lam_ref,
                      qa_ref, ka_ref, va_ref, sc_ref, tau_ref, ntake_ref, bias_a_ref,
                      qb_ref, kb_ref, vb_ref, bias_b_ref, gb_ref,
                      oa_ref, ob_ref,
                      qma_ref, ma_ref, acca_ref, mask_ref, carry_ref, qmb_ref, mb_ref, accb_ref, *, nq, out_scale):
    b = pl.program_id(0)
    s = pl.program_id(1)
    q_blk = qmap[s]
    k_blk = kmap[s]
    init_a, select_a, attend_a, finish_a = _sparse_attn_steps(
        tieflag[b * nq + q_blk], qa_ref, ka_ref, va_ref, sc_ref, tau_ref, ntake_ref, bias_a_ref,
        oa_ref, qma_ref, ma_ref, acca_ref, mask_ref, carry_ref)
    init_b, attend_b, finish_b = _diff_attn_steps(
        lam_ref, qb_ref, kb_ref, vb_ref, bias_b_ref, gb_ref, ob_ref, qmb_ref, mb_ref, accb_ref, out_scale)

    @pl.when(k_blk == 0)
    def _():
        init_a()
        init_b()

    select_a()

    @pl.when(kind[s] == 0)
    def _():
        attend_a(False)
        attend_b(False)

    @pl.when(kind[s] != 0)
    def _():
        attend_a(True)
        attend_b(True)

    @pl.when(k_blk == q_blk)
    def _():
        finish_a()
        finish_b()


def _attention(p16, scores, tau, ntake, tieflag, lam, bias_a, bias_b, g_pair, out_scale):
    bsz, tp, _ = p16.shape
    nq = tp // ATT_BLK
    qmap, kmap, kind = _triangle_steps(nq)
    base_b = 4 * A_W // B_W
    at_q = lambda w, c: pl.BlockSpec((None, ATT_BLK, w), lambda b, s, qm, km, kd, tf, lm: (b, qm[s], c))
    at_k = lambda w, c: pl.BlockSpec((None, ATT_BLK, w), lambda b, s, qm, km, kd, tf, lm: (b, km[s], c))
    bias = lambda heads: pl.BlockSpec((None, heads, ATT_BLK, ATT_BLK),
                                      lambda b, s, qm, km, kd, tf, lm: (jnp.maximum(kd[s] - 1, 0), 0, 0, 0))
    n_slots = 2 * B_HEADS
    grid_spec = pltpu.PrefetchScalarGridSpec(
        num_scalar_prefetch=5,
        grid=(bsz, int(qmap.shape[0])),
        in_specs=[
            at_q(A_W, 0), at_k(A_W, 1), at_k(A_W, 2),
            pl.BlockSpec((None, ATT_BLK, ATT_BLK), lambda b, s, qm, km, kd, tf, lm: (b, qm[s], km[s])),
            at_q(1, 0), at_q(1, 0),
            bias(A_HEADS),
            at_q(B_W, base_b), at_k(B_W, base_b + 1), at_k(B_W, base_b + 2),
            bias(B_HEADS),
            pl.BlockSpec((1, LANES), lambda b, s, qm, km, kd, tf, lm: (0, 0)),
        ],
        out_specs=[at_q(A_W, 0), at_q(B_W, 0)],
        scratch_shapes=[
            pltpu.VMEM((A_HEADS, ATT_BLK, LANES), BF16),
            pltpu.VMEM((A_HEADS, ATT_BLK, LANES), F32),
            pltpu.VMEM((A_HEADS, ATT_BLK, 2 * LANES), F32),
            pltpu.VMEM((ATT_BLK, ATT_BLK), F32),
            pltpu.VMEM((ATT_BLK, 1), F32),
            pltpu.VMEM((n_slots, ATT_BLK, LANES), BF16),
            pltpu.VMEM((n_slots, ATT_BLK, LANES), F32),
            pltpu.VMEM((n_slots, ATT_BLK, 2 * LANES), F32),
        ],
    )
    return pl.pallas_call(
        functools.partial(_attention_kernel, nq=nq, out_scale=out_scale),
        grid_spec=grid_spec,
        out_shape=[jax.ShapeDtypeStruct((bsz, tp, A_W), BF16), jax.ShapeDtypeStruct((bsz, tp, B_W), BF16)],
        compiler_params=_cparams(("arbitrary", "arbitrary")),
        name="attention",
    )(qmap, kmap, kind, tieflag, lam, p16, p16, p16, scores, tau, ntake, bias_a, p16, p16, p16, bias_b, g_pair)


def _retention_kernel(q_ref, k_ref, v_ref, gate_ref, cos_ref, sin_ref, dmat_ref, qdec_ref, kdec_ref, sdec_ref,
                      g_ref, o_ref, state_ref):
    c = pl.program_id(1)

    @pl.when(c == 0)
    def _():
        state_ref[...] = jnp.zeros(state_ref.shape, F32)

    lane = lax.broadcasted_iota(jnp.int32, (RET_BLK, C_W), 1)
    first_half = (lane % C_HEAD_DIM) < (C_HEAD_DIM // 2)

    def rope(x):
        swapped = jnp.where(first_half,
                            pltpu.roll(x, C_W - C_HEAD_DIM // 2, axis=1),
                            pltpu.roll(x, C_HEAD_DIM // 2, axis=1))
        return x * cos_ref[...] + swapped * sin_ref[...]

    q = rope(q_ref[...])
    k = rope(k_ref[...]) * (C_HEAD_DIM ** -0.5)
    q16 = q.astype(BF16)
    k16 = k.astype(BF16)
    v16 = v_ref[...].astype(BF16)

    state = state_ref[...]
    out = _dot(q16, state.astype(BF16)) * qdec_ref[...]
    for h in range(C_HEADS):
        in_head = (lane >= h * C_HEAD_DIM) & (lane < (h + 1) * C_HEAD_DIM)
        qh = jnp.where(in_head, q, 0.0).astype(BF16)
        inner = _dot_nt(qh, k16) * dmat_ref[h]
        out = out + jnp.where(in_head, _dot(inner.astype(BF16), v16), 0.0)

    kd = (k * kdec_ref[...]).astype(BF16)
    update = _dot(kd.T, v16)
    r = lax.broadcasted_iota(jnp.int32, (C_W, C_W), 0) // C_HEAD_DIM
    cc = lax.broadcasted_iota(jnp.int32, (C_W, C_W), 1) // C_HEAD_DIM
    state_ref[...] = state * sdec_ref[...] + jnp.where(r == cc, update, 0.0)

    sq = out * out
    ms = jnp.zeros_like(out)
    for h in range(C_HEADS):
        in_head = (lane >= h * C_HEAD_DIM) & (lane < (h + 1) * C_HEAD_DIM)
        ss = jnp.sum(jnp.where(in_head, sq, 0.0), axis=1, keepdims=True)
        ms = jnp.where(in_head, ss * (1.0 / C_HEAD_DIM), ms)
    normed = (out * lax.rsqrt(ms + EPS)) * g_ref[...]
    gate = gate_ref[...]
    o_ref[...] = ((gate * jax.nn.sigmoid(gate)) * normed).astype(o_ref.dtype)


def _retention(p32, cos, sin, dmat, qdec, kdec, sdec, g):
    bsz, tp, _ = p32.shape
    blk = lambda c: pl.BlockSpec((None, RET_BLK, C_W), lambda b, i: (b, i, c))
    const = lambda shape: pl.BlockSpec(shape, lambda b, i: (0,) * len(shape))
    return pl.pallas_call(
        _retention_kernel,
        grid=(bsz, tp // RET_BLK),
        in_specs=[
            blk(0), blk(1), blk(2), blk(3),
            pl.BlockSpec((RET_BLK, C_W), lambda b, i: (i, 0)),
            pl.BlockSpec((RET_BLK, C_W), lambda b, i: (i, 0)),
            const((C_HEADS, RET_BLK, RET_BLK)),
            const((RET_BLK, C_W)), const((RET_BLK, C_W)), const((1, C_W)), const((1, C_W)),
        ],
        out_specs=pl.BlockSpec((None, RET_BLK, C_W), lambda b, i: (b, i, 0)),
        out_shape=jax.ShapeDtypeStruct((bsz, tp, C_W), BF16),
        scratch_shapes=[pltpu.VMEM((C_W, C_W), F32)],
        compiler_params=_cparams(("arbitrary", "arbitrary")),
        name="retention",
    )(p32, p32, p32, p32, cos, sin, dmat, qdec, kdec, sdec, g)


def _mix_mlp_kernel(h_ref, a_ref, b_ref, c_ref, wo_ref, g_ref, w1_ref, w2_ref, gf_ref, o_ref, u_ref, acc_ref,
                    *, final_norm):
    j = pl.program_id(1)

    @pl.when(j == 0)
    def _():
        mixed = (_dot(a_ref[...], wo_ref[0:A_W, :])
                 + _dot(b_ref[...], wo_ref[A_W:A_W + B_W, :])
                 + _dot(c_ref[...], wo_ref[A_W + B_W:, :]))
        h1 = h_ref[...] + mixed
        acc_ref[...] = h1
        ms = jnp.mean(h1 * h1, axis=-1, keepdims=True)
        u_ref[...] = ((h1 * lax.rsqrt(ms + EPS)) * g_ref[...]).astype(BF16)

    t = jnp.maximum(_dot(u_ref[...], w1_ref[...]), 0.0)
    acc_ref[...] += _dot((t * t).astype(BF16), w2_ref[...])

    @pl.when(j == pl.num_programs(1) - 1)
    def _():
        y = acc_ref[...]
        if final_norm:
            ms = jnp.mean(y * y, axis=-1, keepdims=True)
            y = (y * lax.rsqrt(ms + EPS)) * gf_ref[...]
        o_ref[...] = y


def _mix_mlp(h2d, a, b, c, wo, g, w1, w2, gf, final_norm):
    rows = h2d.shape[0]
    row_blk = lambda w: pl.BlockSpec((ROW_TILE, w), lambda i, j: (i, 0))
    vec = pl.BlockSpec((1, D_MODEL), lambda i, j: (0, 0))
    return pl.pallas_call(
        functools.partial(_mix_mlp_kernel, final_norm=final_norm),
        grid=(rows // ROW_TILE, D_FF // FF_CHUNK),
        in_specs=[
            row_blk(D_MODEL), row_blk(A_W), row_blk(B_W), row_blk(C_W),
            pl.BlockSpec((D_MODEL, D_MODEL), lambda i, j: (0, 0)),
            vec,
            pl.BlockSpec((D_MODEL, FF_CHUNK), lambda i, j: (0, j)),
            pl.BlockSpec((FF_CHUNK, D_MODEL), lambda i, j: (j, 0)),
            vec,
        ],
        out_specs=row_blk(D_MODEL),
        out_shape=jax.ShapeDtypeStruct((rows, D_MODEL), F32),
        scratch_shapes=[pltpu.VMEM((ROW_TILE, D_MODEL), BF16), pltpu.VMEM((ROW_TILE, D_MODEL), F32)],
        compiler_params=_cparams(("arbitrary", "arbitrary")),
        name="mix_mlp",
    )(h2d, a, b, c, wo, g, w1, w2, gf)


def _t5_bucket(dist):
    n = jnp.maximum(dist, 0)
    max_exact = REL_BUCKETS // 2
    nf = jnp.maximum(n, 1).astype(F32)
    large = max_exact + (jnp.log(nf / max_exact) / math.log(REL_MAX_DIST / max_exact)
                         * (REL_BUCKETS - max_exact)).astype(jnp.int32)
    large = jnp.minimum(large, REL_BUCKETS - 1)
    return jnp.where(n < max_exact, n, large)


def _far_bucket_from():
    max_exact = REL_BUCKETS // 2
    n = np.arange(1, 4 * REL_MAX_DIST, dtype=np.float32)
    large = max_exact + (np.log(n / max_exact) / math.log(REL_MAX_DIST / max_exact)
                         * (REL_BUCKETS - max_exact)).astype(np.int32)
    bucket = np.where(n < max_exact, n, np.minimum(large, REL_BUCKETS - 1))
    return int(np.nonzero(bucket < REL_BUCKETS - 1)[0].max()) + 2


_FAR_BUCKET_FROM = _far_bucket_from()


def _bias_tiles(table):
    i = jnp.arange(ATT_BLK)[:, None]
    j = jnp.arange(ATT_BLK)[None, :]
    assert _FAR_BUCKET_FROM <= ATT_BLK + 1
    far = table[REL_BUCKETS - 1].astype(F32)
    table2 = (table.astype(F32) - far[None, :]).T * LOG2E
    tiles = []
    for offset in (ATT_BLK, 0):
        dist = i - j + offset
        bucket = _t5_bucket(dist)
        t = jnp.zeros((table2.shape[0], ATT_BLK, ATT_BLK), F32)
        for b in range(REL_BUCKETS):
            t = jnp.where(bucket[None] == b, table2[:, b, None, None], t)
        tiles.append(jnp.where(dist >= 0, t, NEG))
    return jnp.stack(tiles)


def _retention_constants(tp):
    nh, dk = C_HEADS, C_HEAD_DIM
    log_gamma = jnp.log1p(-jnp.exp2(-5.0 - jnp.arange(nh, dtype=F32)))
    i = jnp.arange(RET_BLK, dtype=F32)
    diff = i[:, None] - i[None, :]
    dmat = jnp.where(diff >= 0, jnp.exp(log_gamma[:, None, None] * jnp.maximum(diff, 0.0)), 0.0)
    per_lane = lambda a: jnp.repeat(a, dk, axis=-1)
    qdec = per_lane(jnp.exp(log_gamma[None, :] * (i[:, None] + 1.0)))
    kdec = per_lane(jnp.exp(log_gamma[None, :] * (RET_BLK - 1.0 - i[:, None])))
    sdec = per_lane(jnp.exp(log_gamma * RET_BLK)[None, :])
    inv = ROPE_BASE ** (-jnp.arange(0, dk, 2, dtype=F32) / dk)
    ang = jnp.arange(tp, dtype=F32)[:, None] * inv[None, :]
    cos, sin = jnp.cos(ang), jnp.sin(ang)
    cos_full = jnp.tile(jnp.concatenate([cos, cos], axis=-1), (1, nh))
    sin_full = jnp.tile(jnp.concatenate([-sin, sin], axis=-1), (1, nh))
    return cos_full, sin_full, dmat, qdec, kdec, sdec


def _split_w_in(w):
    parts, acc = [], 0
    for size in IN_SIZES:
        parts.append(w[:, acc:acc + size])
        acc += size
    return parts


def kernel(x, meta, rel_bias, w_in, norm_mix, diff_lambda, diff_norm, ret_norm, w_out, norm_ff, w_ff1, w_ff2,
           final_norm):
    bsz, s_len, _ = x.shape
    depth = w_in.shape[0]
    t_len = s_len + N_META
    tp = -(-t_len // ATT_BLK) * ATT_BLK
    topk = min(TOPK_MAX, s_len // 4)
    nq = tp // ATT_BLK

    h = jnp.concatenate([
        jnp.broadcast_to(meta.astype(x.dtype)[None], (bsz, N_META, D_MODEL)), x,
        jnp.zeros((bsz, tp - t_len, D_MODEL), x.dtype)], axis=1).reshape(bsz * tp, D_MODEL)

    bias_a = _bias_tiles(rel_bias[:, :A_HEADS])
    bias_b = _bias_tiles(rel_bias[:, A_HEADS:])
    cos, sin, dmat, qdec, kdec, sdec = _retention_constants(tp)
    row = lambda v: v.reshape(1, -1).astype(F32)

    for l in range(depth):
        qa, ka, va, qi, ki, wi, qb, kb, vb, qc, kc, vc, gc = _split_w_in(w_in[l])
        w16 = jnp.concatenate([qa, ka, va, qi, qb, kb, vb, ki, ki], axis=1).astype(BF16)
        w32 = jnp.concatenate([qc, kc, vc, gc, wi, jnp.zeros((D_MODEL, LANES - IDX_HEADS), F32)],
                              axis=1).astype(BF16)
        p16, p32 = _inproj(h, row(norm_mix[l]), w16, w32)
        p16 = p16.reshape(bsz, tp, P16_COLS)
        p32 = p32.reshape(bsz, tp, P32_COLS)

        scores, tau, ntake, tie = _indexer(p16, p32, topk)
        tieflag = jnp.max(tie.reshape(bsz * nq, ATT_BLK), axis=1)

        lambda_init = 0.8 - 0.6 * math.exp(-0.3 * l)
        lp = diff_lambda[l].astype(F32)
        lam = jnp.exp(jnp.sum(lp[0] * lp[1])) - jnp.exp(jnp.sum(lp[2] * lp[3])) + lambda_init
        g_pair = row(jnp.concatenate([diff_norm[l], diff_norm[l]]))
        a_out, b_out = _attention(p16, scores, tau, ntake, tieflag, lam.reshape(1), bias_a, bias_b, g_pair,
                                  1.0 - lambda_init)

        c_out = _retention(p32, cos, sin, dmat, qdec, kdec, sdec, row(ret_norm[l]))

        last = l == depth - 1
        h = _mix_mlp(h, a_out.reshape(bsz * tp, A_W), b_out.reshape(bsz * tp, B_W), c_out.reshape(bsz * tp, C_W),
                     w_out[l].astype(BF16), row(norm_ff[l]), w_ff1[l].astype(BF16), w_ff2[l].astype(BF16),
                     row(final_norm), final_norm=last)

    return h.reshape(bsz, tp, D_MODEL)[:, N_META:t_len]
```

```python
import functools
import math

import numpy as np
import jax
import jax.numpy as jnp
from jax import lax
from jax.experimental import pallas as pl
from jax.experimental.pallas import tpu as pltpu

D_MODEL = 1024
N_META = 16
A_HEADS = 8
A_HEAD_DIM = 64
A_W = A_HEADS * A_HEAD_DIM
IDX_HEADS = 8
IDX_DIM = 64
TOPK_MAX = 256
B_HEADS = 4
B_HEAD_DIM = 64
B_HALF = 32
B_W = B_HEADS * B_HEAD_DIM
C_HEADS = 4
C_HEAD_DIM = 64
C_W = C_HEADS * C_HEAD_DIM
ROPE_BASE = 10000.0
REL_BUCKETS = 32
REL_MAX_DIST = 128
D_FF = 4 * D_MODEL
EPS = 1e-6
IN_SIZES = (A_W, A_W, A_W, IDX_HEADS * IDX_DIM, IDX_DIM, IDX_HEADS, B_W, B_W, B_W, C_W, C_W, C_W, C_W)

LANES = 128
ATT_BLK = 512
IDX_ROWS = 256
IDX_CHUNK = 512
COUNT_ROWS = 128
BISECT_FIXED_STEPS = 18
SMALLEST_NORMAL = float(np.finfo(np.float32).tiny)
RET_BLK = 256
ROW_TILE = 512
FF_CHUNK = 1024
VMEM_LIMIT_BYTES = 56 * 1024 * 1024

NEG = -1e30
LOG2E = math.log2(math.e)

P16_COLS = 4 * A_W + 3 * B_W + 2 * IDX_DIM
P32_COLS = 4 * C_W + LANES

BF16 = jnp.bfloat16
F32 = jnp.float32


def _cparams(sem):
    return pltpu.CompilerParams(dimension_semantics=sem, vmem_limit_bytes=VMEM_LIMIT_BYTES)


def _dot_nt(a, b):
    return lax.dot_general(a, b, (((1,), (1,)), ((), ())), preferred_element_type=F32)


def _dot(a, b):
    return jnp.dot(a, b, preferred_element_type=F32)


def _inproj_kernel(h_ref, g_ref, w16_ref, w32_ref, o16_ref, o32_ref):
    x = h_ref[...]
    ms = jnp.mean(x * x, axis=-1, keepdims=True)
    u = ((x * lax.rsqrt(ms + EPS)) * g_ref[...]).astype(BF16)
    o16_ref[...] = _dot(u, w16_ref[...]).astype(BF16)
    o32_ref[...] = _dot(u, w32_ref[...])


def _inproj(h2d, g, w16, w32):
    rows = h2d.shape[0]
    return pl.pallas_call(
        _inproj_kernel,
        grid=(rows // ROW_TILE,),
        in_specs=[
            pl.BlockSpec((ROW_TILE, D_MODEL), lambda i: (i, 0)),
            pl.BlockSpec((1, D_MODEL), lambda i: (0, 0)),
            pl.BlockSpec((D_MODEL, P16_COLS), lambda i: (0, 0)),
            pl.BlockSpec((D_MODEL, P32_COLS), lambda i: (0, 0)),
        ],
        out_specs=[
            pl.BlockSpec((ROW_TILE, P16_COLS), lambda i: (i, 0)),
            pl.BlockSpec((ROW_TILE, P32_COLS), lambda i: (i, 0)),
        ],
        out_shape=[
            jax.ShapeDtypeStruct((rows, P16_COLS), BF16),
            jax.ShapeDtypeStruct((rows, P32_COLS), F32),
        ],
        compiler_params=_cparams(("arbitrary",)),
        name="inproj",
    )(h2d, g, w16, w32)


def _indexer_kernel(qi_ref, ki_ref, wi_ref, sc_ref, tau_ref, ntake_ref, tie_ref, qm_ref, *, topk, n_chunks):
    i = pl.program_id(1)
    row0 = i * IDX_ROWS
    n_live = (row0 + IDX_ROWS - 1) // IDX_CHUNK + 1
    n_tiles = IDX_CHUNK // LANES
    k_f = float(topk)

    lane = lax.broadcasted_iota(jnp.int32, (IDX_ROWS, LANES), 1)
    for h in range(IDX_HEADS):
        pair = qi_ref[:, (h // 2) * LANES:(h // 2 + 1) * LANES]
        keep = (lane >= IDX_DIM) if h % 2 else (lane < IDX_DIM)
        qm_ref[h] = jnp.where(keep, pair, jnp.zeros_like(pair))

    w = wi_ref[...] * (IDX_HEADS ** -0.5 * IDX_DIM ** -0.5)
    wcol = [w[:, h:h + 1] for h in range(IDX_HEADS)]
    row = row0 + lax.broadcasted_iota(jnp.int32, (IDX_ROWS, 1), 0)
    col_in_chunk = lax.broadcasted_iota(jnp.int32, (IDX_ROWS, IDX_CHUNK), 1)

    def score_chunk(c, carry):
        lo, hi = carry
        start = pl.multiple_of(c * IDX_CHUNK, IDX_CHUNK)
        kc = ki_ref[pl.ds(start, IDX_CHUNK), :]
        acc = jnp.zeros((IDX_ROWS, IDX_CHUNK), F32)
        for h in range(IDX_HEADS):
            acc = acc + wcol[h] * jnp.maximum(_dot_nt(qm_ref[h], kc), 0.0)
        causal = col_in_chunk + start <= row
        sc = jnp.where(causal, acc, -jnp.inf)
        sc_ref[:, pl.ds(start, IDX_CHUNK)] = sc
        for_min = jnp.where(causal, acc, jnp.inf)
        for t in range(n_tiles):
            lo = jnp.minimum(lo, for_min[:, t * LANES:(t + 1) * LANES])
            hi = jnp.maximum(hi, sc[:, t * LANES:(t + 1) * LANES])
        return lo, hi

    lo_l, hi_l = lax.fori_loop(0, n_live, score_chunk, (jnp.full((IDX_ROWS, LANES), jnp.inf, F32),
                                                        jnp.full((IDX_ROWS, LANES), -jnp.inf, F32)))
    across = lambda col: jnp.broadcast_to(col, (IDX_ROWS, LANES))
    lo0 = across(jnp.min(lo_l, axis=1, keepdims=True))
    hi0 = across(jnp.max(hi_l, axis=1, keepdims=True))

    def fill_chunk(c, carry):
        start = pl.multiple_of(c * IDX_CHUNK, IDX_CHUNK)
        sc_ref[:, pl.ds(start, IDX_CHUNK)] = jnp.full((IDX_ROWS, IDX_CHUNK), -jnp.inf, F32)
        return carry

    lax.fori_loop(n_live, n_chunks, fill_chunk, 0)

    def count(thr, strict=False):
        parts = []
        for r0 in range(0, IDX_ROWS, COUNT_ROWS):
            thr_r = thr[r0:r0 + COUNT_ROWS]

            def body(c, acc, r0=r0, thr_r=thr_r):
                start = pl.multiple_of(c * IDX_CHUNK, IDX_CHUNK)
                sc = sc_ref[r0:r0 + COUNT_ROWS, pl.ds(start, IDX_CHUNK)]
                for t in range(n_tiles):
                    tile = sc[:, t * LANES:(t + 1) * LANES]
                    acc = jnp.where((tile > thr_r) if strict else (tile >= thr_r), acc + 1.0, acc)
                return acc

            acc = lax.fori_loop(0, n_live, body, jnp.zeros((COUNT_ROWS, LANES), F32))
            parts.append(jnp.broadcast_to(jnp.sum(acc, axis=1, keepdims=True), (COUNT_ROWS, LANES)))
        return jnp.concatenate(parts, axis=0)

    n_valid = (row0 + 1 + lax.broadcasted_iota(jnp.int32, (IDX_ROWS, LANES), 0)).astype(F32)
    keeps_all = n_valid <= k_f
    tied_at_max = count(hi0) >= k_f
    lo0 = jnp.where(keeps_all, jnp.float32(jnp.finfo(jnp.float32).min), jnp.where(tied_at_max, hi0, lo0))
    hi0 = jnp.where(keeps_all, lo0, hi0)

    def probe(lo, hi):
        mid = 0.5 * lo + 0.5 * hi
        mid = jnp.where((lo <= 0.0) & (hi > SMALLEST_NORMAL), SMALLEST_NORMAL, mid)
        mid = jnp.where((lo < 0.0) & (hi > 0.0) & (hi <= SMALLEST_NORMAL), 0.0, mid)
        return mid, (mid > lo) & (mid < hi)

    def unfinished(state):
        return jnp.sum(jnp.where(probe(*state)[1], 1.0, 0.0)) > 0.0

    def bisect(state):
        lo, hi = state
        mid, inside = probe(lo, hi)
        cnt = count(mid)
        hit = cnt == k_f
        lo = jnp.where(inside & (hit | (cnt > k_f)), mid, lo)
        hi = jnp.where(inside & (hit | (cnt < k_f)), mid, hi)
        return lo, hi

    state = lax.fori_loop(0, BISECT_FIXED_STEPS, lambda _, st: bisect(st), (lo0, hi0))
    tau, hi = lax.while_loop(unfinished, bisect, state)
    tau_ref[...] = tau[:, :1]
    ntake_ref[...] = jnp.zeros(ntake_ref.shape, F32)
    tie_ref[...] = jnp.zeros(tie_ref.shape, jnp.int32)

    may_tie = jnp.where((tau != hi) | tied_at_max, 1.0, 0.0)

    @pl.when(jnp.sum(may_tie) > 0.0)
    def _():
        n_gt = count(tau, strict=True)
        n_ge = count(tau)
        ntake_ref[...] = k_f - n_gt[:, :1]
        tie_ref[...] = (n_ge[:, :1] > k_f).astype(jnp.int32)


def _indexer(p16, p32, topk):
    bsz, tp, _ = p16.shape
    n_chunks = tp // IDX_CHUNK
    kern = functools.partial(_indexer_kernel, topk=topk, n_chunks=n_chunks)
    col = lambda n: pl.BlockSpec((None, IDX_ROWS, 1), lambda b, i: (b, i, 0))
    return pl.pallas_call(
        kern,
        grid=(bsz, tp // IDX_ROWS),
        in_specs=[
            pl.BlockSpec((None, IDX_ROWS, A_W), lambda b, i: (b, i, 3)),
            pl.BlockSpec((None, tp, LANES), lambda b, i: (b, 0, (P16_COLS - LANES) // LANES)),
            pl.BlockSpec((None, IDX_ROWS, LANES), lambda b, i: (b, i, 4 * C_W // LANES)),
        ],
        out_specs=[
            pl.BlockSpec((None, IDX_ROWS, tp), lambda b, i: (b, i, 0)),
            col(0), col(1), col(2),
        ],
        out_shape=[
            jax.ShapeDtypeStruct((bsz, tp, tp), F32),
            jax.ShapeDtypeStruct((bsz, tp, 1), F32),
            jax.ShapeDtypeStruct((bsz, tp, 1), F32),
            jax.ShapeDtypeStruct((bsz, tp, 1), jnp.int32),
        ],
        scratch_shapes=[pltpu.VMEM((IDX_HEADS, IDX_ROWS, LANES), BF16)],
        compiler_params=_cparams(("arbitrary", "arbitrary")),
        name="indexer",
    )(p16, p16, p32)


def _lane_masked_queries(q_ref, qm_ref, n_slots, width, scale2):
    per_group = LANES // width
    lane = lax.broadcasted_iota(jnp.int32, (ATT_BLK, LANES), 1)
    for s in range(n_slots):
        g, r = divmod(s, per_group)
        grp = q_ref[:, g * LANES:(g + 1) * LANES].astype(F32) * scale2
        keep = (lane >= r * width) & (lane < (r + 1) * width)
        qm_ref[s] = jnp.where(keep, grp, 0.0).astype(BF16)


def _flash_update(s, logit2, v_ones, m_ref, acc_ref):
    m_prev = m_ref[s]
    m_new = jnp.maximum(m_prev, jnp.max(logit2, axis=1, keepdims=True))
    alpha = jnp.exp2(m_prev - m_new)
    p = jnp.concatenate([jnp.exp2(logit2[:, t * LANES:(t + 1) * LANES] - m_new)
                         for t in range(logit2.shape[1] // LANES)], axis=1).astype(BF16)
    acc_ref[s] = jnp.concatenate([alpha, alpha], axis=1) * acc_ref[s] + _dot(p, v_ones)
    m_ref[s] = m_new


def _with_ones(v_grp):
    return jnp.concatenate([v_grp, jnp.ones(v_grp.shape, v_grp.dtype)], axis=1)


def _flash_init(m_ref, acc_ref):
    m_ref[...] = jnp.full(m_ref.shape, NEG, F32)
    acc_ref[...] = jnp.zeros(acc_ref.shape, F32)


def _flash_result(s, acc_ref):
    acc = acc_ref[s]
    return acc[:, :LANES] / acc[:, LANES:]


def _triangle_steps(nq):
    qs, ks, kinds = [], [], []
    for q in range(nq):
        for k in range(q + 1):
            qs.append(q)
            ks.append(k)
            kinds.append(2 if k == q else (1 if k == q - 1 else 0))
    return (jnp.asarray(qs, jnp.int32), jnp.asarray(ks, jnp.int32), jnp.asarray(kinds, jnp.int32))


def _sparse_attn_steps(has_ties, q_ref, k_ref, v_ref, sc_ref, tau_ref, ntake_ref, bias_ref,
                       o_ref, qm_ref, m_ref, acc_ref, mask_ref, carry_ref):
    def init():
        _lane_masked_queries(q_ref, qm_ref, A_HEADS, A_HEAD_DIM, A_HEAD_DIM ** -0.5 * LOG2E)
        _flash_init(m_ref, acc_ref)
        carry_ref[...] = jnp.zeros(carry_ref.shape, F32)

    def select():
        @pl.when(has_ties == 0)
        def _():
            mask_ref[...] = jnp.where(sc_ref[...] >= tau_ref[...], 0.0, NEG)

        @pl.when(has_ties != 0)
        def _():
            sc = sc_ref[...]
            tau = tau_ref[...]
            eq = sc == tau
            eq_b = jnp.where(eq, 1.0, 0.0).astype(BF16)
            r = lax.broadcasted_iota(jnp.int32, (ATT_BLK, ATT_BLK), 0)
            c = lax.broadcasted_iota(jnp.int32, (ATT_BLK, ATT_BLK), 1)
            before = jnp.where(r < c, 1.0, 0.0).astype(BF16)
            seen = carry_ref[...] + _dot(eq_b, before)
            keep = (sc > tau) | (eq & (seen < ntake_ref[...]))
            mask_ref[...] = jnp.where(keep, 0.0, NEG)
            carry_ref[...] = carry_ref[...] + jnp.sum(eq_b.astype(F32), axis=1, keepdims=True)

    def attend(near_diagonal):
        for g in range(A_HEADS // 2):
            k_grp = k_ref[:, g * LANES:(g + 1) * LANES]
            v_ones = _with_ones(v_ref[:, g * LANES:(g + 1) * LANES])
            for h in (2 * g, 2 * g + 1):
                logit2 = _dot_nt(qm_ref[h], k_grp) + mask_ref[...]
                if near_diagonal:
                    logit2 = logit2 + bias_ref[h]
                _flash_update(h, logit2, v_ones, m_ref, acc_ref)

    def finish():
        lane = lax.broadcasted_iota(jnp.int32, (ATT_BLK, LANES), 1)
        for g in range(A_HEADS // 2):
            out = jnp.where(lane < A_HEAD_DIM, _flash_result(2 * g, acc_ref), _flash_result(2 * g + 1, acc_ref))
            o_ref[:, g * LANES:(g + 1) * LANES] = out.astype(o_ref.dtype)

    return init, select, attend, finish


def _diff_attn_steps(lam_ref, q_ref, k_ref, v_ref, bias_ref, g_ref, o_ref, qm_ref, m_ref, acc_ref, out_scale):
    n_slots = 2 * B_HEADS

    def init():
        _lane_masked_queries(q_ref, qm_ref, n_slots, B_HALF, B_HALF ** -0.5 * LOG2E)
        _flash_init(m_ref, acc_ref)

    def attend(near_diagonal):
        for g in range(B_HEADS // 2):
            k_grp = k_ref[:, g * LANES:(g + 1) * LANES]
            v_ones = _with_ones(v_ref[:, g * LANES:(g + 1) * LANES])
            for slot in range(4 * g, 4 * g + 4):
                logit2 = _dot_nt(qm_ref[slot], k_grp)
                if near_diagonal:
                    logit2 = logit2 + bias_ref[slot // 2]
                _flash_update(slot, logit2, v_ones, m_ref, acc_ref)

    def finish():
        lam = lam_ref[0]
        lane = lax.broadcasted_iota(jnp.int32, (ATT_BLK, LANES), 1)
        low = lane < B_HEAD_DIM
        for g in range(B_HEADS // 2):
            heads = []
            for h in (2 * g, 2 * g + 1):
                heads.append(_flash_result(2 * h, acc_ref) - lam * _flash_result(2 * h + 1, acc_ref))
            x = jnp.where(low, heads[0], heads[1])
            sq = x * x
            ss_lo = jnp.sum(jnp.where(low, sq, 0.0), axis=1, keepdims=True)
            ss_hi = jnp.sum(jnp.where(low, 0.0, sq), axis=1, keepdims=True)
            ms = jnp.where(low, ss_lo, ss_hi) * (1.0 / B_HEAD_DIM)
            y = (x * lax.rsqrt(ms + EPS)) * g_ref[...]
            o_ref[:, g * LANES:(g + 1) * LANES] = (y * out_scale).astype(o_ref.dtype)

    return init, attend, finish


def _attention_kernel(qmap, kmap, kind, tieflag, lam_ref,
                      qa_ref, ka_ref, va_ref, sc_ref, tau_ref, ntake_ref, bias_a_ref,
                      qb_ref, kb_ref, vb_ref, bias_b_ref, gb_ref,
                      oa_ref, ob_ref,
                      qma_ref, ma_ref, acca_ref, mask_ref, carry_ref, qmb_ref, mb_ref, accb_ref, *, nq, out_scale):
    b = pl.program_id(0)
    s = pl.program_id(1)
    q_blk = qmap[s]
    k_blk = kmap[s]
    init_a, select_a, attend_a, finish_a = _sparse_attn_steps(
        tieflag[b * nq + q_blk], qa_ref, ka_ref, va_ref, sc_ref, tau_ref, ntake_ref, bias_a_ref,
        oa_ref, qma_ref, ma_ref, acca_ref, mask_ref, carry_ref)
    init_b, attend_b, finish_b = _diff_attn_steps(
        lam_ref, qb_ref, kb_ref, vb_ref, bias_b_ref, gb_ref, ob_ref, qmb_ref, mb_ref, accb_ref, out_scale)

    @pl.when(k_blk == 0)
    def _():
        init_a()
        init_b()

    select_a()

    @pl.when(kind[s] == 0)
    def _():
        attend_a(False)
        attend_b(False)

    @pl.when(kind[s] != 0)
    def _():
        attend_a(True)
        attend_b(True)

    @pl.when(k_blk == q_blk)
    def _():
        finish_a()
        finish_b()


def _attention(p16, scores, tau, ntake, tieflag, lam, bias_a, bias_b, g_pair, out_scale):
    bsz, tp, _ = p16.shape
    nq = tp // ATT_BLK
    qmap, kmap, kind = _triangle_steps(nq)
    base_b = 4 * A_W // B_W
    at_q = lambda w, c: pl.BlockSpec((None, ATT_BLK, w), lambda b, s, qm, km, kd, tf, lm: (b, qm[s], c))
    at_k = lambda w, c: pl.BlockSpec((None, ATT_BLK, w), lambda b, s, qm, km, kd, tf, lm: (b, km[s], c))
    bias = lambda heads: pl.BlockSpec((None, heads, ATT_BLK, ATT_BLK),
                                      lambda b, s, qm, km, kd, tf, lm: (jnp.maximum(kd[s] - 1, 0), 0, 0, 0))
    n_slots = 2 * B_HEADS
    grid_spec = pltpu.PrefetchScalarGridSpec(
        num_scalar_prefetch=5,
        grid=(bsz, int(qmap.shape[0])),
        in_specs=[
            at_q(A_W, 0), at_k(A_W, 1), at_k(A_W, 2),
            pl.BlockSpec((None, ATT_BLK, ATT_BLK), lambda b, s, qm, km, kd, tf, lm: (b, qm[s], km[s])),
            at_q(1, 0), at_q(1, 0),
            bias(A_HEADS),
            at_q(B_W, base_b), at_k(B_W, base_b + 1), at_k(B_W, base_b + 2),
            bias(B_HEADS),
            pl.BlockSpec((1, LANES), lambda b, s, qm, km, kd, tf, lm: (0, 0)),
        ],
        out_specs=[at_q(A_W, 0), at_q(B_W, 0)],
        scratch_shapes=[
            pltpu.VMEM((A_HEADS, ATT_BLK, LANES), BF16),
            pltpu.VMEM((A_HEADS, ATT_BLK, LANES), F32),
            pltpu.VMEM((A_HEADS, ATT_BLK, 2 * LANES), F32),
            pltpu.VMEM((ATT_BLK, ATT_BLK), F32),
            pltpu.VMEM((ATT_BLK, 1), F32),
            pltpu.VMEM((n_slots, ATT_BLK, LANES), BF16),
            pltpu.VMEM((n_slots, ATT_BLK, LANES), F32),
            pltpu.VMEM((n_slots, ATT_BLK, 2 * LANES), F32),
        ],
    )
    return pl.pallas_call(
        functools.partial(_attention_kernel, nq=nq, out_scale=out_scale),
        grid_spec=grid_spec,
        out_shape=[jax.ShapeDtypeStruct((bsz, tp, A_W), BF16), jax.ShapeDtypeStruct((bsz, tp, B_W), BF16)],
        compiler_params=_cparams(("arbitrary", "arbitrary")),
        name="attention",
    )(qmap, kmap, kind, tieflag, lam, p16, p16, p16, scores, tau, ntake, bias_a, p16, p16, p16, bias_b, g_pair)


def _retention_kernel(q_ref, k_ref, v_ref, gate_ref, cos_ref, sin_ref, dmat_ref, qdec_ref, kdec_ref, sdec_ref,
                      g_ref, o_ref, state_ref):
    c = pl.program_id(1)

    @pl.when(c == 0)
    def _():
        state_ref[...] = jnp.zeros(state_ref.shape, F32)

    lane = lax.broadcasted_iota(jnp.int32, (RET_BLK, C_W), 1)
    first_half = (lane % C_HEAD_DIM) < (C_HEAD_DIM // 2)

    def rope(x):
        swapped = jnp.where(first_half,
                            pltpu.roll(x, C_W - C_HEAD_DIM // 2, axis=1),
                            pltpu.roll(x, C_HEAD_DIM // 2, axis=1))
        return x * cos_ref[...] + swapped * sin_ref[...]

    q = rope(q_ref[...])
    k = rope(k_ref[...]) * (C_HEAD_DIM ** -0.5)
    q16 = q.astype(BF16)
    k16 = k.astype(BF16)
    v16 = v_ref[...].astype(BF16)

    state = state_ref[...]
    out = _dot(q16, state.astype(BF16)) * qdec_ref[...]
    for h in range(C_HEADS):
        in_head = (lane >= h * C_HEAD_DIM) & (lane < (h + 1) * C_HEAD_DIM)
        qh = jnp.where(in_head, q, 0.0).astype(BF16)
        inner = _dot_nt(qh, k16) * dmat_ref[h]
        out = out + jnp.where(in_head, _dot(inner.astype(BF16), v16), 0.0)

    kd = (k * kdec_ref[...]).astype(BF16)
    update = _dot(kd.T, v16)
    r = lax.broadcasted_iota(jnp.int32, (C_W, C_W), 0) // C_HEAD_DIM
    cc = lax.broadcasted_iota(jnp.int32, (C_W, C_W), 1) // C_HEAD_DIM
    state_ref[...] = state * sdec_ref[...] + jnp.where(r == cc, update, 0.0)

    sq = out * out
    ms = jnp.zeros_like(out)
    for h in range(C_HEADS):
        in_head = (lane >= h * C_HEAD_DIM) & (lane < (h + 1) * C_HEAD_DIM)
        ss = jnp.sum(jnp.where(in_head, sq, 0.0), axis=1, keepdims=True)
        ms = jnp.where(in_head, ss * (1.0 / C_HEAD_DIM), ms)
    normed = (out * lax.rsqrt(ms + EPS)) * g_ref[...]
    gate = gate_ref[...]
    o_ref[...] = ((gate * jax.nn.sigmoid(gate)) * normed).astype(o_ref.dtype)


def _retention(p32, cos, sin, dmat, qdec, kdec, sdec, g):
    bsz, tp, _ = p32.shape
    blk = lambda c: pl.BlockSpec((None, RET_BLK, C_W), lambda b, i: (b, i, c))
    const = lambda shape: pl.BlockSpec(shape, lambda b, i: (0,) * len(shape))
    return pl.pallas_call(
        _retention_kernel,
        grid=(bsz, tp // RET_BLK),
        in_specs=[
            blk(0), blk(1), blk(2), blk(3),
            pl.BlockSpec((RET_BLK, C_W), lambda b, i: (i, 0)),
            pl.BlockSpec((RET_BLK, C_W), lambda b, i: (i, 0)),
            const((C_HEADS, RET_BLK, RET_BLK)),
            const((RET_BLK, C_W)), const((RET_BLK, C_W)), const((1, C_W)), const((1, C_W)),
        ],
        out_specs=pl.BlockSpec((None, RET_BLK, C_W), lambda b, i: (b, i, 0)),
        out_shape=jax.ShapeDtypeStruct((bsz, tp, C_W), BF16),
        scratch_shapes=[pltpu.VMEM((C_W, C_W), F32)],
        compiler_params=_cparams(("arbitrary", "arbitrary")),
        name="retention",
    )(p32, p32, p32, p32, cos, sin, dmat, qdec, kdec, sdec, g)


def _mix_mlp_kernel(h_ref, a_ref, b_ref, c_ref, wo_ref, g_ref, w1_ref, w2_ref, gf_ref, o_ref, u_ref, acc_ref,
                    *, final_norm):
    j = pl.program_id(1)

    @pl.when(j == 0)
    def _():
        mixed = (_dot(a_ref[...], wo_ref[0:A_W, :])
                 + _dot(b_ref[...], wo_ref[A_W:A_W + B_W, :])
                 + _dot(c_ref[...], wo_ref[A_W + B_W:, :]))
        h1 = h_ref[...] + mixed
        acc_ref[...] = h1
        ms = jnp.mean(h1 * h1, axis=-1, keepdims=True)
        u_ref[...] = ((h1 * lax.rsqrt(ms + EPS)) * g_ref[...]).astype(BF16)

    t = jnp.maximum(_dot(u_ref[...], w1_ref[...]), 0.0)
    acc_ref[...] += _dot((t * t).astype(BF16), w2_ref[...])

    @pl.when(j == pl.num_programs(1) - 1)
    def _():
        y = acc_ref[...]
        if final_norm:
            ms = jnp.mean(y * y, axis=-1, keepdims=True)
            y = (y * lax.rsqrt(ms + EPS)) * gf_ref[...]
        o_ref[...] = y


def _mix_mlp(h2d, a, b, c, wo, g, w1, w2, gf, final_norm):
    rows = h2d.shape[0]
    row_blk = lambda w: pl.BlockSpec((ROW_TILE, w), lambda i, j: (i, 0))
    vec = pl.BlockSpec((1, D_MODEL), lambda i, j: (0, 0))
    return pl.pallas_call(
        functools.partial(_mix_mlp_kernel, final_norm=final_norm),
        grid=(rows // ROW_TILE, D_FF // FF_CHUNK),
        in_specs=[
            row_blk(D_MODEL), row_blk(A_W), row_blk(B_W), row_blk(C_W),
            pl.BlockSpec((D_MODEL, D_MODEL), lambda i, j: (0, 0)),
            vec,
            pl.BlockSpec((D_MODEL, FF_CHUNK), lambda i, j: (0, j)),
            pl.BlockSpec((FF_CHUNK, D_MODEL), lambda i, j: (j, 0)),
            vec,
        ],
        out_specs=row_blk(D_MODEL),
        out_shape=jax.ShapeDtypeStruct((rows, D_MODEL), F32),
        scratch_shapes=[pltpu.VMEM((ROW_TILE, D_MODEL), BF16), pltpu.VMEM((ROW_TILE, D_MODEL), F32)],
        compiler_params=_cparams(("arbitrary", "arbitrary")),
        name="mix_mlp",
    )(h2d, a, b, c, wo, g, w1, w2, gf)


def _t5_bucket(dist):
    n = jnp.maximum(dist, 0)
    max_exact = REL_BUCKETS // 2
    nf = jnp.maximum(n, 1).astype(F32)
    large = max_exact + (jnp.log(nf / max_exact) / math.log(REL_MAX_DIST / max_exact)
                         * (REL_BUCKETS - max_exact)).astype(jnp.int32)
    large = jnp.minimum(large, REL_BUCKETS - 1)
    return jnp.where(n < max_exact, n, large)


def _far_bucket_from():
    max_exact = REL_BUCKETS // 2
    n = np.arange(1, 4 * REL_MAX_DIST, dtype=np.float32)
    large = max_exact + (np.log(n / max_exact) / math.log(REL_MAX_DIST / max_exact)
                         * (REL_BUCKETS - max_exact)).astype(np.int32)
    bucket = np.where(n < max_exact, n, np.minimum(large, REL_BUCKETS - 1))
    return int(np.nonzero(bucket < REL_BUCKETS - 1)[0].max()) + 2


_FAR_BUCKET_FROM = _far_bucket_from()


def _bias_tiles(table):
    i = jnp.arange(ATT_BLK)[:, None]
    j = jnp.arange(ATT_BLK)[None, :]
    assert _FAR_BUCKET_FROM <= ATT_BLK + 1
    far = table[REL_BUCKETS - 1].astype(F32)
    table2 = (table.astype(F32) - far[None, :]).T * LOG2E
    tiles = []
    for offset in (ATT_BLK, 0):
        dist = i - j + offset
        bucket = _t5_bucket(dist)
        t = jnp.zeros((table2.shape[0], ATT_BLK, ATT_BLK), F32)
        for b in range(REL_BUCKETS):
            t = jnp.where(bucket[None] == b, table2[:, b, None, None], t)
        tiles.append(jnp.where(dist >= 0, t, NEG))
    return jnp.stack(tiles)


def _retention_constants(tp):
    nh, dk = C_HEADS, C_HEAD_DIM
    log_gamma = jnp.log1p(-jnp.exp2(-5.0 - jnp.arange(nh, dtype=F32)))
    i = jnp.arange(RET_BLK, dtype=F32)
    diff = i[:, None] - i[None, :]
    dmat = jnp.where(diff >= 0, jnp.exp(log_gamma[:, None, None] * jnp.maximum(diff, 0.0)), 0.0)
    per_lane = lambda a: jnp.repeat(a, dk, axis=-1)
    qdec = per_lane(jnp.exp(log_gamma[None, :] * (i[:, None] + 1.0)))
    kdec = per_lane(jnp.exp(log_gamma[None, :] * (RET_BLK - 1.0 - i[:, None])))
    sdec = per_lane(jnp.exp(log_gamma * RET_BLK)[None, :])
    inv = ROPE_BASE ** (-jnp.arange(0, dk, 2, dtype=F32) / dk)
    ang = jnp.arange(tp, dtype=F32)[:, None] * inv[None, :]
    cos, sin = jnp.cos(ang), jnp.sin(ang)
    cos_full = jnp.tile(jnp.concatenate([cos, cos], axis=-1), (1, nh))
    sin_full = jnp.tile(jnp.concatenate([-sin, sin], axis=-1), (1, nh))
    return cos_full, sin_full, dmat, qdec, kdec, sdec


def _split_w_in(w):
    parts, acc = [], 0
    for size in IN_SIZES:
        parts.append(w[:, acc:acc + size])
        acc += size
    return parts


def kernel(x, meta, rel_bias, w_in, norm_mix, diff_lambda, diff_norm, ret_norm, w_out, norm_ff, w_ff1, w_ff2,
           final_norm):
    bsz, s_len, _ = x.shape
    depth = w_in.shape[0]
    t_len = s_len + N_META
    tp = -(-t_len // ATT_BLK) * ATT_BLK
    topk = min(TOPK_MAX, s_len // 4)
    nq = tp // ATT_BLK

    h = jnp.concatenate([
        jnp.broadcast_to(meta.astype(x.dtype)[None], (bsz, N_META, D_MODEL)), x,
        jnp.zeros((bsz, tp - t_len, D_MODEL), x.dtype)], axis=1).reshape(bsz * tp, D_MODEL)

    bias_a = _bias_tiles(rel_bias[:, :A_HEADS])
    bias_b = _bias_tiles(rel_bias[:, A_HEADS:])
    cos, sin, dmat, qdec, kdec, sdec = _retention_constants(tp)
    row = lambda v: v.reshape(1, -1).astype(F32)

    for l in range(depth):
        qa, ka, va, qi, ki, wi, qb, kb, vb, qc, kc, vc, gc = _split_w_in(w_in[l])
        w16 = jnp.concatenate([qa, ka, va, qi, qb, kb, vb, ki, ki], axis=1).astype(BF16)
        w32 = jnp.concatenate([qc, kc, vc, gc, wi, jnp.zeros((D_MODEL, LANES - IDX_HEADS), F32)],
                              axis=1).astype(BF16)
        p16, p32 = _inproj(h, row(norm_mix[l]), w16, w32)
        p16 = p16.reshape(bsz, tp, P16_COLS)
        p32 = p32.reshape(bsz, tp, P32_COLS)

        scores, tau, ntake, tie = _indexer(p16, p32, topk)
        tieflag = jnp.max(tie.reshape(bsz * nq, ATT_BLK), axis=1)

        lambda_init = 0.8 - 0.6 * math.exp(-0.3 * l)
        lp = diff_lambda[l].astype(F32)
        lam = jnp.exp(jnp.sum(lp[0] * lp[1])) - jnp.exp(jnp.sum(lp[2] * lp[3])) + lambda_init
        g_pair = row(jnp.concatenate([diff_norm[l], diff_norm[l]]))
        a_out, b_out = _attention(p16, scores, tau, ntake, tieflag, lam.reshape(1), bias_a, bias_b, g_pair,
                                  1.0 - lambda_init)

        c_out = _retention(p32, cos, sin, dmat, qdec, kdec, sdec, row(ret_norm[l]))

        last = l == depth - 1
        h = _mix_mlp(h, a_out.reshape(bsz * tp, A_W), b_out.reshape(bsz * tp, B_W), c_out.reshape(bsz * tp, C_W),
                     w_out[l].astype(BF16), row(norm_ff[l]), w_ff1[l].astype(BF16), w_ff2[l].astype(BF16),
                     row(final_norm), final_norm=last)

    return h.reshape(bsz, tp, D_MODEL)[:, N_META:t_len]
```

```python
import functools
import math

import numpy as np
import jax
import jax.numpy as jnp
from jax import lax
from jax.experimental import pallas as pl
from jax.experimental.pallas import tpu as pltpu

D_MODEL = 1024
N_META = 16
A_HEADS = 8
A_HEAD_DIM = 64
A_W = A_HEADS * A_HEAD_DIM
IDX_HEADS = 8
IDX_DIM = 64
TOPK_MAX = 256
B_HEADS = 4
B_HEAD_DIM = 64
B_HALF = 32
B_W = B_HEADS * B_HEAD_DIM
C_HEADS = 4
C_HEAD_DIM = 64
C_W = C_HEADS * C_HEAD_DIM
ROPE_BASE = 10000.0
REL_BUCKETS = 32
REL_MAX_DIST = 128
D_FF = 4 * D_MODEL
EPS = 1e-6
IN_SIZES = (A_W, A_W, A_W, IDX_HEADS * IDX_DIM, IDX_DIM, IDX_HEADS, B_W, B_W, B_W, C_W, C_W, C_W, C_W)

LANES = 128
ATT_BLK = 512
IDX_ROWS = 256
IDX_CHUNK = 512
COUNT_ROWS = 128
BISECT_FIXED_STEPS = 18
SMALLEST_NORMAL = float(np.finfo(np.float32).tiny)
RET_BLK = 256
ROW_TILE = 512
FF_CHUNK = 2048
VMEM_LIMIT_BYTES = 56 * 1024 * 1024

NEG = -1e30
LOG2E = math.log2(math.e)

P16_COLS = 4 * A_W + 3 * B_W + 2 * IDX_DIM
P32_COLS = 4 * C_W + LANES

BF16 = jnp.bfloat16
F32 = jnp.float32


def _cparams(sem):
    return pltpu.CompilerParams(dimension_semantics=sem, vmem_limit_bytes=VMEM_LIMIT_BYTES)


def _dot_nt(a, b):
    return lax.dot_general(a, b, (((1,), (1,)), ((), ())), preferred_element_type=F32)


def _dot(a, b):
    return jnp.dot(a, b, preferred_element_type=F32)


def _inproj_kernel(h_ref, g_ref, w16_ref, w32_ref, o16_ref, o32_ref):
    x = h_ref[...]
    ms = jnp.mean(x * x, axis=-1, keepdims=True)
    u = ((x * lax.rsqrt(ms + EPS)) * g_ref[...]).astype(BF16)
    o16_ref[...] = _dot(u, w16_ref[...]).astype(BF16)
    o32_ref[...] = _dot(u, w32_ref[...])


def _inproj(h2d, g, w16, w32):
    rows = h2d.shape[0]
    return pl.pallas_call(
        _inproj_kernel,
        grid=(rows // ROW_TILE,),
        in_specs=[
            pl.BlockSpec((ROW_TILE, D_MODEL), lambda i: (i, 0)),
            pl.BlockSpec((1, D_MODEL), lambda i: (0, 0)),
            pl.BlockSpec((D_MODEL, P16_COLS), lambda i: (0, 0)),
            pl.BlockSpec((D_MODEL, P32_COLS), lambda i: (0, 0)),
        ],
        out_specs=[
            pl.BlockSpec((ROW_TILE, P16_COLS), lambda i: (i, 0)),
            pl.BlockSpec((ROW_TILE, P32_COLS), lambda i: (i, 0)),
        ],
        out_shape=[
            jax.ShapeDtypeStruct((rows, P16_COLS), BF16),
            jax.ShapeDtypeStruct((rows, P32_COLS), F32),
        ],
        compiler_params=_cparams(("arbitrary",)),
        name="inproj",
    )(h2d, g, w16, w32)


def _indexer_kernel(qi_ref, ki_ref, wi_ref, sc_ref, tau_ref, ntake_ref, tie_ref, qm_ref, *, topk, n_chunks):
    i = pl.program_id(1)
    row0 = i * IDX_ROWS
    n_live = (row0 + IDX_ROWS - 1) // IDX_CHUNK + 1
    n_tiles = IDX_CHUNK // LANES
    k_f = float(topk)

    lane = lax.broadcasted_iota(jnp.int32, (IDX_ROWS, LANES), 1)
    for h in range(IDX_HEADS):
        pair = qi_ref[:, (h // 2) * LANES:(h // 2 + 1) * LANES]
        keep = (lane >= IDX_DIM) if h % 2 else (lane < IDX_DIM)
        qm_ref[h] = jnp.where(keep, pair, jnp.zeros_like(pair))

    w = wi_ref[...] * (IDX_HEADS ** -0.5 * IDX_DIM ** -0.5)
    wcol = [w[:, h:h + 1] for h in range(IDX_HEADS)]
    row = row0 + lax.broadcasted_iota(jnp.int32, (IDX_ROWS, 1), 0)
    col_in_chunk = lax.broadcasted_iota(jnp.int32, (IDX_ROWS, IDX_CHUNK), 1)

    def score_chunk(c, carry):
        lo, hi = carry
        start = pl.multiple_of(c * IDX_CHUNK, IDX_CHUNK)
        kc = ki_ref[pl.ds(start, IDX_CHUNK), :]
        acc = jnp.zeros((IDX_ROWS, IDX_CHUNK), F32)
        for h in range(IDX_HEADS):
            acc = acc + wcol[h] * jnp.maximum(_dot_nt(qm_ref[h], kc), 0.0)
        causal = col_in_chunk + start <= row
        sc = jnp.where(causal, acc, -jnp.inf)
        sc_ref[:, pl.ds(start, IDX_CHUNK)] = sc
        for_min = jnp.where(causal, acc, jnp.inf)
        for t in range(n_tiles):
            lo = jnp.minimum(lo, for_min[:, t * LANES:(t + 1) * LANES])
            hi = jnp.maximum(hi, sc[:, t * LANES:(t + 1) * LANES])
        return lo, hi

    lo_l, hi_l = lax.fori_loop(0, n_live, score_chunk, (jnp.full((IDX_ROWS, LANES), jnp.inf, F32),
                                                        jnp.full((IDX_ROWS, LANES), -jnp.inf, F32)))
    across = lambda col: jnp.broadcast_to(col, (IDX_ROWS, LANES))
    lo0 = across(jnp.min(lo_l, axis=1, keepdims=True))
    hi0 = across(jnp.max(hi_l, axis=1, keepdims=True))

    def fill_chunk(c, carry):
        start = pl.multiple_of(c * IDX_CHUNK, IDX_CHUNK)
        sc_ref[:, pl.ds(start, IDX_CHUNK)] = jnp.full((IDX_ROWS, IDX_CHUNK), -jnp.inf, F32)
        return carry

    lax.fori_loop(n_live, n_chunks, fill_chunk, 0)

    def count(thr, strict=False):
        parts = []
        for r0 in range(0, IDX_ROWS, COUNT_ROWS):
            thr_r = thr[r0:r0 + COUNT_ROWS]

            def body(c, acc, r0=r0, thr_r=thr_r):
                start = pl.multiple_of(c * IDX_CHUNK, IDX_CHUNK)
                sc = sc_ref[r0:r0 + COUNT_ROWS, pl.ds(start, IDX_CHUNK)]
                for t in range(n_tiles):
                    tile = sc[:, t * LANES:(t + 1) * LANES]
                    acc = jnp.where((tile > thr_r) if strict else (tile >= thr_r), acc + 1.0, acc)
                return acc

            acc = lax.fori_loop(0, n_live, body, jnp.zeros((COUNT_ROWS, LANES), F32))
            parts.append(jnp.broadcast_to(jnp.sum(acc, axis=1, keepdims=True), (COUNT_ROWS, LANES)))
        return jnp.concatenate(parts, axis=0)

    n_valid = (row0 + 1 + lax.broadcasted_iota(jnp.int32, (IDX_ROWS, LANES), 0)).astype(F32)
    keeps_all = n_valid <= k_f
    tied_at_max = count(hi0) >= k_f
    lo0 = jnp.where(keeps_all, jnp.float32(jnp.finfo(jnp.float32).min), jnp.where(tied_at_max, hi0, lo0))
    hi0 = jnp.where(keeps_all, lo0, hi0)

    def probe(lo, hi):
        mid = 0.5 * lo + 0.5 * hi
        mid = jnp.where((lo <= 0.0) & (hi > SMALLEST_NORMAL), SMALLEST_NORMAL, mid)
        mid = jnp.where((lo < 0.0) & (hi > 0.0) & (hi <= SMALLEST_NORMAL), 0.0, mid)
        return mid, (mid > lo) & (mid < hi)

    def unfinished(state):
        return jnp.sum(jnp.where(probe(*state)[1], 1.0, 0.0)) > 0.0

    def bisect(state):
        lo, hi = state
        mid, inside = probe(lo, hi)
        cnt = count(mid)
        hit = cnt == k_f
        lo = jnp.where(inside & (hit | (cnt > k_f)), mid, lo)
        hi = jnp.where(inside & (hit | (cnt < k_f)), mid, hi)
        return lo, hi

    state = lax.fori_loop(0, BISECT_FIXED_STEPS, lambda _, st: bisect(st), (lo0, hi0))
    tau, hi = lax.while_loop(unfinished, bisect, state)
    tau_ref[...] = tau[:, :1]
    ntake_ref[...] = jnp.zeros(ntake_ref.shape, F32)
    tie_ref[...] = jnp.zeros(tie_ref.shape, jnp.int32)

    may_tie = jnp.where((tau != hi) | tied_at_max, 1.0, 0.0)

    @pl.when(jnp.sum(may_tie) > 0.0)
    def _():
        n_gt = count(tau, strict=True)
        n_ge = count(tau)
        ntake_ref[...] = k_f - n_gt[:, :1]
        tie_ref[...] = (n_ge[:, :1] > k_f).astype(jnp.int32)


def _indexer(p16, p32, topk):
    bsz, tp, _ = p16.shape
    n_chunks = tp // IDX_CHUNK
    kern = functools.partial(_indexer_kernel, topk=topk, n_chunks=n_chunks)
    col = lambda n: pl.BlockSpec((None, IDX_ROWS, 1), lambda b, i: (b, i, 0))
    return pl.pallas_call(
        kern,
        grid=(bsz, tp // IDX_ROWS),
        in_specs=[
            pl.BlockSpec((None, IDX_ROWS, A_W), lambda b, i: (b, i, 3)),
            pl.BlockSpec((None, tp, LANES), lambda b, i: (b, 0, (P16_COLS - LANES) // LANES)),
            pl.BlockSpec((None, IDX_ROWS, LANES), lambda b, i: (b, i, 4 * C_W // LANES)),
        ],
        out_specs=[
            pl.BlockSpec((None, IDX_ROWS, tp), lambda b, i: (b, i, 0)),
            col(0), col(1), col(2),
        ],
        out_shape=[
            jax.ShapeDtypeStruct((bsz, tp, tp), F32),
            jax.ShapeDtypeStruct((bsz, tp, 1), F32),
            jax.ShapeDtypeStruct((bsz, tp, 1), F32),
            jax.ShapeDtypeStruct((bsz, tp, 1), jnp.int32),
        ],
        scratch_shapes=[pltpu.VMEM((IDX_HEADS, IDX_ROWS, LANES), BF16)],
        compiler_params=_cparams(("arbitrary", "arbitrary")),
        name="indexer",
    )(p16, p16, p32)


def _lane_masked_queries(q_ref, qm_ref, n_slots, width, scale2):
    per_group = LANES // width
    lane = lax.broadcasted_iota(jnp.int32, (ATT_BLK, LANES), 1)
    for s in range(n_slots):
        g, r = divmod(s, per_group)
        grp = q_ref[:, g * LANES:(g + 1) * LANES].astype(F32) * scale2
        keep = (lane >= r * width) & (lane < (r + 1) * width)
        qm_ref[s] = jnp.where(keep, grp, 0.0).astype(BF16)


def _flash_update(s, logit2, v_ones, m_ref, acc_ref):
    m_prev = m_ref[s]
    m_new = jnp.maximum(m_prev, jnp.max(logit2, axis=1, keepdims=True))
    alpha = jnp.exp2(m_prev - m_new)
    p = jnp.concatenate([jnp.exp2(logit2[:, t * LANES:(t + 1) * LANES] - m_new)
                         for t in range(logit2.shape[1] // LANES)], axis=1).astype(BF16)
    acc_ref[s] = jnp.concatenate([alpha, alpha], axis=1) * acc_ref[s] + _dot(p, v_ones)
    m_ref[s] = m_new


def _with_ones(v_grp):
    return jnp.concatenate([v_grp, jnp.ones(v_grp.shape, v_grp.dtype)], axis=1)


def _flash_init(m_ref, acc_ref):
    m_ref[...] = jnp.full(m_ref.shape, NEG, F32)
    acc_ref[...] = jnp.zeros(acc_ref.shape, F32)


def _flash_result(s, acc_ref):
    acc = acc_ref[s]
    return acc[:, :LANES] / acc[:, LANES:]


def _triangle_steps(nq):
    qs, ks, kinds = [], [], []
    for q in range(nq):
        for k in range(q + 1):
            qs.append(q)
            ks.append(k)
            kinds.append(2 if k == q else (1 if k == q - 1 else 0))
    return (jnp.asarray(qs, jnp.int32), jnp.asarray(ks, jnp.int32), jnp.asarray(kinds, jnp.int32))


def _sparse_attn_steps(has_ties, q_ref, k_ref, v_ref, sc_ref, tau_ref, ntake_ref, bias_ref,
                       o_ref, qm_ref, m_ref, acc_ref, mask_ref, carry_ref):
    def init():
        _lane_masked_queries(q_ref, qm_ref, A_HEADS, A_HEAD_DIM, A_HEAD_DIM ** -0.5 * LOG2E)
        _flash_init(m_ref, acc_ref)
        carry_ref[...] = jnp.zeros(carry_ref.shape, F32)

    def select():
        @pl.when(has_ties == 0)
        def _():
            mask_ref[...] = jnp.where(sc_ref[...] >= tau_ref[...], 0.0, NEG)

        @pl.when(has_ties != 0)
        def _():
            sc = sc_ref[...]
            tau = tau_ref[...]
            eq = sc == tau
            eq_b = jnp.where(eq, 1.0, 0.0).astype(BF16)
            r = lax.broadcasted_iota(jnp.int32, (ATT_BLK, ATT_BLK), 0)
            c = lax.broadcasted_iota(jnp.int32, (ATT_BLK, ATT_BLK), 1)
            before = jnp.where(r < c, 1.0, 0.0).astype(BF16)
            seen = carry_ref[...] + _dot(eq_b, before)
            keep = (sc > tau) | (eq & (seen < ntake_ref[...]))
            mask_ref[...] = jnp.where(keep, 0.0, NEG)
            carry_ref[...] = carry_ref[...] + jnp.sum(eq_b.astype(F32), axis=1, keepdims=True)

    def attend(near_diagonal):
        for g in range(A_HEADS // 2):
            k_grp = k_ref[:, g * LANES:(g + 1) * LANES]
            v_ones = _with_ones(v_ref[:, g * LANES:(g + 1) * LANES])
            for h in (2 * g, 2 * g + 1):
                logit2 = _dot_nt(qm_ref[h], k_grp) + mask_ref[...]
                if near_diagonal:
                    logit2 = logit2 + bias_ref[h]
                _flash_update(h, logit2, v_ones, m_ref, acc_ref)

    def finish():
        lane = lax.broadcasted_iota(jnp.int32, (ATT_BLK, LANES), 1)
        for g in range(A_HEADS // 2):
            out = jnp.where(lane < A_HEAD_DIM, _flash_result(2 * g, acc_ref), _flash_result(2 * g + 1, acc_ref))
            o_ref[:, g * LANES:(g + 1) * LANES] = out.astype(o_ref.dtype)

    return init, select, attend, finish


def _diff_attn_steps(lam_ref, q_ref, k_ref, v_ref, bias_ref, g_ref, o_ref, qm_ref, m_ref, acc_ref, out_scale):
    n_slots = 2 * B_HEADS

    def init():
        _lane_masked_queries(q_ref, qm_ref, n_slots, B_HALF, B_HALF ** -0.5 * LOG2E)
        _flash_init(m_ref, acc_ref)

    def attend(near_diagonal):
        for g in range(B_HEADS // 2):
            k_grp = k_ref[:, g * LANES:(g + 1) * LANES]
            v_ones = _with_ones(v_ref[:, g * LANES:(g + 1) * LANES])
            for slot in range(4 * g, 4 * g + 4):
                logit2 = _dot_nt(qm_ref[slot], k_grp)
                if near_diagonal:
                    logit2 = logit2 + bias_ref[slot // 2]
                _flash_update(slot, logit2, v_ones, m_ref, acc_ref)

    def finish():
        lam = lam_ref[0]
        lane = lax.broadcasted_iota(jnp.int32, (ATT_BLK, LANES), 1)
        low = lane < B_HEAD_DIM
        for g in range(B_HEADS // 2):
            heads = []
            for h in (2 * g, 2 * g + 1):
                heads.append(_flash_result(2 * h, acc_ref) - lam * _flash_result(2 * h + 1, acc_ref))
            x = jnp.where(low, heads[0], heads[1])
            sq = x * x
            ss_lo = jnp.sum(jnp.where(low, sq, 0.0), axis=1, keepdims=True)
            ss_hi = jnp.sum(jnp.where(low, 0.0, sq), axis=1, keepdims=True)
            ms = jnp.where(low, ss_lo, ss_hi) * (1.0 / B_HEAD_DIM)
            y = (x * lax.rsqrt(ms + EPS)) * g_ref[...]
            o_ref[:, g * LANES:(g + 1) * LANES] = (y * out_scale).astype(o_ref.dtype)

    return init, attend, finish


def _attention_kernel(qmap, kmap, kind, tieflag, lam_ref,
                      qa_ref, ka_ref, va_ref, sc_ref, tau_ref, ntake_ref, bias_a_ref,
                      qb_ref, kb_ref, vb_ref, bias_b_ref, gb_ref,
                      oa_ref, ob_ref,
                      qma_ref, ma_ref, acca_ref, mask_ref, carry_ref, qmb_ref, mb_ref, accb_ref, *, nq, out_scale):
    b = pl.program_id(0)
    s = pl.program_id(1)
    q_blk = qmap[s]
    k_blk = kmap[s]
    init_a, select_a, attend_a, finish_a = _sparse_attn_steps(
        tieflag[b * nq + q_blk], qa_ref, ka_ref, va_ref, sc_ref, tau_ref, ntake_ref, bias_a_ref,
        oa_ref, qma_ref, ma_ref, acca_ref, mask_ref, carry_ref)
    init_b, attend_b, finish_b = _diff_attn_steps(
        lam_ref, qb_ref, kb_ref, vb_ref, bias_b_ref, gb_ref, ob_ref, qmb_ref, mb_ref, accb_ref, out_scale)

    @pl.when(k_blk == 0)
    def _():
        init_a()
        init_b()

    select_a()

    @pl.when(kind[s] == 0)
    def _():
        attend_a(False)
        attend_b(False)

    @pl.when(kind[s] != 0)
    def _():
        attend_a(True)
        attend_b(True)

    @pl.when(k_blk == q_blk)
    def _():
        finish_a()
        finish_b()


def _attention(p16, scores, tau, ntake, tieflag, lam, bias_a, bias_b, g_pair, out_scale):
    bsz, tp, _ = p16.shape
    nq = tp // ATT_BLK
    qmap, kmap, kind = _triangle_steps(nq)
    base_b = 4 * A_W // B_W
    at_q = lambda w, c: pl.BlockSpec((None, ATT_BLK, w), lambda b, s, qm, km, kd, tf, lm: (b, qm[s], c))
    at_k = lambda w, c: pl.BlockSpec((None, ATT_BLK, w), lambda b, s, qm, km, kd, tf, lm: (b, km[s], c))
    bias = lambda heads: pl.BlockSpec((None, heads, ATT_BLK, ATT_BLK),
                                      lambda b, s, qm, km, kd, tf, lm: (jnp.maximum(kd[s] - 1, 0), 0, 0, 0))
    n_slots = 2 * B_HEADS
    grid_spec = pltpu.PrefetchScalarGridSpec(
        num_scalar_prefetch=5,
        grid=(bsz, int(qmap.shape[0])),
        in_specs=[
            at_q(A_W, 0), at_k(A_W, 1), at_k(A_W, 2),
            pl.BlockSpec((None, ATT_BLK, ATT_BLK), lambda b, s, qm, km, kd, tf, lm: (b, qm[s], km[s])),
            at_q(1, 0), at_q(1, 0),
            bias(A_HEADS),
            at_q(B_W, base_b), at_k(B_W, base_b + 1), at_k(B_W, base_b + 2),
            bias(B_HEADS),
            pl.BlockSpec((1, LANES), lambda b, s, qm, km, kd, tf, lm: (0, 0)),
        ],
        out_specs=[at_q(A_W, 0), at_q(B_W, 0)],
        scratch_shapes=[
            pltpu.VMEM((A_HEADS, ATT_BLK, LANES), BF16),
            pltpu.VMEM((A_HEADS, ATT_BLK, LANES), F32),
            pltpu.VMEM((A_HEADS, ATT_BLK, 2 * LANES), F32),
            pltpu.VMEM((ATT_BLK, ATT_BLK), F32),
            pltpu.VMEM((ATT_BLK, 1), F32),
            pltpu.VMEM((n_slots, ATT_BLK, LANES), BF16),
            pltpu.VMEM((n_slots, ATT_BLK, LANES), F32),
            pltpu.VMEM((n_slots, ATT_BLK, 2 * LANES), F32),
        ],
    )
    return pl.pallas_call(
        functools.partial(_attention_kernel, nq=nq, out_scale=out_scale),
        grid_spec=grid_spec,
        out_shape=[jax.ShapeDtypeStruct((bsz, tp, A_W), BF16), jax.ShapeDtypeStruct((bsz, tp, B_W), BF16)],
        compiler_params=_cparams(("arbitrary", "arbitrary")),
        name="attention",
    )(qmap, kmap, kind, tieflag, lam, p16, p16, p16, scores, tau, ntake, bias_a, p16, p16, p16, bias_b, g_pair)


def _retention_kernel(q_ref, k_ref, v_ref, gate_ref, cos_ref, sin_ref, dmat_ref, qdec_ref, kdec_ref, sdec_ref,
                      g_ref, o_ref, state_ref):
    c = pl.program_id(1)

    @pl.when(c == 0)
    def _():
        state_ref[...] = jnp.zeros(state_ref.shape, F32)

    lane = lax.broadcasted_iota(jnp.int32, (RET_BLK, C_W), 1)
    first_half = (lane % C_HEAD_DIM) < (C_HEAD_DIM // 2)

    def rope(x):
        swapped = jnp.where(first_half,
                            pltpu.roll(x, C_W - C_HEAD_DIM // 2, axis=1),
                            pltpu.roll(x, C_HEAD_DIM // 2, axis=1))
        return x * cos_ref[...] + swapped * sin_ref[...]

    q = rope(q_ref[...])
    k = rope(k_ref[...]) * (C_HEAD_DIM ** -0.5)
    q16 = q.astype(BF16)
    k16 = k.astype(BF16)
    v16 = v_ref[...].astype(BF16)

    state = state_ref[...]
    out = _dot(q16, state.astype(BF16)) * qdec_ref[...]
    for h in range(C_HEADS):
        in_head = (lane >= h * C_HEAD_DIM) & (lane < (h + 1) * C_HEAD_DIM)
        qh = jnp.where(in_head, q, 0.0).astype(BF16)
        inner = _dot_nt(qh, k16) * dmat_ref[h]
        out = out + jnp.where(in_head, _dot(inner.astype(BF16), v16), 0.0)

    kd = (k * kdec_ref[...]).astype(BF16)
    update = _dot(kd.T, v16)
    r = lax.broadcasted_iota(jnp.int32, (C_W, C_W), 0) // C_HEAD_DIM
    cc = lax.broadcasted_iota(jnp.int32, (C_W, C_W), 1) // C_HEAD_DIM
    state_ref[...] = state * sdec_ref[...] + jnp.where(r == cc, update, 0.0)

    sq = out * out
    ms = jnp.zeros_like(out)
    for h in range(C_HEADS):
        in_head = (lane >= h * C_HEAD_DIM) & (lane < (h + 1) * C_HEAD_DIM)
        ss = jnp.sum(jnp.where(in_head, sq, 0.0), axis=1, keepdims=True)
        ms = jnp.where(in_head, ss * (1.0 / C_HEAD_DIM), ms)
    normed = (out * lax.rsqrt(ms + EPS)) * g_ref[...]
    gate = gate_ref[...]
    o_ref[...] = ((gate * jax.nn.sigmoid(gate)) * normed).astype(o_ref.dtype)


def _retention(p32, cos, sin, dmat, qdec, kdec, sdec, g):
    bsz, tp, _ = p32.shape
    blk = lambda c: pl.BlockSpec((None, RET_BLK, C_W), lambda b, i: (b, i, c))
    const = lambda shape: pl.BlockSpec(shape, lambda b, i: (0,) * len(shape))
    return pl.pallas_call(
        _retention_kernel,
        grid=(bsz, tp // RET_BLK),
        in_specs=[
            blk(0), blk(1), blk(2), blk(3),
            pl.BlockSpec((RET_BLK, C_W), lambda b, i: (i, 0)),
            pl.BlockSpec((RET_BLK, C_W), lambda b, i: (i, 0)),
            const((C_HEADS, RET_BLK, RET_BLK)),
            const((RET_BLK, C_W)), const((RET_BLK, C_W)), const((1, C_W)), const((1, C_W)),
        ],
        out_specs=pl.BlockSpec((None, RET_BLK, C_W), lambda b, i: (b, i, 0)),
        out_shape=jax.ShapeDtypeStruct((bsz, tp, C_W), BF16),
        scratch_shapes=[pltpu.VMEM((C_W, C_W), F32)],
        compiler_params=_cparams(("arbitrary", "arbitrary")),
        name="retention",
    )(p32, p32, p32, p32, cos, sin, dmat, qdec, kdec, sdec, g)


def _mix_mlp_kernel(h_ref, a_ref, b_ref, c_ref, wo_ref, g_ref, w1_ref, w2_ref, gf_ref, o_ref, u_ref, acc_ref,
                    *, final_norm):
    j = pl.program_id(1)

    @pl.when(j == 0)
    def _():
        mixed = (_dot(a_ref[...], wo_ref[0:A_W, :])
                 + _dot(b_ref[...], wo_ref[A_W:A_W + B_W, :])
                 + _dot(c_ref[...], wo_ref[A_W + B_W:, :]))
        h1 = h_ref[...] + mixed
        acc_ref[...] = h1
        ms = jnp.mean(h1 * h1, axis=-1, keepdims=True)
        u_ref[...] = ((h1 * lax.rsqrt(ms + EPS)) * g_ref[...]).astype(BF16)

    t = jnp.maximum(_dot(u_ref[...], w1_ref[...]), 0.0)
    acc_ref[...] += _dot((t * t).astype(BF16), w2_ref[...])

    @pl.when(j == pl.num_programs(1) - 1)
    def _():
        y = acc_ref[...]
        if final_norm:
            ms = jnp.mean(y * y, axis=-1, keepdims=True)
            y = (y * lax.rsqrt(ms + EPS)) * gf_ref[...]
        o_ref[...] = y


def _mix_mlp(h2d, a, b, c, wo, g, w1, w2, gf, final_norm):
    rows = h2d.shape[0]
    row_blk = lambda w: pl.BlockSpec((ROW_TILE, w), lambda i, j: (i, 0))
    vec = pl.BlockSpec((1, D_MODEL), lambda i, j: (0, 0))
    return pl.pallas_call(
        functools.partial(_mix_mlp_kernel, final_norm=final_norm),
        grid=(rows // ROW_TILE, D_FF // FF_CHUNK),
        in_specs=[
            row_blk(D_MODEL), row_blk(A_W), row_blk(B_W), row_blk(C_W),
            pl.BlockSpec((D_MODEL, D_MODEL), lambda i, j: (0, 0)),
            vec,
            pl.BlockSpec((D_MODEL, FF_CHUNK), lambda i, j: (0, j)),
            pl.BlockSpec((FF_CHUNK, D_MODEL), lambda i, j: (j, 0)),
            vec,
        ],
        out_specs=row_blk(D_MODEL),
        out_shape=jax.ShapeDtypeStruct((rows, D_MODEL), F32),
        scratch_shapes=[pltpu.VMEM((ROW_TILE, D_MODEL), BF16), pltpu.VMEM((ROW_TILE, D_MODEL), F32)],
        compiler_params=_cparams(("arbitrary", "arbitrary")),
        name="mix_mlp",
    )(h2d, a, b, c, wo, g, w1, w2, gf)


def _t5_bucket(dist):
    n = jnp.maximum(dist, 0)
    max_exact = REL_BUCKETS // 2
    nf = jnp.maximum(n, 1).astype(F32)
    large = max_exact + (jnp.log(nf / max_exact) / math.log(REL_MAX_DIST / max_exact)
                         * (REL_BUCKETS - max_exact)).astype(jnp.int32)
    large = jnp.minimum(large, REL_BUCKETS - 1)
    return jnp.where(n < max_exact, n, large)


def _far_bucket_from():
    max_exact = REL_BUCKETS // 2
    n = np.arange(1, 4 * REL_MAX_DIST, dtype=np.float32)
    large = max_exact + (np.log(n / max_exact) / math.log(REL_MAX_DIST / max_exact)
                         * (REL_BUCKETS - max_exact)).astype(np.int32)
    bucket = np.where(n < max_exact, n, np.minimum(large, REL_BUCKETS - 1))
    return int(np.nonzero(bucket < REL_BUCKETS - 1)[0].max()) + 2


_FAR_BUCKET_FROM = _far_bucket_from()


def _bias_tiles(table):
    i = jnp.arange(ATT_BLK)[:, None]
    j = jnp.arange(ATT_BLK)[None, :]
    assert _FAR_BUCKET_FROM <= ATT_BLK + 1
    far = table[REL_BUCKETS - 1].astype(F32)
    table2 = (table.astype(F32) - far[None, :]).T * LOG2E
    tiles = []
    for offset in (ATT_BLK, 0):
        dist = i - j + offset
        bucket = _t5_bucket(dist)
        t = jnp.zeros((table2.shape[0], ATT_BLK, ATT_BLK), F32)
        for b in range(REL_BUCKETS):
            t = jnp.where(bucket[None] == b, table2[:, b, None, None], t)
        tiles.append(jnp.where(dist >= 0, t, NEG))
    return jnp.stack(tiles)


def _retention_constants(tp):
    nh, dk = C_HEADS, C_HEAD_DIM
    log_gamma = jnp.log1p(-jnp.exp2(-5.0 - jnp.arange(nh, dtype=F32)))
    i = jnp.arange(RET_BLK, dtype=F32)
    diff = i[:, None] - i[None, :]
    dmat = jnp.where(diff >= 0, jnp.exp(log_gamma[:, None, None] * jnp.maximum(diff, 0.0)), 0.0)
    per_lane = lambda a: jnp.repeat(a, dk, axis=-1)
    qdec = per_lane(jnp.exp(log_gamma[None, :] * (i[:, None] + 1.0)))
    kdec = per_lane(jnp.exp(log_gamma[None, :] * (RET_BLK - 1.0 - i[:, None])))
    sdec = per_lane(jnp.exp(log_gamma * RET_BLK)[None, :])
    inv = ROPE_BASE ** (-jnp.arange(0, dk, 2, dtype=F32) / dk)
    ang = jnp.arange(tp, dtype=F32)[:, None] * inv[None, :]
    cos, sin = jnp.cos(ang), jnp.sin(ang)
    cos_full = jnp.tile(jnp.concatenate([cos, cos], axis=-1), (1, nh))
    sin_full = jnp.tile(jnp.concatenate([-sin, sin], axis=-1), (1, nh))
    return cos_full, sin_full, dmat, qdec, kdec, sdec


def _split_w_in(w):
    parts, acc = [], 0
    for size in IN_SIZES:
        parts.append(w[:, acc:acc + size])
        acc += size
    return parts


def kernel(x, meta, rel_bias, w_in, norm_mix, diff_lambda, diff_norm, ret_norm, w_out, norm_ff, w_ff1, w_ff2,
           final_norm):
    bsz, s_len, _ = x.shape
    depth = w_in.shape[0]
    t_len = s_len + N_META
    tp = -(-t_len // ATT_BLK) * ATT_BLK
    topk = min(TOPK_MAX, s_len // 4)
    nq = tp // ATT_BLK

    h = jnp.concatenate([
        jnp.broadcast_to(meta.astype(x.dtype)[None], (bsz, N_META, D_MODEL)), x,
        jnp.zeros((bsz, tp - t_len, D_MODEL), x.dtype)], axis=1).reshape(bsz * tp, D_MODEL)

    bias_a = _bias_tiles(rel_bias[:, :A_HEADS])
    bias_b = _bias_tiles(rel_bias[:, A_HEADS:])
    cos, sin, dmat, qdec, kdec, sdec = _retention_constants(tp)
    row = lambda v: v.reshape(1, -1).astype(F32)

    for l in range(depth):
        qa, ka, va, qi, ki, wi, qb, kb, vb, qc, kc, vc, gc = _split_w_in(w_in[l])
        w16 = jnp.concatenate([qa, ka, va, qi, qb, kb, vb, ki, ki], axis=1).astype(BF16)
        w32 = jnp.concatenate([qc, kc, vc, gc, wi, jnp.zeros((D_MODEL, LANES - IDX_HEADS), F32)],
                              axis=1).astype(BF16)
        p16, p32 = _inproj(h, row(norm_mix[l]), w16, w32)
        p16 = p16.reshape(bsz, tp, P16_COLS)
        p32 = p32.reshape(bsz, tp, P32_COLS)

        scores, tau, ntake, tie = _indexer(p16, p32, topk)
        tieflag = jnp.max(tie.reshape(bsz * nq, ATT_BLK), axis=1)

        lambda_init = 0.8 - 0.6 * math.exp(-0.3 * l)
        lp = diff_lambda[l].astype(F32)
        lam = jnp.exp(jnp.sum(lp[0] * lp[1])) - jnp.exp(jnp.sum(lp[2] * lp[3])) + lambda_init
        g_pair = row(jnp.concatenate([diff_norm[l], diff_norm[l]]))
        a_out, b_out = _attention(p16, scores, tau, ntake, tieflag, lam.reshape(1), bias_a, bias_b, g_pair,
                                  1.0 - lambda_init)

        c_out = _retention(p32, cos, sin, dmat, qdec, kdec, sdec, row(ret_norm[l]))

        last = l == depth - 1
        h = _mix_mlp(h, a_out.reshape(bsz * tp, A_W), b_out.reshape(bsz * tp, B_W), c_out.reshape(bsz * tp, C_W),
                     w_out[l].astype(BF16), row(norm_ff[l]), w_ff1[l].astype(BF16), w_ff2[l].astype(BF16),
                     row(final_norm), final_norm=last)

    return h.reshape(bsz, tp, D_MODEL)[:, N_META:t_len]
```
